```python
import jax, jax.numpy as jnp
from jax import lax
import numpy as np

D_MODEL = 2048
BATCH = 4
SEQ = 4096
DEPTH = 2

EPS = 1e-6
GM_WIDTH = D_MODEL
GM_CHUNK = 128
GM_GROUP_CH = 128
GM_GROUPS = GM_WIDTH // GM_GROUP_CH
GLA_HEADS = 4
GLA_DK = D_MODEL // 2
GLA_DV = D_MODEL
GLA_HK = GLA_DK // GLA_HEADS
GLA_HV = GLA_DV // GLA_HEADS
GLA_RANK = 16
GLA_TAU = 16.0
GLA_CHUNK = 64
MOE_GROUPS = 4
MOE_PER_GROUP = 8
N_EXPERTS = MOE_GROUPS * MOE_PER_GROUP
TOP_K = 2
D_EXPERT = D_MODEL // 2
MOE_BLOCK = 128
IN_WIDTHS = (GM_WIDTH, GM_WIDTH, GLA_DK, GLA_DK, GLA_DV, GLA_DV, GLA_RANK, D_MODEL, D_MODEL)
IN_TOTAL = sum(IN_WIDTHS)

kernel_name = 'hybrid_gmlp_gla_hmoe_adaln'


def rms_norm(x, g):
    xf = x.astype(jnp.float32)
    y = xf * lax.rsqrt(jnp.mean(xf * xf, axis=-1, keepdims=True) + EPS)
    return (y * g.astype(jnp.float32)).astype(x.dtype)


def layer_norm(x, g, b):
    xf = x.astype(jnp.float32)
    mu = jnp.mean(xf, axis=-1, keepdims=True)
    xc = xf - mu
    y = xc * lax.rsqrt(jnp.mean(xc * xc, axis=-1, keepdims=True) + EPS)
    return (y * g.astype(jnp.float32) + b.astype(jnp.float32)).astype(x.dtype)


def gmlp_spatial_gate(u, v, vn_g, vn_b, w_s, b_s):
    B, S, _ = v.shape
    v = layer_norm(v, vn_g, vn_b)
    vc = v.reshape(B, S // GM_CHUNK, GM_CHUNK, GM_GROUPS, GM_GROUP_CH)
    causal = jnp.tril(jnp.ones((GM_CHUNK, GM_CHUNK), dtype=bool))
    w = jnp.where(causal[None], w_s, jnp.zeros_like(w_s))
    mixed = jnp.einsum('gts,bnsgc->bntgc', w, vc) + b_s.T[None, None, :, :, None]
    return u * mixed.reshape(B, S, GM_WIDTH)


def gla_chunked(q, k, v, log_a):
    B, S, H, dk = q.shape
    dv = v.shape[-1]
    C = GLA_CHUNK
    N = S // C
    f32 = jnp.float32
    q = q.astype(f32).reshape(B, N, C, H, dk)
    k = k.astype(f32).reshape(B, N, C, H, dk)
    v = v.astype(f32).reshape(B, N, C, H, dv)
    cum = jnp.cumsum(log_a.astype(f32).reshape(B, N, C, H, dk), axis=2)
    cum_last = cum[:, :, -1]
    q_dec = q * jnp.exp(cum)
    k_inv = k * jnp.exp(-cum)
    k_end = k * jnp.exp(cum_last[:, :, None] - cum)
    causal = jnp.tril(jnp.ones((C, C), dtype=bool))
    scores = jnp.einsum('bnthd,bnshd->bnhts', q_dec, k_inv)
    scores = jnp.where(causal, scores, jnp.zeros_like(scores))
    o_intra = jnp.einsum('bnhts,bnshe->bnthe', scores, v)

    def chunk_step(state, inp):
        qd, ke, vv, dl = inp
        o = jnp.einsum('bthd,bhde->bthe', qd, state)
        state = state * jnp.exp(dl)[..., None] + jnp.einsum('bshd,bshe->bhde', ke, vv)
        return state, o

    init = jnp.zeros((B, H, dk, dv), f32)
    xs = (jnp.moveaxis(q_dec, 1, 0), jnp.moveaxis(k_end, 1, 0),
          jnp.moveaxis(v, 1, 0), jnp.moveaxis(cum_last, 1, 0))
    _, o_inter = lax.scan(chunk_step, init, xs)
    o = o_intra + jnp.moveaxis(o_inter, 0, 1)
    return o.reshape(B, S, H, dv)


def hybrid_mixer(h, w_in, gm_vn_g, gm_vn_b, gm_ws, gm_bs, gla_wa2, gla_ba, gla_on_g,
                 w_pa, w_pb, w_out):
    B, S, _ = h.shape
    proj = jnp.matmul(h, w_in)
    offsets = np.cumsum(IN_WIDTHS)[:-1].tolist()
    u, v, q, k, vg, r, a_lr, ga, gb = jnp.split(proj, offsets, axis=-1)
    y_a = gmlp_spatial_gate(jax.nn.gelu(u, approximate=False), jax.nn.gelu(v, approximate=False),
                            gm_vn_g, gm_vn_b, gm_ws, gm_bs)
    z = (jnp.matmul(a_lr, gla_wa2) + gla_ba).astype(jnp.float32)
    log_a = jax.nn.log_sigmoid(z) / GLA_TAU
    qh = q.reshape(B, S, GLA_HEADS, GLA_HK) * (GLA_HK ** -0.5)
    kh = k.reshape(B, S, GLA_HEADS, GLA_HK)
    vh = vg.reshape(B, S, GLA_HEADS, GLA_HV)
    o = gla_chunked(qh, kh, vh, log_a.reshape(B, S, GLA_HEADS, GLA_HK)).astype(h.dtype)
    o = rms_norm(o, gla_on_g.reshape(GLA_HEADS, GLA_HV)).reshape(B, S, GLA_DV)
    y_b = o * jax.nn.silu(r)
    y = jax.nn.sigmoid(ga) * jnp.matmul(y_a, w_pa) + jax.nn.sigmoid(gb) * jnp.matmul(y_b, w_pb)
    return jnp.matmul(y, w_out)


def hier_moe(h, w_rg, b_rg, w_re, b_re, w_e_gate, w_e_up, w_e_down):
    B, S, D = h.shape
    T = B * S
    ht = h.reshape(T, D)
    g_logits = jnp.matmul(ht, w_rg).astype(jnp.float32) + b_rg.astype(jnp.float32)
    g_prob = jax.nn.softmax(g_logits, axis=-1)
    g_top, g_idx = lax.top_k(g_prob, 1)
    e_all = (jnp.matmul(ht, w_re).astype(jnp.float32) + b_re.astype(jnp.float32))
    e_all = e_all.reshape(T, MOE_GROUPS, MOE_PER_GROUP)
    e_logits = jnp.take_along_axis(e_all, g_idx[:, :, None], axis=1)[:, 0]
    e_prob = jax.nn.softmax(e_logits, axis=-1)
    top_p, top_i = lax.top_k(e_prob, TOP_K)
    weights = g_top * top_p / jnp.sum(top_p, axis=-1, keepdims=True)
    expert_id = g_idx * MOE_PER_GROUP + top_i
    n_assign = T * TOP_K
    flat_e = expert_id.reshape(-1).astype(jnp.int32)
    flat_w = weights.reshape(-1).astype(h.dtype)
    flat_tok = jnp.repeat(jnp.arange(T, dtype=jnp.int32), TOP_K)
    order = jnp.argsort(flat_e)
    se, stok, sw = flat_e[order], flat_tok[order], flat_w[order]
    counts = jnp.bincount(flat_e, length=N_EXPERTS)
    padded = (counts + MOE_BLOCK - 1) // MOE_BLOCK * MOE_BLOCK
    pad_end = jnp.cumsum(padded)
    pad_start = pad_end - padded
    raw_start = jnp.cumsum(counts) - counts
    dest = pad_start[se] + jnp.arange(n_assign, dtype=jnp.int32) - raw_start[se]
    n_rows = (n_assign + N_EXPERTS * (MOE_BLOCK - 1) + MOE_BLOCK - 1) // MOE_BLOCK * MOE_BLOCK
    n_blocks = n_rows // MOE_BLOCK
    row_tok = jnp.zeros((n_rows,), jnp.int32).at[dest].set(stok)
    row_w = jnp.zeros((n_rows,), h.dtype).at[dest].set(sw)
    block_start = jnp.arange(n_blocks, dtype=jnp.int32) * MOE_BLOCK
    block_e = jnp.minimum(jnp.searchsorted(pad_end, block_start, side='right'), N_EXPERTS - 1)
    xb = ht[row_tok].reshape(n_blocks, MOE_BLOCK, D)

    def expert_block(args):
        xblk, e = args
        hid = jax.nn.silu(jnp.matmul(xblk, w_e_gate[e])) * jnp.matmul(xblk, w_e_up[e])
        return jnp.matmul(hid, w_e_down[e])

    yb = lax.map(expert_block, (xb, block_e))
    y = jax.ops.segment_sum(yb.reshape(n_rows, D) * row_w[:, None], row_tok, num_segments=T)
    return y.reshape(B, S, D)


def setup_inputs(seed: int = 0) -> dict:
    key = jax.random.key(seed)
    ks = jax.random.split(key, 26)
    f32 = jnp.float32
    L, D = DEPTH, D_MODEL

    def nrm(k, shape, scale):
        return jax.random.normal(k, shape, f32) * scale

    return {
        'x': nrm(ks[0], (BATCH, SEQ, D), 1.0),
        'c': nrm(ks[1], (BATCH, D), 1.0),
        'ada_w': nrm(ks[2], (L, D, 6 * D), 0.5 * D ** -0.5),
        'ada_b': nrm(ks[3], (L, 6 * D), 0.02),
        'norm1_g': 1.0 + nrm(ks[4], (L, D), 0.02),
        'w_in': nrm(ks[5], (L, D, IN_TOTAL), D ** -0.5),
        'gm_vn_g': 1.0 + nrm(ks[6], (L, GM_WIDTH), 0.02),
        'gm_vn_b': nrm(ks[7], (L, GM_WIDTH), 0.02),
        'gm_ws': nrm(ks[8], (L, GM_GROUPS, GM_CHUNK, GM_CHUNK), GM_CHUNK ** -0.5),
        'gm_bs': 1.0 + nrm(ks[9], (L, GM_GROUPS, GM_CHUNK), 0.02),
        'gla_wa2': nrm(ks[10], (L, GLA_RANK, GLA_DK), GLA_RANK ** -0.5),
        'gla_ba': nrm(ks[11], (L, GLA_DK), 0.1),
        'gla_on_g': 1.0 + nrm(ks[12], (L, GLA_DV), 0.02),
        'w_pa': nrm(ks[13], (L, GM_WIDTH, D), GM_WIDTH ** -0.5),
        'w_pb': nrm(ks[14], (L, GLA_DV, D), GLA_DV ** -0.5),
        'w_out': nrm(ks[15], (L, D, D), D ** -0.5),
        'norm2_g': 1.0 + nrm(ks[16], (L, D), 0.02),
        'w_rg': nrm(ks[17], (L, D, MOE_GROUPS), D ** -0.5),
        'b_rg': nrm(ks[18], (L, MOE_GROUPS), 0.01),
        'w_re': nrm(ks[19], (L, D, N_EXPERTS), D ** -0.5),
        'b_re': nrm(ks[20], (L, N_EXPERTS), 0.01),
        'w_e_gate': nrm(ks[21], (L, N_EXPERTS, D, D_EXPERT), D ** -0.5),
        'w_e_up': nrm(ks[22], (L, N_EXPERTS, D, D_EXPERT), D ** -0.5),
        'w_e_down': nrm(ks[23], (L, N_EXPERTS, D_EXPERT, D), D_EXPERT ** -0.5),
        'final_g': 1.0 + nrm(ks[24], (D,), 0.02),
    }


def reference(x, c, ada_w, ada_b, norm1_g, w_in, gm_vn_g, gm_vn_b, gm_ws, gm_bs,
              gla_wa2, gla_ba, gla_on_g, w_pa, w_pb, w_out, norm2_g, w_rg, b_rg,
              w_re, b_re, w_e_gate, w_e_up, w_e_down, final_g):
    cond = jax.nn.silu(c)
    for l in range(DEPTH):
        mod = jnp.matmul(cond, ada_w[l]) + ada_b[l]
        sh1, sc1, g1, sh2, sc2, g2 = jnp.split(mod[:, None, :], 6, axis=-1)
        h = rms_norm(x, norm1_g[l]) * (1 + sc1) + sh1
        x = x + g1 * hybrid_mixer(h, w_in[l], gm_vn_g[l], gm_vn_b[l], gm_ws[l], gm_bs[l],
                                  gla_wa2[l], gla_ba[l], gla_on_g[l], w_pa[l], w_pb[l], w_out[l])
        h = rms_norm(x, norm2_g[l]) * (1 + sc2) + sh2
        x = x + g2 * hier_moe(h, w_rg[l], b_rg[l], w_re[l], b_re[l],
                              w_e_gate[l], w_e_up[l], w_e_down[l])
    return rms_norm(x, final_g)
```

```python
import functools

import jax
import jax.numpy as jnp
from jax import lax
from jax.experimental import pallas as pl
from jax.experimental.pallas import tpu as pltpu

F32 = jnp.float32
BF16 = jnp.bfloat16

EPS = 1e-6
GLA_HEADS = 4
GLA_TAU = 16.0
GLA_CHUNK = 64
MOE_GROUPS = 4
TOP_K = 2

LANES = 128
MIB = 1024 * 1024
MOE_BLOCK = 256
GATHER_ROWS = 256

NT_DIMS = (((1,), (1,)), ((), ()))
TN_DIMS = (((0,), (0,)), ((), ()))


def _params(semantics, vmem_mib):
    return pltpu.CompilerParams(dimension_semantics=semantics,
                                vmem_limit_bytes=int(vmem_mib * MIB))


def _dot(a, b):
    return jnp.dot(a, b, preferred_element_type=F32)


def _sigmoid(x):
    return 1.0 / (1.0 + jnp.exp(-x))


def _gelu(x):
    return 0.5 * x * (1.0 + lax.erf(x * (2.0 ** -0.5)))


def _ada_kernel(c_ref, w_ref, b_ref, o_ref):
    c = c_ref[...]
    cond = c * _sigmoid(c)
    o_ref[0] = _dot(cond.astype(BF16), w_ref[0].astype(BF16)) + b_ref[0]


def _ada_call(c8, ada_w, ada_b3):
    L, D, N = ada_w.shape
    tn = min(1024, N)
    return pl.pallas_call(
        _ada_kernel,
        grid=(L, N // tn),
        in_specs=[pl.BlockSpec((8, D), lambda l, j: (0, 0)),
                  pl.BlockSpec((1, D, tn), lambda l, j: (l, 0, j)),
                  pl.BlockSpec((1, 1, tn), lambda l, j: (l, 0, j))],
        out_specs=pl.BlockSpec((1, 8, tn), lambda l, j: (l, 0, j)),
        out_shape=jax.ShapeDtypeStruct((L, 8, N), F32),
        compiler_params=_params(("parallel", "parallel"), 40),
        name="ada_mod",
    )(c8, ada_w, ada_b3)


def _inproj_kernel(x_ref, g_ref, sc_ref, sh_ref, w_ref, walr_ref, o_ref, alr_ref, h_scr):
    @pl.when(pl.program_id(1) == 0)
    def _():
        x = x_ref[...]
        ms = jnp.mean(x * x, axis=-1, keepdims=True)
        y = x * lax.rsqrt(ms + EPS) * g_ref[...]
        h = (y * (1.0 + sc_ref[0]) + sh_ref[0]).astype(BF16)
        h_scr[...] = h
        alr_ref[...] = _dot(h, walr_ref[...])

    o_ref[...] = _dot(h_scr[...], w_ref[...]).astype(o_ref.dtype)


def _inproj_call(x2, g, sc, sh, w_main, w_alr, S):
    T, D = x2.shape
    NM = w_main.shape[1]
    tm = min(1024, S)
    tn = min(1024, NM)
    per_b = S // tm
    return pl.pallas_call(
        _inproj_kernel,
        grid=(T // tm, NM // tn),
        in_specs=[pl.BlockSpec((tm, D), lambda i, j: (i, 0)),
                  pl.BlockSpec((1, D), lambda i, j: (0, 0)),
                  pl.BlockSpec((1, 1, D), lambda i, j: (i // per_b, 0, 0)),
                  pl.BlockSpec((1, 1, D), lambda i, j: (i // per_b, 0, 0)),
                  pl.BlockSpec((D, tn), lambda i, j: (0, j)),
                  pl.BlockSpec((D, LANES), lambda i, j: (0, 0))],
        out_specs=[pl.BlockSpec((tm, tn), lambda i, j: (i, j)),
                   pl.BlockSpec((tm, LANES), lambda i, j: (i, 0))],
        out_shape=[jax.ShapeDtypeStruct((T, NM), BF16),
                   jax.ShapeDtypeStruct((T, LANES), F32)],
        scratch_shapes=[pltpu.VMEM((tm, D), BF16)],
        compiler_params=_params(("parallel", "arbitrary"), 56),
        name="in_proj",
    )(x2, g, sc, sh, w_main, w_alr)


def _gmlp_kernel(u_ref, v_ref, g_ref, b_ref, ws_ref, bias_ref, o_ref, vn_scr, *, groups, chunk):
    rows, width = v_ref.shape
    gc = width // groups
    r = lax.broadcasted_iota(jnp.int32, (chunk, chunk), 0)
    c = lax.broadcasted_iota(jnp.int32, (chunk, chunk), 1)
    causal = r >= c
    for ci in range(rows // chunk):
        rs = slice(ci * chunk, (ci + 1) * chunk)
        gv = _gelu(v_ref[rs, :].astype(F32))
        mu = jnp.mean(gv, axis=-1, keepdims=True)
        xc = gv - mu
        var = jnp.mean(xc * xc, axis=-1, keepdims=True)
        vn = xc * lax.rsqrt(var + EPS) * g_ref[...] + b_ref[...]
        vn_scr[...] = vn.astype(BF16)
        for gi in range(groups):
            cs = slice(gi * gc, (gi + 1) * gc)
            wm = jnp.where(causal, ws_ref[gi], 0.0).astype(BF16)
            mixed = _dot(wm, vn_scr[:, cs]) + bias_ref[:, cs]
            gu = _gelu(u_ref[rs, cs].astype(F32))
            o_ref[rs, cs] = (gu * mixed).astype(o_ref.dtype)


def _gmlp_call(proj, vn_g, vn_b, ws, bias_full):
    T = proj.shape[0]
    G, C, _ = ws.shape
    W = vn_g.shape[1]
    R = min(2 * C, T)
    kern = functools.partial(_gmlp_kernel, groups=G, chunk=C)
    return pl.pallas_call(
        kern,
        grid=(T // R,),
        in_specs=[pl.BlockSpec((R, W), lambda i: (i, 0)),
                  pl.BlockSpec((R, W), lambda i: (i, 1)),
                  pl.BlockSpec((1, W), lambda i: (0, 0)),
                  pl.BlockSpec((1, W), lambda i: (0, 0)),
                  pl.BlockSpec((G, C, C), lambda i: (0, 0, 0)),
                  pl.BlockSpec((C, W), lambda i: (0, 0))],
        out_specs=pl.BlockSpec((R, W), lambda i: (i, 0)),
        out_shape=jax.ShapeDtypeStruct((T, W), BF16),
        scratch_shapes=[pltpu.VMEM((C, W), BF16)],
        compiler_params=_params(("parallel",), 32),
        name="gmlp_gate",
    )(proj, proj, vn_g, vn_b, ws, bias_full)


def _gla_kernel(q_ref, k_ref, v_ref, r_ref, alr_ref, wa2_ref, ba_ref, g_ref, o_ref,
                st_ref, la_ref, *, heads):
    rows, dk = q_ref.shape
    dv = v_ref.shape[1]
    hk = dk // heads
    hv = dv // heads
    C = GLA_CHUNK
    scale = hk ** -0.5

    @pl.when(pl.program_id(1) == 0)
    def _():
        st_ref[...] = jnp.zeros(st_ref.shape, F32)

    z = _dot(alr_ref[...].astype(BF16), wa2_ref[...]) + ba_ref[...]
    la_ref[...] = (jnp.minimum(z, 0.0) - jnp.log1p(jnp.exp(-jnp.abs(z)))) * (1.0 / GLA_TAU)

    ri = lax.broadcasted_iota(jnp.int32, (C, C), 0)
    ci = lax.broadcasted_iota(jnp.int32, (C, C), 1)
    causal = ri >= ci
    tri = causal.astype(BF16)

    def chunk_step(c, carry):
        rs = pl.ds(pl.multiple_of(c * C, C), C)
        la = la_ref[rs, :]
        hi = la.astype(BF16)
        r1 = la - hi.astype(F32)
        mid = r1.astype(BF16)
        lo = (r1 - mid.astype(F32)).astype(BF16)
        cum = _dot(tri, hi) + _dot(tri, mid) + _dot(tri, lo)
        cl = cum[C - 1:C, :]
        q = q_ref[rs, :].astype(F32) * scale
        k = k_ref[rs, :].astype(F32)
        qd = (q * jnp.exp(cum)).astype(BF16)
        ki = (k * jnp.exp(-cum)).astype(BF16)
        ke = (k * jnp.exp(cl - cum)).astype(BF16)
        dec = jnp.exp(cl)
        for h in range(heads):
            ks = slice(h * hk, (h + 1) * hk)
            vs = slice(h * hv, (h + 1) * hv)
            vh = v_ref[rs, vs]
            s = lax.dot_general(qd[:, ks], ki[:, ks], NT_DIMS, preferred_element_type=F32)
            s = jnp.where(causal, s, 0.0).astype(BF16)
            st = st_ref[h]
            o = _dot(s, vh) + lax.dot_general(qd[:, ks], st.astype(BF16), NT_DIMS,
                                              preferred_element_type=F32)
            st_ref[h] = st * dec[:, ks] + lax.dot_general(vh, ke[:, ks], TN_DIMS,
                                                          preferred_element_type=F32)
            ms = jnp.mean(o * o, axis=-1, keepdims=True)
            on = o * lax.rsqrt(ms + EPS) * g_ref[:, vs]
            rr = r_ref[rs, vs].astype(F32)
            o_ref[rs, vs] = (on * (rr * _sigmoid(rr))).astype(o_ref.dtype)
        return carry

    lax.fori_loop(0, rows // C, chunk_step, 0)


def _gla_call(proj, alr, wa2p, ba, on_g, B, S, W):
    DK = wa2p.shape[1]
    DV = on_g.shape[1]
    H = GLA_HEADS
    Cb = min(256, S)
    nb = S // Cb
    q_blk = (2 * W) // DK
    k_blk = (2 * W + DK) // DK
    v_blk = (2 * W + 2 * DK) // DV
    r_blk = (2 * W + 2 * DK + DV) // DV
    kern = functools.partial(_gla_kernel, heads=H)
    return pl.pallas_call(
        kern,
        grid=(B, nb),
        in_specs=[pl.BlockSpec((Cb, DK), lambda b, i: (b * nb + i, q_blk)),
                  pl.BlockSpec((Cb, DK), lambda b, i: (b * nb + i, k_blk)),
                  pl.BlockSpec((Cb, DV), lambda b, i: (b * nb + i, v_blk)),
                  pl.BlockSpec((Cb, DV), lambda b, i: (b * nb + i, r_blk)),
                  pl.BlockSpec((Cb, LANES), lambda b, i: (b * nb + i, 0)),
                  pl.BlockSpec((LANES, DK), lambda b, i: (0, 0)),
                  pl.BlockSpec((1, DK), lambda b, i: (0, 0)),
                  pl.BlockSpec((1, DV), lambda b, i: (0, 0))],
        out_specs=pl.BlockSpec((Cb, DV), lambda b, i: (b * nb + i, 0)),
        out_shape=jax.ShapeDtypeStruct((B * S, DV), BF16),
        scratch_shapes=[pltpu.VMEM((H, DV // H, DK // H), F32),
                        pltpu.VMEM((Cb, DK), F32)],
        compiler_params=_params(("parallel", "arbitrary"), 32),
        name="gla",
    )(proj, proj, proj, proj, alr, wa2p, ba, on_g)


def _merge_kernel(ya_ref, yb_ref, ga_ref, gb_ref, wpa_ref, wpb_ref, wout_ref, x_ref, g1_ref,
                  n2_ref, sc_ref, sh_ref, wr_ref, x1_ref, h2_ref, lg_ref):
    a = _dot(ya_ref[...], wpa_ref[...])
    b = _dot(yb_ref[...], wpb_ref[...])
    y = _sigmoid(ga_ref[...].astype(F32)) * a + _sigmoid(gb_ref[...].astype(F32)) * b
    out = _dot(y.astype(BF16), wout_ref[...])
    x1 = x_ref[...] + g1_ref[0] * out
    x1_ref[...] = x1
    ms = jnp.mean(x1 * x1, axis=-1, keepdims=True)
    h2 = (x1 * lax.rsqrt(ms + EPS) * n2_ref[...]) * (1.0 + sc_ref[0]) + sh_ref[0]
    h2b = h2.astype(BF16)
    h2_ref[...] = h2b
    lg_ref[...] = _dot(h2b, wr_ref[...])


def _merge_call(ya, yb, proj, wpa, wpb, wout, x2, g1, n2g, sc2, sh2, wr, S, W):
    T, D = x2.shape
    ga_blk = (proj.shape[1] - 2 * D) // D
    gb_blk = ga_blk + 1
    tm = min(256, S)
    per_b = S // tm
    const = dict(pipeline_mode=pl.Buffered(1))
    return pl.pallas_call(
        _merge_kernel,
        grid=(T // tm,),
        in_specs=[pl.BlockSpec((tm, W), lambda i: (i, 0)),
                  pl.BlockSpec((tm, W), lambda i: (i, 0)),
                  pl.BlockSpec((tm, D), lambda i: (i, ga_blk)),
                  pl.BlockSpec((tm, D), lambda i: (i, gb_blk)),
                  pl.BlockSpec((W, D), lambda i: (0, 0), **const),
                  pl.BlockSpec((W, D), lambda i: (0, 0), **const),
                  pl.BlockSpec((D, D), lambda i: (0, 0), **const),
                  pl.BlockSpec((tm, D), lambda i: (i, 0)),
                  pl.BlockSpec((1, 1, D), lambda i: (i // per_b, 0, 0)),
                  pl.BlockSpec((1, D), lambda i: (0, 0)),
                  pl.BlockSpec((1, 1, D), lambda i: (i // per_b, 0, 0)),
                  pl.BlockSpec((1, 1, D), lambda i: (i // per_b, 0, 0)),
                  pl.BlockSpec((D, LANES), lambda i: (0, 0), **const)],
        out_specs=[pl.BlockSpec((tm, D), lambda i: (i, 0)),
                   pl.BlockSpec((tm, D), lambda i: (i, 0)),
                   pl.BlockSpec((tm, LANES), lambda i: (i, 0))],
        out_shape=[jax.ShapeDtypeStruct((T, D), F32),
                   jax.ShapeDtypeStruct((T, D), BF16),
                   jax.ShapeDtypeStruct((T, LANES), F32)],
        compiler_params=_params(("parallel",), 56),
        name="merge_out",
    )(ya, yb, proj, proj, wpa, wpb, wout, x2, g1, n2g, sc2, sh2, wr)


def _router_kernel(lg_ref, b_ref, e_ref, w_ref, *, groups, per_group):
    lg = lg_ref[...] + b_ref[...]
    lane = lax.broadcasted_iota(jnp.int32, lg.shape, 1)
    lane_f = lane.astype(F32)
    neg = jnp.float32(-1e30)
    big = jnp.float32(LANES)

    def first_argmax(vals):
        m = jnp.max(vals, axis=-1, keepdims=True)
        idx = jnp.min(jnp.where(vals == m, lane_f, big), axis=-1, keepdims=True)
        return m, idx

    gmask = lane < groups
    gl = jnp.where(gmask, lg, neg)
    gmax, gidx = first_argmax(gl)
    gsum = jnp.sum(jnp.where(gmask, jnp.exp(gl - gmax), 0.0), axis=-1, keepdims=True)
    gtop = 1.0 / gsum

    lo = groups + gidx * per_group
    emask = (lane_f >= lo) & (lane_f < lo + per_group)
    el = jnp.where(emask, lg, neg)
    m1, i1 = first_argmax(el)
    el2 = jnp.where(lane_f == i1, neg, el)
    m2, i2 = first_argmax(el2)
    den = jnp.sum(jnp.where(emask, jnp.exp(el - m1), 0.0), axis=-1, keepdims=True)
    p1 = 1.0 / den
    p2 = jnp.exp(m2 - m1) / den
    ps = p1 + p2
    w1 = gtop * p1 / ps
    w2 = gtop * p2 / ps
    e1 = (i1 - groups).astype(jnp.int32)
    e2 = (i2 - groups).astype(jnp.int32)
    e_ref[...] = jnp.where(lane == 0, e1, jnp.where(lane == 1, e2, 0))
    w_ref[...] = jnp.where(lane == 0, w1, jnp.where(lane == 1, w2, 0.0))


def _router_call(logits, bias, per_group):
    T = logits.shape[0]
    tm = min(1024, T)
    kern = functools.partial(_router_kernel, groups=MOE_GROUPS, per_group=per_group)
    return pl.pallas_call(
        kern,
        grid=(T // tm,),
        in_specs=[pl.BlockSpec((tm, LANES), lambda i: (i, 0)),
                  pl.BlockSpec((1, LANES), lambda i: (0, 0))],
        out_specs=[pl.BlockSpec((tm, LANES), lambda i: (i, 0)),
                   pl.BlockSpec((tm, LANES), lambda i: (i, 0))],
        out_shape=[jax.ShapeDtypeStruct((T, LANES), jnp.int32),
                   jax.ShapeDtypeStruct((T, LANES), F32)],
        compiler_params=_params(("parallel",), 16),
        name="router_topk",
    )(logits, bias)


def _gather_kernel(na_ref, idx_ref, src_ref, out_ref, sem, zbuf, zsem):
    i = pl.program_id(0)
    n = pl.num_programs(0)
    G = idx_ref.shape[2]
    slot = i % 2
    n_act = na_ref[0]

    def block_copy(s):
        return pltpu.make_async_copy(src_ref.at[pl.ds(0, G)], out_ref.at[pl.ds(0, G)], sem.at[s])

    @pl.when(i < n_act)
    def _():
        def issue(r, c):
            t = idx_ref[0, 0, r]
            pltpu.make_async_copy(src_ref.at[t], out_ref.at[i * G + r], sem.at[slot]).start()
            return c
        lax.fori_loop(0, G, issue, 0)

    @pl.when(i >= n_act)
    def _():
        zbuf[...] = jnp.zeros(zbuf.shape, zbuf.dtype)
        fill = pltpu.make_async_copy(zbuf, out_ref.at[pl.ds(i * G, G)], zsem.at[0])
        fill.start()
        fill.wait()

    @pl.when((i > 0) & (i - 1 < n_act))
    def _():
        block_copy(1 - slot).wait()

    @pl.when((i == n - 1) & (i < n_act))
    def _():
        block_copy(slot).wait()


def _gather_call(src3, idx, n_act):
    M = idx.shape[0]
    G = GATHER_ROWS
    nb = M // G
    idx3 = idx.reshape(nb, 1, G)
    grid_spec = pltpu.PrefetchScalarGridSpec(
        num_scalar_prefetch=1,
        grid=(nb,),
        in_specs=[pl.BlockSpec((1, 1, G), lambda i, na: (i, 0, 0), memory_space=pltpu.SMEM),
                  pl.BlockSpec(memory_space=pl.ANY)],
        out_specs=pl.BlockSpec(memory_space=pl.ANY),
        scratch_shapes=[pltpu.SemaphoreType.DMA((2,)),
                        pltpu.VMEM((G,) + src3.shape[1:], src3.dtype),
                        pltpu.SemaphoreType.DMA((1,))],
    )
    return pl.pallas_call(
        _gather_kernel,
        grid_spec=grid_spec,
        out_shape=jax.ShapeDtypeStruct((M,) + src3.shape[1:], src3.dtype),
        compiler_params=_params(("arbitrary",), 16),
        name="row_gather",
    )(n_act, idx3, src3)


def _expert_changed(be_ref, na_ref, i):
    prev = be_ref[jnp.maximum(i - 1, 0)]
    return (i == 0) | (be_ref[i] != prev)


def _expert_up_kernel(be_ref, na_ref, xs_ref, wg_ref, wu_ref, hid_ref, wgb, wub):
    i = pl.program_id(1)
    active = i < na_ref[0]

    @pl.when(active & _expert_changed(be_ref, na_ref, i))
    def _():
        wgb[...] = wg_ref[0].astype(BF16)
        wub[...] = wu_ref[0].astype(BF16)

    @pl.when(active)
    def _():
        x = xs_ref[...]
        a = _dot(x, wgb[...])
        b = _dot(x, wub[...])
        hid_ref[...] = (a * _sigmoid(a) * b).astype(hid_ref.dtype)

    @pl.when(jnp.logical_not(active))
    def _():
        hid_ref[...] = jnp.zeros(hid_ref.shape, hid_ref.dtype)


def _expert_down_kernel(be_ref, na_ref, hid_ref, wd_ref, ys_ref, wdb):
    i = pl.program_id(1)
    active = i < na_ref[0]

    @pl.when(active & _expert_changed(be_ref, na_ref, i))
    def _():
        wdb[...] = wd_ref[0].astype(BF16)

    @pl.when(active)
    def _():
        ys_ref[...] = _dot(hid_ref[...], wdb[...]).astype(ys_ref.dtype)

    @pl.when(jnp.logical_not(active))
    def _():
        ys_ref[...] = jnp.zeros(ys_ref.shape, ys_ref.dtype)


def _experts_call(xs, block_e, n_act, w_gate, w_up, w_down):
    n_rows, D = xs.shape
    E, _, DE = w_gate.shape
    BM = MOE_BLOCK
    nb = n_rows // BM
    tj = min(512, DE)
    tn = min(1024, D)

    def blk(i, na):
        return jnp.minimum(i, na[0] - 1)

    up_spec = pltpu.PrefetchScalarGridSpec(
        num_scalar_prefetch=2,
        grid=(DE // tj, nb),
        in_specs=[pl.BlockSpec((BM, D), lambda j, i, be, na: (blk(i, na), 0)),
                  pl.BlockSpec((1, D, tj), lambda j, i, be, na: (be[blk(i, na)], 0, j)),
                  pl.BlockSpec((1, D, tj), lambda j, i, be, na: (be[blk(i, na)], 0, j))],
        out_specs=pl.BlockSpec((BM, tj), lambda j, i, be, na: (i, j)),
        scratch_shapes=[pltpu.VMEM((D, tj), BF16), pltpu.VMEM((D, tj), BF16)],
    )
    hid = pl.pallas_call(
        _expert_up_kernel,
        grid_spec=up_spec,
        out_shape=jax.ShapeDtypeStruct((n_rows, DE), BF16),
        compiler_params=_params(("arbitrary", "arbitrary"), 48),
        name="expert_up",
    )(block_e, n_act, xs, w_gate, w_up)

    down_spec = pltpu.PrefetchScalarGridSpec(
        num_scalar_prefetch=2,
        grid=(D // tn, nb),
        in_specs=[pl.BlockSpec((BM, DE), lambda j, i, be, na: (blk(i, na), 0)),
                  pl.BlockSpec((1, DE, tn), lambda j, i, be, na: (be[blk(i, na)], 0, j))],
        out_specs=pl.BlockSpec((BM, tn), lambda j, i, be, na: (i, j)),
        scratch_shapes=[pltpu.VMEM((DE, tn), BF16)],
    )
    return pl.pallas_call(
        _expert_down_kernel,
        grid_spec=down_spec,
        out_shape=jax.ShapeDtypeStruct((n_rows, D), BF16),
        compiler_params=_params(("arbitrary", "arbitrary"), 40),
        name="expert_down",
    )(block_e, n_act, hid, w_down)


def _combine_kernel(x_ref, y0_ref, y1_ref, w_ref, g2_ref, fg_ref, o_ref, *, final_norm):
    w = w_ref[...]
    y = w[:, 0:1] * y0_ref[...].astype(F32) + w[:, 1:2] * y1_ref[...].astype(F32)
    x = x_ref[...] + g2_ref[0] * y
    if final_norm:
        ms = jnp.mean(x * x, axis=-1, keepdims=True)
        x = x * lax.rsqrt(ms + EPS) * fg_ref[...]
    o_ref[...] = x


def _combine_call(x1, yg, wts, g2, final_g, S, final_norm):
    T, D = x1.shape
    tm = min(512, S)
    per_b = S // tm
    kern = functools.partial(_combine_kernel, final_norm=final_norm)
    return pl.pallas_call(
        kern,
        grid=(T // tm,),
        in_specs=[pl.BlockSpec((tm, D), lambda i: (i, 0)),
                  pl.BlockSpec((tm, D), lambda i: (i, 0)),
                  pl.BlockSpec((tm, D), lambda i: (i, 1)),
                  pl.BlockSpec((tm, LANES), lambda i: (i, 0)),
                  pl.BlockSpec((1, 1, D), lambda i: (i // per_b, 0, 0)),
                  pl.BlockSpec((1, D), lambda i: (0, 0))],
        out_specs=pl.BlockSpec((tm, D), lambda i: (i, 0)),
        out_shape=jax.ShapeDtypeStruct((T, D), F32),
        compiler_params=_params(("parallel",), 40),
        name="moe_combine",
    )(x1, yg, yg, wts, g2, final_g)


def _dispatch_plan(eid, n_experts):
    T = eid.shape[0]
    A = T * TOP_K
    BM = MOE_BLOCK
    flat_e = eid.reshape(A)
    onehot = (flat_e[:, None] == jnp.arange(n_experts, dtype=jnp.int32)[None, :]).astype(jnp.int32)
    csum = jnp.cumsum(onehot, axis=0)
    rank = jnp.take_along_axis(csum, flat_e[:, None], axis=1)[:, 0] - 1
    counts = csum[-1]
    padded = (counts + BM - 1) // BM * BM
    pad_end = jnp.cumsum(padded)
    pad_start = pad_end - padded
    dest = pad_start[flat_e] + rank
    nb = (A + n_experts * (BM - 1) + BM - 1) // BM
    n_rows = nb * BM
    row_tok = jnp.zeros((n_rows,), jnp.int32).at[dest].set(jnp.arange(A, dtype=jnp.int32) // TOP_K)
    block_start = jnp.arange(nb, dtype=jnp.int32) * BM
    block_e = jnp.minimum(jnp.searchsorted(pad_end, block_start, side='right'),
                          n_experts - 1).astype(jnp.int32)
    n_act = (pad_end[-1] // BM).astype(jnp.int32).reshape(1)
    return dest.astype(jnp.int32), row_tok, block_e, n_act


def kernel(x, c, ada_w, ada_b, norm1_g, w_in, gm_vn_g, gm_vn_b, gm_ws, gm_bs, gla_wa2, gla_ba,
           gla_on_g, w_pa, w_pb, w_out, norm2_g, w_rg, b_rg, w_re, b_re, w_e_gate, w_e_up,
           w_e_down, final_g):
    B, S, D = x.shape
    T = B * S
    L = ada_w.shape[0]
    W = gm_vn_g.shape[1]
    G, C = gm_ws.shape[1], gm_ws.shape[2]
    RANK, DK = gla_wa2.shape[1], gla_wa2.shape[2]
    DV = gla_on_g.shape[1]
    E = w_e_gate.shape[1]
    per_group = E // MOE_GROUPS
    SUB = D // LANES
    assert B <= 8 and MOE_GROUPS + E <= LANES and RANK <= LANES

    c8 = jnp.zeros((8, D), F32).at[:B].set(c)
    mod = _ada_call(c8, ada_w, ada_b.reshape(L, 1, 6 * D))

    o_alr = 2 * W + 2 * DK + 2 * DV
    x2 = x.reshape(T, D)
    for l in range(L):
        sh1, sc1, g1, sh2, sc2, g2 = [mod[l, :B, k * D:(k + 1) * D].reshape(B, 1, D) for k in range(6)]
        wl = w_in[l]
        w_main = jnp.concatenate([wl[:, :o_alr], wl[:, o_alr + RANK:]], axis=1).astype(BF16)
        w_alr = jnp.zeros((D, LANES), BF16).at[:, :RANK].set(wl[:, o_alr:o_alr + RANK].astype(BF16))
        wa2p = jnp.zeros((LANES, DK), BF16).at[:RANK].set(gla_wa2[l].astype(BF16))
        bias_full = jnp.repeat(gm_bs[l].T, W // G, axis=1)
        wr = jnp.zeros((D, LANES), BF16).at[:, :MOE_GROUPS + E].set(
            jnp.concatenate([w_rg[l], w_re[l]], axis=1).astype(BF16))
        br = jnp.zeros((1, LANES), F32).at[0, :MOE_GROUPS + E].set(
            jnp.concatenate([b_rg[l], b_re[l]]))

        proj, alr = _inproj_call(x2, norm1_g[l].reshape(1, D), sc1, sh1, w_main, w_alr, S)
        ya = _gmlp_call(proj, gm_vn_g[l].reshape(1, W), gm_vn_b[l].reshape(1, W), gm_ws[l], bias_full)
        yb = _gla_call(proj, alr, wa2p, gla_ba[l].reshape(1, DK), gla_on_g[l].reshape(1, DV), B, S, W)
        x1, h2, logits = _merge_call(ya, yb, proj, w_pa[l].astype(BF16), w_pb[l].astype(BF16),
                                     w_out[l].astype(BF16), x2, g1, norm2_g[l].reshape(1, D),
                                     sc2, sh2, wr, S, W)
        eid, wts = _router_call(logits, br, per_group)
        dest, row_tok, block_e, n_act = _dispatch_plan(eid[:, :TOP_K], E)
        n_rows = row_tok.shape[0]
        xs = _gather_call(h2.reshape(T, SUB, LANES), row_tok, n_act * (MOE_BLOCK // GATHER_ROWS))
        ys = _experts_call(xs.reshape(n_rows, D), block_e, n_act, w_e_gate[l], w_e_up[l], w_e_down[l])
        all_blocks = jnp.full((1,), (T * TOP_K) // GATHER_ROWS, jnp.int32)
        yg = _gather_call(ys.reshape(n_rows, SUB, LANES), dest, all_blocks)
        x2 = _combine_call(x1, yg.reshape(T, TOP_K * D), wts, g2, final_g.reshape(1, D), S,
                           final_norm=(l == L - 1))
    return x2.reshape(B, S, D)
```

```python
import functools

import jax
import jax.numpy as jnp
from jax import lax
from jax.experimental import pallas as pl
from jax.experimental.pallas import tpu as pltpu

F32 = jnp.float32
BF16 = jnp.bfloat16

EPS = 1e-6
GLA_HEADS = 4
GLA_TAU = 16.0
GLA_CHUNK = 64
MOE_GROUPS = 4
TOP_K = 2

LANES = 128
MIB = 1024 * 1024
MOE_BLOCK = 256
GATHER_ROWS = 512

NT_DIMS = (((1,), (1,)), ((), ()))
TN_DIMS = (((0,), (0,)), ((), ()))


def _params(semantics, vmem_mib):
    return pltpu.CompilerParams(dimension_semantics=semantics,
                                vmem_limit_bytes=int(vmem_mib * MIB))


def _dot(a, b):
    return jnp.dot(a, b, preferred_element_type=F32)


def _sigmoid(x):
    return 1.0 / (1.0 + jnp.exp(-x))


def _gelu(x):
    return 0.5 * x * (1.0 + lax.erf(x * (2.0 ** -0.5)))


def _ada_kernel(c_ref, w_ref, b_ref, o_ref):
    c = c_ref[...]
    cond = c * _sigmoid(c)
    o_ref[0] = _dot(cond.astype(BF16), w_ref[0].astype(BF16)) + b_ref[0]


def _ada_call(c8, ada_w, ada_b3):
    L, D, N = ada_w.shape
    tn = min(1024, N)
    return pl.pallas_call(
        _ada_kernel,
        grid=(L, N // tn),
        in_specs=[pl.BlockSpec((8, D), lambda l, j: (0, 0)),
                  pl.BlockSpec((1, D, tn), lambda l, j: (l, 0, j)),
                  pl.BlockSpec((1, 1, tn), lambda l, j: (l, 0, j))],
        out_specs=pl.BlockSpec((1, 8, tn), lambda l, j: (l, 0, j)),
        out_shape=jax.ShapeDtypeStruct((L, 8, N), F32),
        compiler_params=_params(("parallel", "parallel"), 40),
        name="ada_mod",
    )(c8, ada_w, ada_b3)


def _inproj_kernel(x_ref, g_ref, sc_ref, sh_ref, w_ref, walr_ref, o_ref, alr_ref, h_scr):
    @pl.when(pl.program_id(1) == 0)
    def _():
        x = x_ref[...]
        ms = jnp.mean(x * x, axis=-1, keepdims=True)
        y = x * lax.rsqrt(ms + EPS) * g_ref[...]
        h = (y * (1.0 + sc_ref[0]) + sh_ref[0]).astype(BF16)
        h_scr[...] = h
        alr_ref[...] = _dot(h, walr_ref[...])

    o_ref[...] = _dot(h_scr[...], w_ref[...]).astype(o_ref.dtype)


def _inproj_call(x2, g, sc, sh, w_main, w_alr, S):
    T, D = x2.shape
    NM = w_main.shape[1]
    tm = min(1024, S)
    tn = min(1024, NM)
    per_b = S // tm
    return pl.pallas_call(
        _inproj_kernel,
        grid=(T // tm, NM // tn),
        in_specs=[pl.BlockSpec((tm, D), lambda i, j: (i, 0)),
                  pl.BlockSpec((1, D), lambda i, j: (0, 0)),
                  pl.BlockSpec((1, 1, D), lambda i, j: (i // per_b, 0, 0)),
                  pl.BlockSpec((1, 1, D), lambda i, j: (i // per_b, 0, 0)),
                  pl.BlockSpec((D, tn), lambda i, j: (0, j)),
                  pl.BlockSpec((D, LANES), lambda i, j: (0, 0))],
        out_specs=[pl.BlockSpec((tm, tn), lambda i, j: (i, j)),
                   pl.BlockSpec((tm, LANES), lambda i, j: (i, 0))],
        out_shape=[jax.ShapeDtypeStruct((T, NM), BF16),
                   jax.ShapeDtypeStruct((T, LANES), F32)],
        scratch_shapes=[pltpu.VMEM((tm, D), BF16)],
        compiler_params=_params(("parallel", "arbitrary"), 56),
        name="in_proj",
    )(x2, g, sc, sh, w_main, w_alr)


def _gmlp_kernel(u_ref, v_ref, g_ref, b_ref, ws_ref, bias_ref, o_ref, vn_scr, *, groups, chunk):
    rows, width = v_ref.shape
    gc = width // groups
    r = lax.broadcasted_iota(jnp.int32, (chunk, chunk), 0)
    c = lax.broadcasted_iota(jnp.int32, (chunk, chunk), 1)
    causal = r >= c
    for ci in range(rows // chunk):
        rs = slice(ci * chunk, (ci + 1) * chunk)
        gv = _gelu(v_ref[rs, :].astype(F32))
        mu = jnp.mean(gv, axis=-1, keepdims=True)
        xc = gv - mu
        var = jnp.mean(xc * xc, axis=-1, keepdims=True)
        vn = xc * lax.rsqrt(var + EPS) * g_ref[...] + b_ref[...]
        vn_scr[...] = vn.astype(BF16)
        for gi in range(groups):
            cs = slice(gi * gc, (gi + 1) * gc)
            wm = jnp.where(causal, ws_ref[gi], 0.0).astype(BF16)
            mixed = _dot(wm, vn_scr[:, cs]) + bias_ref[:, cs]
            gu = _gelu(u_ref[rs, cs].astype(F32))
            o_ref[rs, cs] = (gu * mixed).astype(o_ref.dtype)


def _gmlp_call(proj, vn_g, vn_b, ws, bias_full):
    T = proj.shape[0]
    G, C, _ = ws.shape
    W = vn_g.shape[1]
    R = min(2 * C, T)
    kern = functools.partial(_gmlp_kernel, groups=G, chunk=C)
    return pl.pallas_call(
        kern,
        grid=(T // R,),
        in_specs=[pl.BlockSpec((R, W), lambda i: (i, 0)),
                  pl.BlockSpec((R, W), lambda i: (i, 1)),
                  pl.BlockSpec((1, W), lambda i: (0, 0)),
                  pl.BlockSpec((1, W), lambda i: (0, 0)),
                  pl.BlockSpec((G, C, C), lambda i: (0, 0, 0)),
                  pl.BlockSpec((C, W), lambda i: (0, 0))],
        out_specs=pl.BlockSpec((R, W), lambda i: (i, 0)),
        out_shape=jax.ShapeDtypeStruct((T, W), BF16),
        scratch_shapes=[pltpu.VMEM((C, W), BF16)],
        compiler_params=_params(("parallel",), 32),
        name="gmlp_gate",
    )(proj, proj, vn_g, vn_b, ws, bias_full)


def _gla_kernel(q_ref, k_ref, v_ref, r_ref, alr_ref, wa2_ref, ba_ref, g_ref, o_ref,
                st_ref, la_ref, *, heads):
    rows, dk = q_ref.shape
    dv = v_ref.shape[1]
    hk = dk // heads
    hv = dv // heads
    C = GLA_CHUNK
    scale = hk ** -0.5

    @pl.when(pl.program_id(1) == 0)
    def _():
        st_ref[...] = jnp.zeros(st_ref.shape, F32)

    z = _dot(alr_ref[...].astype(BF16), wa2_ref[...]) + ba_ref[...]
    la_ref[...] = (jnp.minimum(z, 0.0) - jnp.log1p(jnp.exp(-jnp.abs(z)))) * (1.0 / GLA_TAU)

    ri = lax.broadcasted_iota(jnp.int32, (C, C), 0)
    ci = lax.broadcasted_iota(jnp.int32, (C, C), 1)
    causal = ri >= ci
    tri = causal.astype(BF16)

    def chunk_step(c, carry):
        rs = pl.ds(pl.multiple_of(c * C, C), C)
        la = la_ref[rs, :]
        hi = la.astype(BF16)
        r1 = la - hi.astype(F32)
        mid = r1.astype(BF16)
        lo = (r1 - mid.astype(F32)).astype(BF16)
        cum = _dot(tri, hi) + _dot(tri, mid) + _dot(tri, lo)
        cl = cum[C - 1:C, :]
        q = q_ref[rs, :].astype(F32) * scale
        k = k_ref[rs, :].astype(F32)
        qd = (q * jnp.exp(cum)).astype(BF16)
        ki = (k * jnp.exp(-cum)).astype(BF16)
        ke = (k * jnp.exp(cl - cum)).astype(BF16)
        dec = jnp.exp(cl)
        for h in range(heads):
            ks = slice(h * hk, (h + 1) * hk)
            vs = slice(h * hv, (h + 1) * hv)
            vh = v_ref[rs, vs]
            s = lax.dot_general(qd[:, ks], ki[:, ks], NT_DIMS, preferred_element_type=F32)
            s = jnp.where(causal, s, 0.0).astype(BF16)
            st = st_ref[h]
            o = _dot(s, vh) + lax.dot_general(qd[:, ks], st.astype(BF16), NT_DIMS,
                                              preferred_element_type=F32)
            st_ref[h] = st * dec[:, ks] + lax.dot_general(vh, ke[:, ks], TN_DIMS,
                                                          preferred_element_type=F32)
            ms = jnp.mean(o * o, axis=-1, keepdims=True)
            on = o * lax.rsqrt(ms + EPS) * g_ref[:, vs]
            rr = r_ref[rs, vs].astype(F32)
            o_ref[rs, vs] = (on * (rr * _sigmoid(rr))).astype(o_ref.dtype)
        return carry

    lax.fori_loop(0, rows // C, chunk_step, 0)


def _gla_call(proj, alr, wa2p, ba, on_g, B, S, W):
    DK = wa2p.shape[1]
    DV = on_g.shape[1]
    H = GLA_HEADS
    Cb = min(256, S)
    nb = S // Cb
    q_blk = (2 * W) // DK
    k_blk = (2 * W + DK) // DK
    v_blk = (2 * W + 2 * DK) // DV
    r_blk = (2 * W + 2 * DK + DV) // DV
    kern = functools.partial(_gla_kernel, heads=H)
    return pl.pallas_call(
        kern,
        grid=(B, nb),
        in_specs=[pl.BlockSpec((Cb, DK), lambda b, i: (b * nb + i, q_blk)),
                  pl.BlockSpec((Cb, DK), lambda b, i: (b * nb + i, k_blk)),
                  pl.BlockSpec((Cb, DV), lambda b, i: (b * nb + i, v_blk)),
                  pl.BlockSpec((Cb, DV), lambda b, i: (b * nb + i, r_blk)),
                  pl.BlockSpec((Cb, LANES), lambda b, i: (b * nb + i, 0)),
                  pl.BlockSpec((LANES, DK), lambda b, i: (0, 0)),
                  pl.BlockSpec((1, DK), lambda b, i: (0, 0)),
                  pl.BlockSpec((1, DV), lambda b, i: (0, 0))],
        out_specs=pl.BlockSpec((Cb, DV), lambda b, i: (b * nb + i, 0)),
        out_shape=jax.ShapeDtypeStruct((B * S, DV), BF16),
        scratch_shapes=[pltpu.VMEM((H, DV // H, DK // H), F32),
                        pltpu.VMEM((Cb, DK), F32)],
        compiler_params=_params(("parallel", "arbitrary"), 32),
        name="gla",
    )(proj, proj, proj, proj, alr, wa2p, ba, on_g)


def _merge_kernel(ya_ref, yb_ref, ga_ref, gb_ref, wpa_ref, wpb_ref, wout_ref, x_ref, g1_ref,
                  n2_ref, sc_ref, sh_ref, wr_ref, x1_ref, h2_ref, lg_ref):
    a = _dot(ya_ref[...], wpa_ref[...])
    b = _dot(yb_ref[...], wpb_ref[...])
    y = _sigmoid(ga_ref[...].astype(F32)) * a + _sigmoid(gb_ref[...].astype(F32)) * b
    out = _dot(y.astype(BF16), wout_ref[...])
    x1 = x_ref[...] + g1_ref[0] * out
    x1_ref[...] = x1
    ms = jnp.mean(x1 * x1, axis=-1, keepdims=True)
    h2 = (x1 * lax.rsqrt(ms + EPS) * n2_ref[...]) * (1.0 + sc_ref[0]) + sh_ref[0]
    h2b = h2.astype(BF16)
    h2_ref[...] = h2b
    lg_ref[...] = _dot(h2b, wr_ref[...])


def _merge_call(ya, yb, proj, wpa, wpb, wout, x2, g1, n2g, sc2, sh2, wr, S, W):
    T, D = x2.shape
    ga_blk = (proj.shape[1] - 2 * D) // D
    gb_blk = ga_blk + 1
    tm = min(256, S)
    per_b = S // tm
    const = dict(pipeline_mode=pl.Buffered(1))
    return pl.pallas_call(
        _merge_kernel,
        grid=(T // tm,),
        in_specs=[pl.BlockSpec((tm, W), lambda i: (i, 0)),
                  pl.BlockSpec((tm, W), lambda i: (i, 0)),
                  pl.BlockSpec((tm, D), lambda i: (i, ga_blk)),
                  pl.BlockSpec((tm, D), lambda i: (i, gb_blk)),
                  pl.BlockSpec((W, D), lambda i: (0, 0), **const),
                  pl.BlockSpec((W, D), lambda i: (0, 0), **const),
                  pl.BlockSpec((D, D), lambda i: (0, 0), **const),
                  pl.BlockSpec((tm, D), lambda i: (i, 0)),
                  pl.BlockSpec((1, 1, D), lambda i: (i // per_b, 0, 0)),
                  pl.BlockSpec((1, D), lambda i: (0, 0)),
                  pl.BlockSpec((1, 1, D), lambda i: (i // per_b, 0, 0)),
                  pl.BlockSpec((1, 1, D), lambda i: (i // per_b, 0, 0)),
                  pl.BlockSpec((D, LANES), lambda i: (0, 0), **const)],
        out_specs=[pl.BlockSpec((tm, D), lambda i: (i, 0)),
                   pl.BlockSpec((tm, D), lambda i: (i, 0)),
                   pl.BlockSpec((tm, LANES), lambda i: (i, 0))],
        out_shape=[jax.ShapeDtypeStruct((T, D), F32),
                   jax.ShapeDtypeStruct((T, D), BF16),
                   jax.ShapeDtypeStruct((T, LANES), F32)],
        compiler_params=_params(("parallel",), 56),
        name="merge_out",
    )(ya, yb, proj, proj, wpa, wpb, wout, x2, g1, n2g, sc2, sh2, wr)


def _router_kernel(lg_ref, b_ref, e_ref, w_ref, *, groups, per_group):
    lg = lg_ref[...] + b_ref[...]
    lane = lax.broadcasted_iota(jnp.int32, lg.shape, 1)
    lane_f = lane.astype(F32)
    neg = jnp.float32(-1e30)
    big = jnp.float32(LANES)

    def first_argmax(vals):
        m = jnp.max(vals, axis=-1, keepdims=True)
        idx = jnp.min(jnp.where(vals == m, lane_f, big), axis=-1, keepdims=True)
        return m, idx

    gmask = lane < groups
    gl = jnp.where(gmask, lg, neg)
    gmax, gidx = first_argmax(gl)
    gsum = jnp.sum(jnp.where(gmask, jnp.exp(gl - gmax), 0.0), axis=-1, keepdims=True)
    gtop = 1.0 / gsum

    lo = groups + gidx * per_group
    emask = (lane_f >= lo) & (lane_f < lo + per_group)
    el = jnp.where(emask, lg, neg)
    m1, i1 = first_argmax(el)
    el2 = jnp.where(lane_f == i1, neg, el)
    m2, i2 = first_argmax(el2)
    den = jnp.sum(jnp.where(emask, jnp.exp(el - m1), 0.0), axis=-1, keepdims=True)
    p1 = 1.0 / den
    p2 = jnp.exp(m2 - m1) / den
    ps = p1 + p2
    w1 = gtop * p1 / ps
    w2 = gtop * p2 / ps
    e1 = (i1 - groups).astype(jnp.int32)
    e2 = (i2 - groups).astype(jnp.int32)
    e_ref[...] = jnp.where(lane == 0, e1, jnp.where(lane == 1, e2, 0))
    w_ref[...] = jnp.where(lane == 0, w1, jnp.where(lane == 1, w2, 0.0))


def _router_call(logits, bias, per_group):
    T = logits.shape[0]
    tm = min(1024, T)
    kern = functools.partial(_router_kernel, groups=MOE_GROUPS, per_group=per_group)
    return pl.pallas_call(
        kern,
        grid=(T // tm,),
        in_specs=[pl.BlockSpec((tm, LANES), lambda i: (i, 0)),
                  pl.BlockSpec((1, LANES), lambda i: (0, 0))],
        out_specs=[pl.BlockSpec((tm, LANES), lambda i: (i, 0)),
                   pl.BlockSpec((tm, LANES), lambda i: (i, 0))],
        out_shape=[jax.ShapeDtypeStruct((T, LANES), jnp.int32),
                   jax.ShapeDtypeStruct((T, LANES), F32)],
        compiler_params=_params(("parallel",), 16),
        name="router_topk",
    )(logits, bias)


def _gather_kernel(na_ref, idx_ref, src_ref, out_ref, sem):
    i = pl.program_id(0)
    G = out_ref.shape[0]
    n_act = na_ref[0]

    @pl.when(i < n_act)
    def _():
        def issue(r, c):
            t = idx_ref[0, 0, r]
            pltpu.make_async_copy(src_ref.at[t], out_ref.at[r], sem.at[0]).start()
            return c
        lax.fori_loop(0, G, issue, 0, unroll=8)
        pltpu.make_async_copy(src_ref.at[pl.ds(0, G)], out_ref, sem.at[0]).wait()

    @pl.when(i >= n_act)
    def _():
        out_ref[...] = jnp.zeros(out_ref.shape, out_ref.dtype)


def _gather_call(src3, idx, n_act):
    M = idx.shape[0]
    G = GATHER_ROWS
    nb = M // G
    idx3 = idx.reshape(nb, 1, G)
    grid_spec = pltpu.PrefetchScalarGridSpec(
        num_scalar_prefetch=1,
        grid=(nb,),
        in_specs=[pl.BlockSpec((1, 1, G), lambda i, na: (i, 0, 0), memory_space=pltpu.SMEM),
                  pl.BlockSpec(memory_space=pl.ANY)],
        out_specs=pl.BlockSpec((G,) + src3.shape[1:], lambda i, na: (i, 0, 0)),
        scratch_shapes=[pltpu.SemaphoreType.DMA((1,))],
    )
    return pl.pallas_call(
        _gather_kernel,
        grid_spec=grid_spec,
        out_shape=jax.ShapeDtypeStruct((M,) + src3.shape[1:], src3.dtype),
        compiler_params=_params(("arbitrary",), 16),
        name="row_gather",
    )(n_act, idx3, src3)


def _expert_changed(be_ref, na_ref, i):
    prev = be_ref[jnp.maximum(i - 1, 0)]
    return (i == 0) | (be_ref[i] != prev)


def _expert_up_kernel(be_ref, na_ref, xs_ref, wg_ref, wu_ref, hid_ref, wgb, wub):
    i = pl.program_id(1)
    active = i < na_ref[0]

    @pl.when(active & _expert_changed(be_ref, na_ref, i))
    def _():
        wgb[...] = wg_ref[0, 0].astype(BF16)
        wub[...] = wu_ref[0, 0].astype(BF16)

    @pl.when(active)
    def _():
        x = xs_ref[...]
        a = _dot(x, wgb[...])
        b = _dot(x, wub[...])
        hid_ref[...] = (a * _sigmoid(a) * b).astype(hid_ref.dtype)

    @pl.when(jnp.logical_not(active))
    def _():
        hid_ref[...] = jnp.zeros(hid_ref.shape, hid_ref.dtype)


def _expert_down_kernel(be_ref, na_ref, hid_ref, wd_ref, ys_ref, wdb):
    i = pl.program_id(1)
    active = i < na_ref[0]

    @pl.when(active & _expert_changed(be_ref, na_ref, i))
    def _():
        wdb[...] = wd_ref[0, 0].astype(BF16)

    @pl.when(active)
    def _():
        ys_ref[...] = _dot(hid_ref[...], wdb[...]).astype(ys_ref.dtype)

    @pl.when(jnp.logical_not(active))
    def _():
        ys_ref[...] = jnp.zeros(ys_ref.shape, ys_ref.dtype)


def _experts_call(xs, block_e, n_act, w_gate, w_up, w_down, l):
    n_rows, D = xs.shape
    DE = w_gate.shape[-1]
    BM = MOE_BLOCK
    nb = n_rows // BM
    tj = min(512, DE)
    tn = min(1024, D)

    def blk(i, na):
        return jnp.minimum(i, na[0] - 1)

    up_spec = pltpu.PrefetchScalarGridSpec(
        num_scalar_prefetch=2,
        grid=(DE // tj, nb),
        in_specs=[pl.BlockSpec((BM, D), lambda j, i, be, na: (blk(i, na), 0)),
                  pl.BlockSpec((1, 1, D, tj), lambda j, i, be, na: (l, be[blk(i, na)], 0, j)),
                  pl.BlockSpec((1, 1, D, tj), lambda j, i, be, na: (l, be[blk(i, na)], 0, j))],
        out_specs=pl.BlockSpec((BM, tj), lambda j, i, be, na: (i, j)),
        scratch_shapes=[pltpu.VMEM((D, tj), BF16), pltpu.VMEM((D, tj), BF16)],
    )
    hid = pl.pallas_call(
        _expert_up_kernel,
        grid_spec=up_spec,
        out_shape=jax.ShapeDtypeStruct((n_rows, DE), BF16),
        compiler_params=_params(("arbitrary", "arbitrary"), 48),
        name="expert_up",
    )(block_e, n_act, xs, w_gate, w_up)

    down_spec = pltpu.PrefetchScalarGridSpec(
        num_scalar_prefetch=2,
        grid=(D // tn, nb),
        in_specs=[pl.BlockSpec((BM, DE), lambda j, i, be, na: (blk(i, na), 0)),
                  pl.BlockSpec((1, 1, DE, tn), lambda j, i, be, na: (l, be[blk(i, na)], 0, j))],
        out_specs=pl.BlockSpec((BM, tn), lambda j, i, be, na: (i, j)),
        scratch_shapes=[pltpu.VMEM((DE, tn), BF16)],
    )
    return pl.pallas_call(
        _expert_down_kernel,
        grid_spec=down_spec,
        out_shape=jax.ShapeDtypeStruct((n_rows, D), BF16),
        compiler_params=_params(("arbitrary", "arbitrary"), 40),
        name="expert_down",
    )(block_e, n_act, hid, w_down)


def _combine_kernel(x_ref, y0_ref, y1_ref, w_ref, g2_ref, fg_ref, o_ref, *, final_norm):
    w = w_ref[...]
    y = w[:, 0:1] * y0_ref[...].astype(F32) + w[:, 1:2] * y1_ref[...].astype(F32)
    x = x_ref[...] + g2_ref[0] * y
    if final_norm:
        ms = jnp.mean(x * x, axis=-1, keepdims=True)
        x = x * lax.rsqrt(ms + EPS) * fg_ref[...]
    o_ref[...] = x


def _combine_call(x1, yg, wts, g2, final_g, S, final_norm):
    T, D = x1.shape
    tm = min(512, S)
    per_b = S // tm
    kern = functools.partial(_combine_kernel, final_norm=final_norm)
    return pl.pallas_call(
        kern,
        grid=(T // tm,),
        in_specs=[pl.BlockSpec((tm, D), lambda i: (i, 0)),
                  pl.BlockSpec((tm, D), lambda i: (i, 0)),
                  pl.BlockSpec((tm, D), lambda i: (i, 1)),
                  pl.BlockSpec((tm, LANES), lambda i: (i, 0)),
                  pl.BlockSpec((1, 1, D), lambda i: (i // per_b, 0, 0)),
                  pl.BlockSpec((1, D), lambda i: (0, 0))],
        out_specs=pl.BlockSpec((tm, D), lambda i: (i, 0)),
        out_shape=jax.ShapeDtypeStruct((T, D), F32),
        compiler_params=_params(("parallel",), 40),
        name="moe_combine",
    )(x1, yg, yg, wts, g2, final_g)


def _rank_kernel(e_ref, rank_ref, cnt_ref, carry):
    @pl.when(pl.program_id(0) == 0)
    def _():
        carry[...] = jnp.zeros(carry.shape, F32)

    e = e_ref[...]
    tb = e.shape[0]
    lane = lax.broadcasted_iota(jnp.int32, e.shape, 1)
    oh0 = lane == e[:, 0:1]
    oh1 = lane == e[:, 1:2]
    both = oh0.astype(F32) + oh1.astype(F32)
    r = lax.broadcasted_iota(jnp.int32, (tb, tb), 0)
    c = lax.broadcasted_iota(jnp.int32, (tb, tb), 1)
    strict = (r > c).astype(BF16)
    before = _dot(strict, both.astype(BF16)) + carry[0:1, :]
    r0 = jnp.sum(jnp.where(oh0, before, 0.0), axis=-1, keepdims=True)
    r1 = jnp.sum(jnp.where(oh1, before, 0.0), axis=-1, keepdims=True)
    rank_ref[...] = jnp.where(lane == 0, r0, jnp.where(lane == 1, r1, 0.0)).astype(jnp.int32)
    carry[...] = carry[...] + jnp.sum(both, axis=0, keepdims=True)
    cnt_ref[...] = carry[...]


def _rank_call(eid):
    T = eid.shape[0]
    tb = min(512, T)
    return pl.pallas_call(
        _rank_kernel,
        grid=(T // tb,),
        in_specs=[pl.BlockSpec((tb, LANES), lambda i: (i, 0))],
        out_specs=[pl.BlockSpec((tb, LANES), lambda i: (i, 0)),
                   pl.BlockSpec((8, LANES), lambda i: (0, 0))],
        out_shape=[jax.ShapeDtypeStruct((T, LANES), jnp.int32),
                   jax.ShapeDtypeStruct((8, LANES), F32)],
        scratch_shapes=[pltpu.VMEM((8, LANES), F32)],
        compiler_params=_params(("arbitrary",), 16),
        name="assign_rank",
    )(eid)


def _dispatch_plan(eid_full, n_experts):
    T = eid_full.shape[0]
    A = T * TOP_K
    BM = MOE_BLOCK
    rank_full, cnt = _rank_call(eid_full)
    eid = eid_full[:, :TOP_K]
    counts = cnt[0, :n_experts].astype(jnp.int32)
    padded = (counts + BM - 1) // BM * BM
    pad_end = jnp.cumsum(padded)
    pad_start = pad_end - padded
    onehot = eid[:, :, None] == jnp.arange(n_experts, dtype=jnp.int32)[None, None, :]
    dest = (jnp.sum(jnp.where(onehot, pad_start[None, None, :], 0), axis=-1)
            + rank_full[:, :TOP_K]).reshape(A)
    nb = (A + n_experts * (BM - 1) + BM - 1) // BM
    n_rows = nb * BM
    row_tok = jnp.zeros((n_rows,), jnp.int32).at[dest].set(jnp.arange(A, dtype=jnp.int32) // TOP_K)
    block_start = jnp.arange(nb, dtype=jnp.int32) * BM
    block_e = jnp.minimum(jnp.searchsorted(pad_end, block_start, side='right'),
                          n_experts - 1).astype(jnp.int32)
    n_act = (pad_end[-1] // BM).astype(jnp.int32).reshape(1)
    return dest.astype(jnp.int32), row_tok, block_e, n_act


def kernel(x, c, ada_w, ada_b, norm1_g, w_in, gm_vn_g, gm_vn_b, gm_ws, gm_bs, gla_wa2, gla_ba,
           gla_on_g, w_pa, w_pb, w_out, norm2_g, w_rg, b_rg, w_re, b_re, w_e_gate, w_e_up,
           w_e_down, final_g):
    B, S, D = x.shape
    T = B * S
    L = ada_w.shape[0]
    W = gm_vn_g.shape[1]
    G, C = gm_ws.shape[1], gm_ws.shape[2]
    RANK, DK = gla_wa2.shape[1], gla_wa2.shape[2]
    DV = gla_on_g.shape[1]
    E = w_e_gate.shape[1]
    per_group = E // MOE_GROUPS
    SUB = D // LANES
    assert B <= 8 and MOE_GROUPS + E <= LANES and RANK <= LANES

    c8 = jnp.zeros((8, D), F32).at[:B].set(c)
    mod = _ada_call(c8, ada_w, ada_b.reshape(L, 1, 6 * D))

    o_alr = 2 * W + 2 * DK + 2 * DV
    x2 = x.reshape(T, D)
    for l in range(L):
        sh1, sc1, g1, sh2, sc2, g2 = [mod[l, :B, k * D:(k + 1) * D].reshape(B, 1, D) for k in range(6)]
        wl = w_in[l]
        w_main = jnp.concatenate([wl[:, :o_alr], wl[:, o_alr + RANK:]], axis=1).astype(BF16)
        w_alr = jnp.zeros((D, LANES), BF16).at[:, :RANK].set(wl[:, o_alr:o_alr + RANK].astype(BF16))
        wa2p = jnp.zeros((LANES, DK), BF16).at[:RANK].set(gla_wa2[l].astype(BF16))
        bias_full = jnp.repeat(gm_bs[l].T, W // G, axis=1)
        wr = jnp.zeros((D, LANES), BF16).at[:, :MOE_GROUPS + E].set(
            jnp.concatenate([w_rg[l], w_re[l]], axis=1).astype(BF16))
        br = jnp.zeros((1, LANES), F32).at[0, :MOE_GROUPS + E].set(
            jnp.concatenate([b_rg[l], b_re[l]]))

        proj, alr = _inproj_call(x2, norm1_g[l].reshape(1, D), sc1, sh1, w_main, w_alr, S)
        ya = _gmlp_call(proj, gm_vn_g[l].reshape(1, W), gm_vn_b[l].reshape(1, W), gm_ws[l], bias_full)
        yb = _gla_call(proj, alr, wa2p, gla_ba[l].reshape(1, DK), gla_on_g[l].reshape(1, DV), B, S, W)
        x1, h2, logits = _merge_call(ya, yb, proj, w_pa[l].astype(BF16), w_pb[l].astype(BF16),
                                     w_out[l].astype(BF16), x2, g1, norm2_g[l].reshape(1, D),
                                     sc2, sh2, wr, S, W)
        eid, wts = _router_call(logits, br, per_group)
        dest, row_tok, block_e, n_act = _dispatch_plan(eid, E)
        n_rows = row_tok.shape[0]
        gather_blocks = (n_act * MOE_BLOCK + GATHER_ROWS - 1) // GATHER_ROWS
        xs = _gather_call(h2.reshape(T, SUB, LANES), row_tok, gather_blocks)
        ys = _experts_call(xs.reshape(n_rows, D), block_e, n_act, w_e_gate, w_e_up, w_e_down, l)
        all_blocks = jnp.full((1,), (T * TOP_K) // GATHER_ROWS, jnp.int32)
        yg = _gather_call(ys.reshape(n_rows, SUB, LANES), dest, all_blocks)
        x2 = _combine_call(x1, yg.reshape(T, TOP_K * D), wts, g2, final_g.reshape(1, D), S,
                           final_norm=(l == L - 1))
    return x2.reshape(B, S, D)
```

```python
import functools

import jax
import jax.numpy as jnp
from jax import lax
from jax.experimental import pallas as pl
from jax.experimental.pallas import tpu as pltpu

F32 = jnp.float32
BF16 = jnp.bfloat16

EPS = 1e-6
GLA_HEADS = 4
GLA_TAU = 16.0
GLA_CHUNK = 64
MOE_GROUPS = 4
TOP_K = 2

LANES = 128
MIB = 1024 * 1024
MOE_BLOCK = 256
GATHER_ROWS = 512

NT_DIMS = (((1,), (1,)), ((), ()))
TN_DIMS = (((0,), (0,)), ((), ()))


def _params(semantics, vmem_mib):
    return pltpu.CompilerParams(dimension_semantics=semantics,
                                vmem_limit_bytes=int(vmem_mib * MIB))


def _dot(a, b):
    return jnp.dot(a, b, preferred_element_type=F32)


def _sigmoid(x):
    return 1.0 / (1.0 + jnp.exp(-x))


def _gelu(x):
    return 0.5 * x * (1.0 + lax.erf(x * (2.0 ** -0.5)))


def _pack_pairs(lo, hi):
    lo_w = pltpu.bitcast(lo.astype(BF16).astype(F32), jnp.uint32) >> 16
    hi_w = pltpu.bitcast(hi.astype(BF16).astype(F32), jnp.uint32) & jnp.uint32(0xFFFF0000)
    return lo_w | hi_w


def _unpack_pairs(w):
    lo = pltpu.bitcast(w << 16, F32)
    hi = pltpu.bitcast(w & jnp.uint32(0xFFFF0000), F32)
    return lo, hi


def _store_packed(ref, x):
    rows, d = x.shape
    half = d // 2
    n_s = half // LANES
    for s in range(n_s):
        lo = x[:, s * LANES:(s + 1) * LANES]
        hi = x[:, half + s * LANES:half + (s + 1) * LANES]
        ref[pl.ds(s, rows, stride=n_s), :] = _pack_pairs(lo, hi)


def _load_packed(ref, rows, n_s, offset=0, group=1):
    los, his = [], []
    for s in range(n_s):
        lo, hi = _unpack_pairs(ref[pl.ds(offset * n_s + s, rows, stride=n_s * group), :])
        los.append(lo)
        his.append(hi)
    return los, his


def _ada_kernel(c_ref, w_ref, b_ref, o_ref):
    c = c_ref[...]
    cond = c * _sigmoid(c)
    o_ref[0] = _dot(cond.astype(BF16), w_ref[0].astype(BF16)) + b_ref[0]


def _ada_call(c8, ada_w, ada_b3):
    L, D, N = ada_w.shape
    tn = min(1024, N)
    return pl.pallas_call(
        _ada_kernel,
        grid=(L, N // tn),
        in_specs=[pl.BlockSpec((8, D), lambda l, j: (0, 0)),
                  pl.BlockSpec((1, D, tn), lambda l, j: (l, 0, j)),
                  pl.BlockSpec((1, 1, tn), lambda l, j: (l, 0, j))],
        out_specs=pl.BlockSpec((1, 8, tn), lambda l, j: (l, 0, j)),
        out_shape=jax.ShapeDtypeStruct((L, 8, N), F32),
        compiler_params=_params(("parallel", "parallel"), 40),
        name="ada_mod",
    )(c8, ada_w, ada_b3)


def _inproj_kernel(x_ref, g_ref, sc_ref, sh_ref, w_ref, walr_ref, o_ref, alr_ref, h_scr):
    @pl.when(pl.program_id(1) == 0)
    def _():
        x = x_ref[...]
        ms = jnp.mean(x * x, axis=-1, keepdims=True)
        y = x * lax.rsqrt(ms + EPS) * g_ref[...]
        h = (y * (1.0 + sc_ref[0]) + sh_ref[0]).astype(BF16)
        h_scr[...] = h
        alr_ref[...] = _dot(h, walr_ref[...])

    o_ref[...] = _dot(h_scr[...], w_ref[...]).astype(o_ref.dtype)


def _inproj_call(x2, g, sc, sh, w_main, w_alr, S):
    T, D = x2.shape
    NM = w_main.shape[1]
    tm = min(1024, S)
    tn = min(1024, NM)
    per_b = S // tm
    return pl.pallas_call(
        _inproj_kernel,
        grid=(T // tm, NM // tn),
        in_specs=[pl.BlockSpec((tm, D), lambda i, j: (i, 0)),
                  pl.BlockSpec((1, D), lambda i, j: (0, 0)),
                  pl.BlockSpec((1, 1, D), lambda i, j: (i // per_b, 0, 0)),
                  pl.BlockSpec((1, 1, D), lambda i, j: (i // per_b, 0, 0)),
                  pl.BlockSpec((D, tn), lambda i, j: (0, j)),
                  pl.BlockSpec((D, LANES), lambda i, j: (0, 0))],
        out_specs=[pl.BlockSpec((tm, tn), lambda i, j: (i, j)),
                   pl.BlockSpec((tm, LANES), lambda i, j: (i, 0))],
        out_shape=[jax.ShapeDtypeStruct((T, NM), BF16),
                   jax.ShapeDtypeStruct((T, LANES), F32)],
        scratch_shapes=[pltpu.VMEM((tm, D), BF16)],
        compiler_params=_params(("parallel", "arbitrary"), 56),
        name="in_proj",
    )(x2, g, sc, sh, w_main, w_alr)


def _gmlp_kernel(u_ref, v_ref, g_ref, b_ref, ws_ref, bias_ref, o_ref, vn_scr, *, groups, chunk):
    rows, width = v_ref.shape
    gc = width // groups
    r = lax.broadcasted_iota(jnp.int32, (chunk, chunk), 0)
    c = lax.broadcasted_iota(jnp.int32, (chunk, chunk), 1)
    causal = r >= c
    for ci in range(rows // chunk):
        rs = slice(ci * chunk, (ci + 1) * chunk)
        gv = _gelu(v_ref[rs, :].astype(F32))
        mu = jnp.mean(gv, axis=-1, keepdims=True)
        xc = gv - mu
        var = jnp.mean(xc * xc, axis=-1, keepdims=True)
        vn = xc * lax.rsqrt(var + EPS) * g_ref[...] + b_ref[...]
        vn_scr[...] = vn.astype(BF16)
        for gi in range(groups):
            cs = slice(gi * gc, (gi + 1) * gc)
            wm = jnp.where(causal, ws_ref[gi], 0.0).astype(BF16)
            mixed = _dot(wm, vn_scr[:, cs]) + bias_ref[:, cs]
            gu = _gelu(u_ref[rs, cs].astype(F32))
            o_ref[rs, cs] = (gu * mixed).astype(o_ref.dtype)


def _gmlp_call(proj, vn_g, vn_b, ws, bias_full):
    T = proj.shape[0]
    G, C, _ = ws.shape
    W = vn_g.shape[1]
    R = min(2 * C, T)
    kern = functools.partial(_gmlp_kernel, groups=G, chunk=C)
    return pl.pallas_call(
        kern,
        grid=(T // R,),
        in_specs=[pl.BlockSpec((R, W), lambda i: (i, 0)),
                  pl.BlockSpec((R, W), lambda i: (i, 1)),
                  pl.BlockSpec((1, W), lambda i: (0, 0)),
                  pl.BlockSpec((1, W), lambda i: (0, 0)),
                  pl.BlockSpec((G, C, C), lambda i: (0, 0, 0)),
                  pl.BlockSpec((C, W), lambda i: (0, 0))],
        out_specs=pl.BlockSpec((R, W), lambda i: (i, 0)),
        out_shape=jax.ShapeDtypeStruct((T, W), BF16),
        scratch_shapes=[pltpu.VMEM((C, W), BF16)],
        compiler_params=_params(("parallel",), 32),
        name="gmlp_gate",
    )(proj, proj, vn_g, vn_b, ws, bias_full)


def _gla_kernel(q_ref, k_ref, v_ref, r_ref, alr_ref, wa2_ref, ba_ref, g_ref, o_ref,
                st_ref, la_ref, *, heads):
    rows, dk = q_ref.shape
    dv = v_ref.shape[1]
    hk = dk // heads
    hv = dv // heads
    C = GLA_CHUNK
    scale = hk ** -0.5

    @pl.when(pl.program_id(1) == 0)
    def _():
        st_ref[...] = jnp.zeros(st_ref.shape, F32)

    z = _dot(alr_ref[...].astype(BF16), wa2_ref[...]) + ba_ref[...]
    la_ref[...] = (jnp.minimum(z, 0.0) - jnp.log1p(jnp.exp(-jnp.abs(z)))) * (1.0 / GLA_TAU)

    ri = lax.broadcasted_iota(jnp.int32, (C, C), 0)
    ci = lax.broadcasted_iota(jnp.int32, (C, C), 1)
    causal = ri >= ci
    tri = causal.astype(BF16)

    def chunk_step(c, carry):
        rs = pl.ds(pl.multiple_of(c * C, C), C)
        la = la_ref[rs, :]
        hi = la.astype(BF16)
        r1 = la - hi.astype(F32)
        mid = r1.astype(BF16)
        lo = (r1 - mid.astype(F32)).astype(BF16)
        cum = _dot(tri, hi) + _dot(tri, mid) + _dot(tri, lo)
        cl = cum[C - 1:C, :]
        q = q_ref[rs, :].astype(F32) * scale
        k = k_ref[rs, :].astype(F32)
        qd = (q * jnp.exp(cum)).astype(BF16)
        ki = (k * jnp.exp(-cum)).astype(BF16)
        ke = (k * jnp.exp(cl - cum)).astype(BF16)
        dec = jnp.exp(cl)
        for h in range(heads):
            ks = slice(h * hk, (h + 1) * hk)
            vs = slice(h * hv, (h + 1) * hv)
            vh = v_ref[rs, vs]
            s = lax.dot_general(qd[:, ks], ki[:, ks], NT_DIMS, preferred_element_type=F32)
            s = jnp.where(causal, s, 0.0).astype(BF16)
            st = st_ref[h]
            o = _dot(s, vh) + lax.dot_general(qd[:, ks], st.astype(BF16), NT_DIMS,
                                              preferred_element_type=F32)
            st_ref[h] = st * dec[:, ks] + lax.dot_general(vh, ke[:, ks], TN_DIMS,
                                                          preferred_element_type=F32)
            ms = jnp.mean(o * o, axis=-1, keepdims=True)
            on = o * lax.rsqrt(ms + EPS) * g_ref[:, vs]
            rr = r_ref[rs, vs].astype(F32)
            o_ref[rs, vs] = (on * (rr * _sigmoid(rr))).astype(o_ref.dtype)
        return carry

    lax.fori_loop(0, rows // C, chunk_step, 0)


def _gla_call(proj, alr, wa2p, ba, on_g, B, S, W):
    DK = wa2p.shape[1]
    DV = on_g.shape[1]
    H = GLA_HEADS
    Cb = min(256, S)
    nb = S // Cb
    q_blk = (2 * W) // DK
    k_blk = (2 * W + DK) // DK
    v_blk = (2 * W + 2 * DK) // DV
    r_blk = (2 * W + 2 * DK + DV) // DV
    kern = functools.partial(_gla_kernel, heads=H)
    return pl.pallas_call(
        kern,
        grid=(B, nb),
        in_specs=[pl.BlockSpec((Cb, DK), lambda b, i: (b * nb + i, q_blk)),
                  pl.BlockSpec((Cb, DK), lambda b, i: (b * nb + i, k_blk)),
                  pl.BlockSpec((Cb, DV), lambda b, i: (b * nb + i, v_blk)),
                  pl.BlockSpec((Cb, DV), lambda b, i: (b * nb + i, r_blk)),
                  pl.BlockSpec((Cb, LANES), lambda b, i: (b * nb + i, 0)),
                  pl.BlockSpec((LANES, DK), lambda b, i: (0, 0)),
                  pl.BlockSpec((1, DK), lambda b, i: (0, 0)),
                  pl.BlockSpec((1, DV), lambda b, i: (0, 0))],
        out_specs=pl.BlockSpec((Cb, DV), lambda b, i: (b * nb + i, 0)),
        out_shape=jax.ShapeDtypeStruct((B * S, DV), BF16),
        scratch_shapes=[pltpu.VMEM((H, DV // H, DK // H), F32),
                        pltpu.VMEM((Cb, DK), F32)],
        compiler_params=_params(("parallel", "arbitrary"), 32),
        name="gla",
    )(proj, proj, proj, proj, alr, wa2p, ba, on_g)


def _merge_kernel(ya_ref, yb_ref, ga_ref, gb_ref, wpa_ref, wpb_ref, wout_ref, x_ref, g1_ref,
                  n2_ref, sc_ref, sh_ref, wr_ref, x1_ref, h2_ref, lg_ref):
    a = _dot(ya_ref[...], wpa_ref[...])
    b = _dot(yb_ref[...], wpb_ref[...])
    y = _sigmoid(ga_ref[...].astype(F32)) * a + _sigmoid(gb_ref[...].astype(F32)) * b
    out = _dot(y.astype(BF16), wout_ref[...])
    x1 = x_ref[...] + g1_ref[0] * out
    x1_ref[...] = x1
    ms = jnp.mean(x1 * x1, axis=-1, keepdims=True)
    h2 = (x1 * lax.rsqrt(ms + EPS) * n2_ref[...]) * (1.0 + sc_ref[0]) + sh_ref[0]
    _store_packed(h2_ref, h2)
    lg_ref[...] = _dot(h2.astype(BF16), wr_ref[...])


def _merge_call(ya, yb, proj, wpa, wpb, wout, x2, g1, n2g, sc2, sh2, wr, S, W):
    T, D = x2.shape
    ga_blk = (proj.shape[1] - 2 * D) // D
    gb_blk = ga_blk + 1
    tm = min(256, S)
    per_b = S // tm
    n_s = D // (2 * LANES)
    const = dict(pipeline_mode=pl.Buffered(1))
    return pl.pallas_call(
        _merge_kernel,
        grid=(T // tm,),
        in_specs=[pl.BlockSpec((tm, W), lambda i: (i, 0)),
                  pl.BlockSpec((tm, W), lambda i: (i, 0)),
                  pl.BlockSpec((tm, D), lambda i: (i, ga_blk)),
                  pl.BlockSpec((tm, D), lambda i: (i, gb_blk)),
                  pl.BlockSpec((W, D), lambda i: (0, 0), **const),
                  pl.BlockSpec((W, D), lambda i: (0, 0), **const),
                  pl.BlockSpec((D, D), lambda i: (0, 0), **const),
                  pl.BlockSpec((tm, D), lambda i: (i, 0)),
                  pl.BlockSpec((1, 1, D), lambda i: (i // per_b, 0, 0)),
                  pl.BlockSpec((1, D), lambda i: (0, 0)),
                  pl.BlockSpec((1, 1, D), lambda i: (i // per_b, 0, 0)),
                  pl.BlockSpec((1, 1, D), lambda i: (i // per_b, 0, 0)),
                  pl.BlockSpec((D, LANES), lambda i: (0, 0), **const)],
        out_specs=[pl.BlockSpec((tm, D), lambda i: (i, 0)),
                   pl.BlockSpec((tm * n_s, LANES), lambda i: (i, 0)),
                   pl.BlockSpec((tm, LANES), lambda i: (i, 0))],
        out_shape=[jax.ShapeDtypeStruct((T, D), F32),
                   jax.ShapeDtypeStruct((T * n_s, LANES), jnp.uint32),
                   jax.ShapeDtypeStruct((T, LANES), F32)],
        compiler_params=_params(("parallel",), 56),
        name="merge_out",
    )(ya, yb, proj, proj, wpa, wpb, wout, x2, g1, n2g, sc2, sh2, wr)


def _router_kernel(lg_ref, b_ref, e_ref, w_ref, *, groups, per_group):
    lg = lg_ref[...] + b_ref[...]
    lane = lax.broadcasted_iota(jnp.int32, lg.shape, 1)
    lane_f = lane.astype(F32)
    neg = jnp.float32(-1e30)
    big = jnp.float32(LANES)

    def first_argmax(vals):
        m = jnp.max(vals, axis=-1, keepdims=True)
        idx = jnp.min(jnp.where(vals == m, lane_f, big), axis=-1, keepdims=True)
        return m, idx

    gmask = lane < groups
    gl = jnp.where(gmask, lg, neg)
    gmax, gidx = first_argmax(gl)
    gsum = jnp.sum(jnp.where(gmask, jnp.exp(gl - gmax), 0.0), axis=-1, keepdims=True)
    gtop = 1.0 / gsum

    lo = groups + gidx * per_group
    emask = (lane_f >= lo) & (lane_f < lo + per_group)
    el = jnp.where(emask, lg, neg)
    m1, i1 = first_argmax(el)
    el2 = jnp.where(lane_f == i1, neg, el)
    m2, i2 = first_argmax(el2)
    den = jnp.sum(jnp.where(emask, jnp.exp(el - m1), 0.0), axis=-1, keepdims=True)
    p1 = 1.0 / den
    p2 = jnp.exp(m2 - m1) / den
    ps = p1 + p2
    w1 = gtop * p1 / ps
    w2 = gtop * p2 / ps
    e1 = (i1 - groups).astype(jnp.int32)
    e2 = (i2 - groups).astype(jnp.int32)
    e_ref[...] = jnp.where(lane == 0, e1, jnp.where(lane == 1, e2, 0))
    w_ref[...] = jnp.where(lane == 0, w1, jnp.where(lane == 1, w2, 0.0))


def _router_call(logits, bias, per_group):
    T = logits.shape[0]
    tm = min(1024, T)
    kern = functools.partial(_router_kernel, groups=MOE_GROUPS, per_group=per_group)
    return pl.pallas_call(
        kern,
        grid=(T // tm,),
        in_specs=[pl.BlockSpec((tm, LANES), lambda i: (i, 0)),
                  pl.BlockSpec((1, LANES), lambda i: (0, 0))],
        out_specs=[pl.BlockSpec((tm, LANES), lambda i: (i, 0)),
                   pl.BlockSpec((tm, LANES), lambda i: (i, 0))],
        out_shape=[jax.ShapeDtypeStruct((T, LANES), jnp.int32),
                   jax.ShapeDtypeStruct((T, LANES), F32)],
        compiler_params=_params(("parallel",), 16),
        name="router_topk",
    )(logits, bias)


def _row(ref, t, n_s):
    return ref.at[pl.ds(pl.multiple_of(t * n_s, n_s), n_s)]


def _gather_kernel(idx_ref, src_ref, out_ref, sem, *, n_s):
    G = idx_ref.shape[2]

    def issue(r, c):
        pltpu.make_async_copy(_row(src_ref, idx_ref[0, 0, r], n_s), _row(out_ref, r, n_s),
                              sem.at[0]).start()
        return c
    lax.fori_loop(0, G, issue, 0, unroll=8)
    pltpu.make_async_copy(src_ref.at[pl.ds(0, G * n_s)], out_ref, sem.at[0]).wait()


def _gather_call(src, idx, n_s):
    M = idx.shape[0]
    G = GATHER_ROWS
    nb = M // G
    kern = functools.partial(_gather_kernel, n_s=n_s)
    return pl.pallas_call(
        kern,
        grid=(nb,),
        in_specs=[pl.BlockSpec((1, 1, G), lambda i: (i, 0, 0), memory_space=pltpu.SMEM),
                  pl.BlockSpec(memory_space=pl.ANY)],
        out_specs=pl.BlockSpec((G * n_s, LANES), lambda i: (i, 0)),
        out_shape=jax.ShapeDtypeStruct((M * n_s, LANES), src.dtype),
        scratch_shapes=[pltpu.SemaphoreType.DMA((1,))],
        compiler_params=_params(("arbitrary",), 16),
        name="row_gather",
    )(idx.reshape(nb, 1, G), src)


def _dispatch_kernel(zr_ref, na_ref, dest_ref, h_ref, out_ref, zbuf, sem, zsem, *, n_s, block):
    n_assign = dest_ref.shape[2]
    tm = n_assign // TOP_K
    nb = out_ref.shape[0] // (block * n_s)

    def zero_block(b):
        start = pl.multiple_of(b * (block * n_s), block * n_s)
        return pltpu.make_async_copy(zbuf, out_ref.at[pl.ds(start, block * n_s)], zsem.at[0])

    @pl.when(pl.program_id(0) == 0)
    def _():
        zbuf[...] = jnp.zeros(zbuf.shape, zbuf.dtype)
        n_exp = zr_ref.shape[0]
        lax.fori_loop(0, n_exp, lambda e, c: (zero_block(zr_ref[e]).start(), c)[1], 0)
        lax.fori_loop(na_ref[0], nb, lambda b, c: (zero_block(b).start(), c)[1], 0)
        lax.fori_loop(0, n_exp, lambda e, c: (zero_block(0).wait(), c)[1], 0)
        lax.fori_loop(na_ref[0], nb, lambda b, c: (zero_block(0).wait(), c)[1], 0)

    def issue(a, c):
        pltpu.make_async_copy(_row(h_ref, a // TOP_K, n_s), _row(out_ref, dest_ref[0, 0, a], n_s),
                              sem.at[0]).start()
        return c
    lax.fori_loop(0, n_assign, issue, 0, unroll=8)
    for _ in range(TOP_K):
        pltpu.make_async_copy(h_ref, out_ref.at[pl.ds(0, tm * n_s)], sem.at[0]).wait()


def _dispatch_call(h2p, dest, zero_rows, n_act, n_rows, n_s):
    T = h2p.shape[0] // n_s
    tm = min(256, T)
    nt = T // tm
    kern = functools.partial(_dispatch_kernel, n_s=n_s, block=MOE_BLOCK)
    grid_spec = pltpu.PrefetchScalarGridSpec(
        num_scalar_prefetch=2,
        grid=(nt,),
        in_specs=[pl.BlockSpec((1, 1, tm * TOP_K), lambda i, zr, na: (i, 0, 0),
                               memory_space=pltpu.SMEM),
                  pl.BlockSpec((tm * n_s, LANES), lambda i, zr, na: (i, 0))],
        out_specs=pl.BlockSpec(memory_space=pl.ANY),
        scratch_shapes=[pltpu.VMEM((MOE_BLOCK * n_s, LANES), h2p.dtype),
                        pltpu.SemaphoreType.DMA((1,)),
                        pltpu.SemaphoreType.DMA((1,))],
    )
    return pl.pallas_call(
        kern,
        grid_spec=grid_spec,
        out_shape=jax.ShapeDtypeStruct((n_rows * n_s, LANES), h2p.dtype),
        compiler_params=_params(("arbitrary",), 16),
        name="row_dispatch",
    )(zero_rows, n_act, dest.reshape(nt, 1, tm * TOP_K), h2p)


def _expert_changed(be_ref, na_ref, i):
    prev = be_ref[jnp.maximum(i - 1, 0)]
    return (i == 0) | (be_ref[i] != prev)


def _expert_up_kernel(be_ref, na_ref, xs_ref, wg_ref, wu_ref, hid_ref, wgb, wub):
    i = pl.program_id(1)
    active = i < na_ref[0]

    @pl.when(active & _expert_changed(be_ref, na_ref, i))
    def _():
        wgb[...] = wg_ref[0, 0].astype(BF16)
        wub[...] = wu_ref[0, 0].astype(BF16)

    @pl.when(active)
    def _():
        rows = hid_ref.shape[0]
        los, his = _load_packed(xs_ref, rows, xs_ref.shape[0] // rows)
        x = jnp.concatenate([p.astype(BF16) for p in los + his], axis=1)
        a = _dot(x, wgb[...])
        b = _dot(x, wub[...])
        hid_ref[...] = (a * _sigmoid(a) * b).astype(hid_ref.dtype)

    @pl.when(jnp.logical_not(active))
    def _():
        hid_ref[...] = jnp.zeros(hid_ref.shape, hid_ref.dtype)


def _expert_down_kernel(be_ref, na_ref, hid_ref, wd_ref, ys_ref, wdb):
    i = pl.program_id(0)
    active = i < na_ref[0]

    @pl.when(active & _expert_changed(be_ref, na_ref, i))
    def _():
        wdb[...] = wd_ref[0, 0].astype(BF16)

    @pl.when(active)
    def _():
        _store_packed(ys_ref, _dot(hid_ref[...], wdb[...]))

    @pl.when(jnp.logical_not(active))
    def _():
        ys_ref[...] = jnp.zeros(ys_ref.shape, ys_ref.dtype)


def _experts_call(xs, block_e, n_act, w_gate, w_up, w_down, l):
    D, DE = w_gate.shape[-2], w_gate.shape[-1]
    n_s = D // (2 * LANES)
    n_rows = xs.shape[0] // n_s
    BM = MOE_BLOCK
    nb = n_rows // BM
    tj = min(512, DE)

    def blk(i, na):
        return jnp.minimum(i, na[0] - 1)

    up_spec = pltpu.PrefetchScalarGridSpec(
        num_scalar_prefetch=2,
        grid=(DE // tj, nb),
        in_specs=[pl.BlockSpec((BM * n_s, LANES), lambda j, i, be, na: (blk(i, na), 0)),
                  pl.BlockSpec((1, 1, D, tj), lambda j, i, be, na: (l, be[blk(i, na)], 0, j)),
                  pl.BlockSpec((1, 1, D, tj), lambda j, i, be, na: (l, be[blk(i, na)], 0, j))],
        out_specs=pl.BlockSpec((BM, tj), lambda j, i, be, na: (i, j)),
        scratch_shapes=[pltpu.VMEM((D, tj), BF16), pltpu.VMEM((D, tj), BF16)],
    )
    hid = pl.pallas_call(
        _expert_up_kernel,
        grid_spec=up_spec,
        out_shape=jax.ShapeDtypeStruct((n_rows, DE), BF16),
        compiler_params=_params(("arbitrary", "arbitrary"), 48),
        name="expert_up",
    )(block_e, n_act, xs, w_gate, w_up)

    down_spec = pltpu.PrefetchScalarGridSpec(
        num_scalar_prefetch=2,
        grid=(nb,),
        in_specs=[pl.BlockSpec((BM, DE), lambda i, be, na: (blk(i, na), 0)),
                  pl.BlockSpec((1, 1, DE, D), lambda i, be, na: (l, be[blk(i, na)], 0, 0))],
        out_specs=pl.BlockSpec((BM * n_s, LANES), lambda i, be, na: (i, 0)),
        scratch_shapes=[pltpu.VMEM((DE, D), BF16)],
    )
    return pl.pallas_call(
        _expert_down_kernel,
        grid_spec=down_spec,
        out_shape=jax.ShapeDtypeStruct((n_rows * n_s, LANES), jnp.uint32),
        compiler_params=_params(("arbitrary",), 48),
        name="expert_down",
    )(block_e, n_act, hid, w_down)


def _combine_kernel(x_ref, yg_ref, w_ref, g2_ref, fg_ref, o_ref, *, final_norm):
    tm, d = x_ref.shape
    half = d // 2
    n_s = half // LANES
    w = w_ref[...]
    w0 = w[:, 0:1]
    w1 = w[:, 1:2]
    lo0, hi0 = _load_packed(yg_ref, tm, n_s, offset=0, group=TOP_K)
    lo1, hi1 = _load_packed(yg_ref, tm, n_s, offset=1, group=TOP_K)
    ss = jnp.zeros((tm, 1), F32)
    for s in range(n_s):
        for base, p0, p1 in ((0, lo0[s], lo1[s]), (half, hi0[s], hi1[s])):
            cs = slice(base + s * LANES, base + (s + 1) * LANES)
            xv = x_ref[:, cs] + g2_ref[0, :, cs] * (w0 * p0 + w1 * p1)
            o_ref[:, cs] = xv
            if final_norm:
                ss = ss + jnp.sum(xv * xv, axis=-1, keepdims=True)
    if final_norm:
        o_ref[...] = o_ref[...] * lax.rsqrt(ss * (1.0 / d) + EPS) * fg_ref[...]


def _combine_call(x1, yg, wts, g2, final_g, S, final_norm):
    T, D = x1.shape
    tm = min(512, S)
    per_b = S // tm
    n_s = D // (2 * LANES)
    kern = functools.partial(_combine_kernel, final_norm=final_norm)
    return pl.pallas_call(
        kern,
        grid=(T // tm,),
        in_specs=[pl.BlockSpec((tm, D), lambda i: (i, 0)),
                  pl.BlockSpec((tm * TOP_K * n_s, LANES), lambda i: (i, 0)),
                  pl.BlockSpec((tm, LANES), lambda i: (i, 0)),
                  pl.BlockSpec((1, 1, D), lambda i: (i // per_b, 0, 0)),
                  pl.BlockSpec((1, D), lambda i: (0, 0))],
        out_specs=pl.BlockSpec((tm, D), lambda i: (i, 0)),
        out_shape=jax.ShapeDtypeStruct((T, D), F32),
        compiler_params=_params(("parallel",), 40),
        name="moe_combine",
    )(x1, yg, wts, g2, final_g)


def _rank_kernel(e_ref, rank_ref, cnt_ref, carry):
    @pl.when(pl.program_id(0) == 0)
    def _():
        carry[...] = jnp.zeros(carry.shape, F32)

    e = e_ref[...]
    tb = e.shape[0]
    lane = lax.broadcasted_iota(jnp.int32, e.shape, 1)
    oh0 = lane == e[:, 0:1]
    oh1 = lane == e[:, 1:2]
    both = oh0.astype(F32) + oh1.astype(F32)
    r = lax.broadcasted_iota(jnp.int32, (tb, tb), 0)
    c = lax.broadcasted_iota(jnp.int32, (tb, tb), 1)
    strict = (r > c).astype(BF16)
    before = _dot(strict, both.astype(BF16)) + carry[0:1, :]
    r0 = jnp.sum(jnp.where(oh0, before, 0.0), axis=-1, keepdims=True)
    r1 = jnp.sum(jnp.where(oh1, before, 0.0), axis=-1, keepdims=True)
    rank_ref[...] = jnp.where(lane == 0, r0, jnp.where(lane == 1, r1, 0.0)).astype(jnp.int32)
    carry[...] = carry[...] + jnp.sum(both, axis=0, keepdims=True)
    cnt_ref[...] = carry[...]


def _rank_call(eid):
    T = eid.shape[0]
    tb = min(512, T)
    return pl.pallas_call(
        _rank_kernel,
        grid=(T // tb,),
        in_specs=[pl.BlockSpec((tb, LANES), lambda i: (i, 0))],
        out_specs=[pl.BlockSpec((tb, LANES), lambda i: (i, 0)),
                   pl.BlockSpec((8, LANES), lambda i: (0, 0))],
        out_shape=[jax.ShapeDtypeStruct((T, LANES), jnp.int32),
                   jax.ShapeDtypeStruct((8, LANES), F32)],
        scratch_shapes=[pltpu.VMEM((8, LANES), F32)],
        compiler_params=_params(("arbitrary",), 16),
        name="assign_rank",
    )(eid)


def _dispatch_plan(eid_full, n_experts):
    T = eid_full.shape[0]
    A = T * TOP_K
    BM = MOE_BLOCK
    rank_full, cnt = _rank_call(eid_full)
    eid = eid_full[:, :TOP_K]
    counts = cnt[0, :n_experts].astype(jnp.int32)
    padded = (counts + BM - 1) // BM * BM
    pad_end = jnp.cumsum(padded)
    pad_start = pad_end - padded
    onehot = eid[:, :, None] == jnp.arange(n_experts, dtype=jnp.int32)[None, None, :]
    dest = (jnp.sum(jnp.where(onehot, pad_start[None, None, :], 0), axis=-1)
            + rank_full[:, :TOP_K]).reshape(A)
    nb = (A + n_experts * (BM - 1) + BM - 1) // BM
    block_start = jnp.arange(nb, dtype=jnp.int32) * BM
    block_e = jnp.minimum(jnp.sum(block_start[:, None] >= pad_end[None, :], axis=1),
                          n_experts - 1).astype(jnp.int32)
    n_act = (pad_end[-1] // BM).astype(jnp.int32).reshape(1)
    last_block = jnp.maximum(pad_end // BM - 1, 0).astype(jnp.int32)
    return dest.astype(jnp.int32), block_e, n_act, last_block, nb * BM


def kernel(x, c, ada_w, ada_b, norm1_g, w_in, gm_vn_g, gm_vn_b, gm_ws, gm_bs, gla_wa2, gla_ba,
           gla_on_g, w_pa, w_pb, w_out, norm2_g, w_rg, b_rg, w_re, b_re, w_e_gate, w_e_up,
           w_e_down, final_g):
    B, S, D = x.shape
    T = B * S
    L = ada_w.shape[0]
    W = gm_vn_g.shape[1]
    G, C = gm_ws.shape[1], gm_ws.shape[2]
    RANK, DK = gla_wa2.shape[1], gla_wa2.shape[2]
    DV = gla_on_g.shape[1]
    E = w_e_gate.shape[1]
    per_group = E // MOE_GROUPS
    SUB = D // (2 * LANES)
    assert B <= 8 and MOE_GROUPS + E <= LANES and RANK <= LANES

    c8 = jnp.zeros((8, D), F32).at[:B].set(c)
    mod = _ada_call(c8, ada_w, ada_b.reshape(L, 1, 6 * D))

    o_alr = 2 * W + 2 * DK + 2 * DV
    x2 = x.reshape(T, D)
    for l in range(L):
        sh1, sc1, g1, sh2, sc2, g2 = [mod[l, :B, k * D:(k + 1) * D].reshape(B, 1, D) for k in range(6)]
        wl = w_in[l]
        w_main = jnp.concatenate([wl[:, :o_alr], wl[:, o_alr + RANK:]], axis=1).astype(BF16)
        w_alr = jnp.zeros((D, LANES), BF16).at[:, :RANK].set(wl[:, o_alr:o_alr + RANK].astype(BF16))
        wa2p = jnp.zeros((LANES, DK), BF16).at[:RANK].set(gla_wa2[l].astype(BF16))
        bias_full = jnp.repeat(gm_bs[l].T, W // G, axis=1)
        wr = jnp.zeros((D, LANES), BF16).at[:, :MOE_GROUPS + E].set(
            jnp.concatenate([w_rg[l], w_re[l]], axis=1).astype(BF16))
        br = jnp.zeros((1, LANES), F32).at[0, :MOE_GROUPS + E].set(
            jnp.concatenate([b_rg[l], b_re[l]]))

        proj, alr = _inproj_call(x2, norm1_g[l].reshape(1, D), sc1, sh1, w_main, w_alr, S)
        ya = _gmlp_call(proj, gm_vn_g[l].reshape(1, W), gm_vn_b[l].reshape(1, W), gm_ws[l], bias_full)
        yb = _gla_call(proj, alr, wa2p, gla_ba[l].reshape(1, DK), gla_on_g[l].reshape(1, DV), B, S, W)
        x1, h2, logits = _merge_call(ya, yb, proj, w_pa[l].astype(BF16), w_pb[l].astype(BF16),
                                     w_out[l].astype(BF16), x2, g1, norm2_g[l].reshape(1, D),
                                     sc2, sh2, wr, S, W)
        eid, wts = _router_call(logits, br, per_group)
        dest, block_e, n_act, last_block, n_rows = _dispatch_plan(eid, E)
        xs = _dispatch_call(h2, dest, last_block, n_act, n_rows, SUB)
        ys = _experts_call(xs, block_e, n_act, w_e_gate, w_e_up, w_e_down, l)
        yg = _gather_call(ys, dest, SUB)
        x2 = _combine_call(x1, yg, wts, g2, final_g.reshape(1, D), S, final_norm=(l == L - 1))
    return x2.reshape(B, S, D)
```

```python
import functools

import jax
import jax.numpy as jnp
from jax import lax
from jax.experimental import pallas as pl
from jax.experimental.pallas import tpu as pltpu

F32 = jnp.float32
BF16 = jnp.bfloat16

EPS = 1e-6
GLA_HEADS = 4
GLA_TAU = 16.0
GLA_CHUNK = 64
MOE_GROUPS = 4
TOP_K = 2

LANES = 128
MIB = 1024 * 1024
MOE_BLOCK = 256
GATHER_ROWS = 1024

NT_DIMS = (((1,), (1,)), ((), ()))
TN_DIMS = (((0,), (0,)), ((), ()))


def _params(semantics, vmem_mib):
    return pltpu.CompilerParams(dimension_semantics=semantics,
                                vmem_limit_bytes=int(vmem_mib * MIB))


def _dot(a, b):
    return jnp.dot(a, b, preferred_element_type=F32)


def _sigmoid(x):
    return 1.0 / (1.0 + jnp.exp(-x))


def _gelu(x):
    return 0.5 * x * (1.0 + lax.erf(x * (2.0 ** -0.5)))


def _pack_pairs(lo, hi):
    lo_w = pltpu.bitcast(lo.astype(BF16).astype(F32), jnp.uint32) >> 16
    hi_w = pltpu.bitcast(hi.astype(BF16).astype(F32), jnp.uint32) & jnp.uint32(0xFFFF0000)
    return lo_w | hi_w


def _unpack_pairs(w):
    lo = pltpu.bitcast(w << 16, F32)
    hi = pltpu.bitcast(w & jnp.uint32(0xFFFF0000), F32)
    return lo, hi


def _store_packed(ref, x):
    rows, d = x.shape
    half = d // 2
    n_s = half // LANES
    for s in range(n_s):
        lo = x[:, s * LANES:(s + 1) * LANES]
        hi = x[:, half + s * LANES:half + (s + 1) * LANES]
        ref[pl.ds(s, rows, stride=n_s), :] = _pack_pairs(lo, hi)


def _load_packed(ref, rows, n_s, offset=0, group=1):
    los, his = [], []
    for s in range(n_s):
        lo, hi = _unpack_pairs(ref[pl.ds(offset * n_s + s, rows, stride=n_s * group), :])
        los.append(lo)
        his.append(hi)
    return los, his


def _ada_kernel(c_ref, w_ref, b_ref, o_ref):
    c = c_ref[...]
    cond = c * _sigmoid(c)
    o_ref[0] = _dot(cond.astype(BF16), w_ref[0].astype(BF16)) + b_ref[0]


def _ada_call(c8, ada_w, ada_b3):
    L, D, N = ada_w.shape
    tn = min(1024, N)
    return pl.pallas_call(
        _ada_kernel,
        grid=(L, N // tn),
        in_specs=[pl.BlockSpec((8, D), lambda l, j: (0, 0)),
                  pl.BlockSpec((1, D, tn), lambda l, j: (l, 0, j)),
                  pl.BlockSpec((1, 1, tn), lambda l, j: (l, 0, j))],
        out_specs=pl.BlockSpec((1, 8, tn), lambda l, j: (l, 0, j)),
        out_shape=jax.ShapeDtypeStruct((L, 8, N), F32),
        compiler_params=_params(("parallel", "parallel"), 40),
        name="ada_mod",
    )(c8, ada_w, ada_b3)


def _inproj_kernel(x_ref, g_ref, sc_ref, sh_ref, w_ref, walr_ref, o_ref, alr_ref, h_scr):
    @pl.when(pl.program_id(1) == 0)
    def _():
        x = x_ref[...]
        ms = jnp.mean(x * x, axis=-1, keepdims=True)
        y = x * lax.rsqrt(ms + EPS) * g_ref[...]
        h = (y * (1.0 + sc_ref[0]) + sh_ref[0]).astype(BF16)
        h_scr[...] = h
        alr_ref[...] = _dot(h, walr_ref[...])

    o_ref[...] = _dot(h_scr[...], w_ref[...]).astype(o_ref.dtype)


def _inproj_call(x2, g, sc, sh, w_main, w_alr, S):
    T, D = x2.shape
    NM = w_main.shape[1]
    tm = min(1024, S)
    tn = min(1024, NM)
    per_b = S // tm
    return pl.pallas_call(
        _inproj_kernel,
        grid=(T // tm, NM // tn),
        in_specs=[pl.BlockSpec((tm, D), lambda i, j: (i, 0)),
                  pl.BlockSpec((1, D), lambda i, j: (0, 0)),
                  pl.BlockSpec((1, 1, D), lambda i, j: (i // per_b, 0, 0)),
                  pl.BlockSpec((1, 1, D), lambda i, j: (i // per_b, 0, 0)),
                  pl.BlockSpec((D, tn), lambda i, j: (0, j)),
                  pl.BlockSpec((D, LANES), lambda i, j: (0, 0))],
        out_specs=[pl.BlockSpec((tm, tn), lambda i, j: (i, j)),
                   pl.BlockSpec((tm, LANES), lambda i, j: (i, 0))],
        out_shape=[jax.ShapeDtypeStruct((T, NM), BF16),
                   jax.ShapeDtypeStruct((T, LANES), F32)],
        scratch_shapes=[pltpu.VMEM((tm, D), BF16)],
        compiler_params=_params(("parallel", "arbitrary"), 56),
        name="in_proj",
    )(x2, g, sc, sh, w_main, w_alr)


def _gmlp_kernel(u_ref, v_ref, g_ref, b_ref, ws_ref, bias_ref, o_ref, vn_scr, *, groups, chunk):
    rows, width = v_ref.shape
    gc = width // groups
    r = lax.broadcasted_iota(jnp.int32, (chunk, chunk), 0)
    c = lax.broadcasted_iota(jnp.int32, (chunk, chunk), 1)
    causal = r >= c
    for ci in range(rows // chunk):
        rs = slice(ci * chunk, (ci + 1) * chunk)
        gv = _gelu(v_ref[rs, :].astype(F32))
        mu = jnp.mean(gv, axis=-1, keepdims=True)
        xc = gv - mu
        var = jnp.mean(xc * xc, axis=-1, keepdims=True)
        vn = xc * lax.rsqrt(var + EPS) * g_ref[...] + b_ref[...]
        vn_scr[...] = vn.astype(BF16)
        for gi in range(groups):
            cs = slice(gi * gc, (gi + 1) * gc)
            wm = jnp.where(causal, ws_ref[gi], 0.0).astype(BF16)
            mixed = _dot(wm, vn_scr[:, cs]) + bias_ref[:, cs]
            gu = _gelu(u_ref[rs, cs].astype(F32))
            o_ref[rs, cs] = (gu * mixed).astype(o_ref.dtype)


def _gmlp_call(proj, vn_g, vn_b, ws, bias_full):
    T = proj.shape[0]
    G, C, _ = ws.shape
    W = vn_g.shape[1]
    R = min(2 * C, T)
    kern = functools.partial(_gmlp_kernel, groups=G, chunk=C)
    return pl.pallas_call(
        kern,
        grid=(T // R,),
        in_specs=[pl.BlockSpec((R, W), lambda i: (i, 0)),
                  pl.BlockSpec((R, W), lambda i: (i, 1)),
                  pl.BlockSpec((1, W), lambda i: (0, 0)),
                  pl.BlockSpec((1, W), lambda i: (0, 0)),
                  pl.BlockSpec((G, C, C), lambda i: (0, 0, 0)),
                  pl.BlockSpec((C, W), lambda i: (0, 0))],
        out_specs=pl.BlockSpec((R, W), lambda i: (i, 0)),
        out_shape=jax.ShapeDtypeStruct((T, W), BF16),
        scratch_shapes=[pltpu.VMEM((C, W), BF16)],
        compiler_params=_params(("parallel",), 32),
        name="gmlp_gate",
    )(proj, proj, vn_g, vn_b, ws, bias_full)


def _gla_kernel(q_ref, k_ref, v_ref, r_ref, alr_ref, wa2_ref, ba_ref, g_ref, o_ref,
                st_ref, la_ref, *, heads):
    rows, dk = q_ref.shape
    dv = v_ref.shape[1]
    hk = dk // heads
    hv = dv // heads
    C = GLA_CHUNK
    scale = hk ** -0.5

    @pl.when(pl.program_id(1) == 0)
    def _():
        st_ref[...] = jnp.zeros(st_ref.shape, F32)

    z = _dot(alr_ref[...].astype(BF16), wa2_ref[...]) + ba_ref[...]
    la_ref[...] = (jnp.minimum(z, 0.0) - jnp.log(1.0 + jnp.exp(-jnp.abs(z)))) * (1.0 / GLA_TAU)

    ri = lax.broadcasted_iota(jnp.int32, (C, C), 0)
    ci = lax.broadcasted_iota(jnp.int32, (C, C), 1)
    causal = ri >= ci
    tri = causal.astype(BF16)

    def chunk_step(c, carry):
        rs = pl.ds(pl.multiple_of(c * C, C), C)
        la = la_ref[rs, :]
        hi = la.astype(BF16)
        r1 = la - hi.astype(F32)
        mid = r1.astype(BF16)
        lo = (r1 - mid.astype(F32)).astype(BF16)
        cum = _dot(tri, hi) + _dot(tri, mid) + _dot(tri, lo)
        cl = cum[C - 1:C, :]
        q = q_ref[rs, :].astype(F32) * scale
        k = k_ref[rs, :].astype(F32)
        qd = (q * jnp.exp(cum)).astype(BF16)
        ki = (k * jnp.exp(-cum)).astype(BF16)
        ke = (k * jnp.exp(cl - cum)).astype(BF16)
        dec = jnp.exp(cl)
        for h in range(heads):
            ks = slice(h * hk, (h + 1) * hk)
            vs = slice(h * hv, (h + 1) * hv)
            vh = v_ref[rs, vs]
            s = lax.dot_general(qd[:, ks], ki[:, ks], NT_DIMS, preferred_element_type=F32)
            s = jnp.where(causal, s, 0.0).astype(BF16)
            st = st_ref[h]
            o = _dot(s, vh) + lax.dot_general(qd[:, ks], st.astype(BF16), NT_DIMS,
                                              preferred_element_type=F32)
            st_ref[h] = st * dec[:, ks] + lax.dot_general(vh, ke[:, ks], TN_DIMS,
                                                          preferred_element_type=F32)
            ms = jnp.mean(o * o, axis=-1, keepdims=True)
            on = o * lax.rsqrt(ms + EPS) * g_ref[:, vs]
            rr = r_ref[rs, vs].astype(F32)
            o_ref[rs, vs] = (on * (rr * _sigmoid(rr))).astype(o_ref.dtype)
        return carry

    lax.fori_loop(0, rows // C, chunk_step, 0)


def _gla_call(proj, alr, wa2p, ba, on_g, B, S, W):
    DK = wa2p.shape[1]
    DV = on_g.shape[1]
    H = GLA_HEADS
    Cb = min(256, S)
    nb = S // Cb
    q_blk = (2 * W) // DK
    k_blk = (2 * W + DK) // DK
    v_blk = (2 * W + 2 * DK) // DV
    r_blk = (2 * W + 2 * DK + DV) // DV
    kern = functools.partial(_gla_kernel, heads=H)
    return pl.pallas_call(
        kern,
        grid=(B, nb),
        in_specs=[pl.BlockSpec((Cb, DK), lambda b, i: (b * nb + i, q_blk)),
                  pl.BlockSpec((Cb, DK), lambda b, i: (b * nb + i, k_blk)),
                  pl.BlockSpec((Cb, DV), lambda b, i: (b * nb + i, v_blk)),
                  pl.BlockSpec((Cb, DV), lambda b, i: (b * nb + i, r_blk)),
                  pl.BlockSpec((Cb, LANES), lambda b, i: (b * nb + i, 0)),
                  pl.BlockSpec((LANES, DK), lambda b, i: (0, 0)),
                  pl.BlockSpec((1, DK), lambda b, i: (0, 0)),
                  pl.BlockSpec((1, DV), lambda b, i: (0, 0))],
        out_specs=pl.BlockSpec((Cb, DV), lambda b, i: (b * nb + i, 0)),
        out_shape=jax.ShapeDtypeStruct((B * S, DV), BF16),
        scratch_shapes=[pltpu.VMEM((H, DV // H, DK // H), F32),
                        pltpu.VMEM((Cb, DK), F32)],
        compiler_params=_params(("parallel", "arbitrary"), 32),
        name="gla",
    )(proj, proj, proj, proj, alr, wa2p, ba, on_g)


def _merge_kernel(ya_ref, yb_ref, ga_ref, gb_ref, wpa_ref, wpb_ref, wout_ref, x_ref, g1_ref,
                  n2_ref, sc_ref, sh_ref, wr_ref, x1_ref, h2_ref, lg_ref):
    a = _dot(ya_ref[...], wpa_ref[...])
    b = _dot(yb_ref[...], wpb_ref[...])
    y = _sigmoid(ga_ref[...].astype(F32)) * a + _sigmoid(gb_ref[...].astype(F32)) * b
    out = _dot(y.astype(BF16), wout_ref[...])
    x1 = x_ref[...] + g1_ref[0] * out
    x1_ref[...] = x1
    ms = jnp.mean(x1 * x1, axis=-1, keepdims=True)
    h2 = (x1 * lax.rsqrt(ms + EPS) * n2_ref[...]) * (1.0 + sc_ref[0]) + sh_ref[0]
    _store_packed(h2_ref, h2)
    lg_ref[...] = _dot(h2.astype(BF16), wr_ref[...])


def _merge_call(ya, yb, proj, wpa, wpb, wout, x2, g1, n2g, sc2, sh2, wr, S, W):
    T, D = x2.shape
    ga_blk = (proj.shape[1] - 2 * D) // D
    gb_blk = ga_blk + 1
    tm = min(256, S)
    per_b = S // tm
    n_s = D // (2 * LANES)
    const = dict(pipeline_mode=pl.Buffered(1))
    return pl.pallas_call(
        _merge_kernel,
        grid=(T // tm,),
        in_specs=[pl.BlockSpec((tm, W), lambda i: (i, 0)),
                  pl.BlockSpec((tm, W), lambda i: (i, 0)),
                  pl.BlockSpec((tm, D), lambda i: (i, ga_blk)),
                  pl.BlockSpec((tm, D), lambda i: (i, gb_blk)),
                  pl.BlockSpec((W, D), lambda i: (0, 0), **const),
                  pl.BlockSpec((W, D), lambda i: (0, 0), **const),
                  pl.BlockSpec((D, D), lambda i: (0, 0), **const),
                  pl.BlockSpec((tm, D), lambda i: (i, 0)),
                  pl.BlockSpec((1, 1, D), lambda i: (i // per_b, 0, 0)),
                  pl.BlockSpec((1, D), lambda i: (0, 0)),
                  pl.BlockSpec((1, 1, D), lambda i: (i // per_b, 0, 0)),
                  pl.BlockSpec((1, 1, D), lambda i: (i // per_b, 0, 0)),
                  pl.BlockSpec((D, LANES), lambda i: (0, 0), **const)],
        out_specs=[pl.BlockSpec((tm, D), lambda i: (i, 0)),
                   pl.BlockSpec((tm * n_s, LANES), lambda i: (i, 0)),
                   pl.BlockSpec((tm, LANES), lambda i: (i, 0))],
        out_shape=[jax.ShapeDtypeStruct((T, D), F32),
                   jax.ShapeDtypeStruct((T * n_s, LANES), jnp.uint32),
                   jax.ShapeDtypeStruct((T, LANES), F32)],
        compiler_params=_params(("parallel",), 56),
        name="merge_out",
    )(ya, yb, proj, proj, wpa, wpb, wout, x2, g1, n2g, sc2, sh2, wr)


def _router_kernel(lg_ref, b_ref, e_ref, w_ref, *, groups, per_group):
    lg = lg_ref[...] + b_ref[...]
    lane = lax.broadcasted_iota(jnp.int32, lg.shape, 1)
    lane_f = lane.astype(F32)
    neg = jnp.float32(-1e30)
    big = jnp.float32(LANES)

    def first_argmax(vals):
        m = jnp.max(vals, axis=-1, keepdims=True)
        idx = jnp.min(jnp.where(vals == m, lane_f, big), axis=-1, keepdims=True)
        return m, idx

    gmask = lane < groups
    gl = jnp.where(gmask, lg, neg)
    gmax, gidx = first_argmax(gl)
    gsum = jnp.sum(jnp.where(gmask, jnp.exp(gl - gmax), 0.0), axis=-1, keepdims=True)
    gtop = 1.0 / gsum

    lo = groups + gidx * per_group
    emask = (lane_f >= lo) & (lane_f < lo + per_group)
    el = jnp.where(emask, lg, neg)
    m1, i1 = first_argmax(el)
    el2 = jnp.where(lane_f == i1, neg, el)
    m2, i2 = first_argmax(el2)
    den = jnp.sum(jnp.where(emask, jnp.exp(el - m1), 0.0), axis=-1, keepdims=True)
    p1 = 1.0 / den
    p2 = jnp.exp(m2 - m1) / den
    ps = p1 + p2
    w1 = gtop * p1 / ps
    w2 = gtop * p2 / ps
    e1 = (i1 - groups).astype(jnp.int32)
    e2 = (i2 - groups).astype(jnp.int32)
    e_ref[...] = jnp.where(lane == 0, e1, jnp.where(lane == 1, e2, 0))
    w_ref[...] = jnp.where(lane == 0, w1, jnp.where(lane == 1, w2, 0.0))


def _router_call(logits, bias, per_group):
    T = logits.shape[0]
    tm = min(1024, T)
    kern = functools.partial(_router_kernel, groups=MOE_GROUPS, per_group=per_group)
    return pl.pallas_call(
        kern,
        grid=(T // tm,),
        in_specs=[pl.BlockSpec((tm, LANES), lambda i: (i, 0)),
                  pl.BlockSpec((1, LANES), lambda i: (0, 0))],
        out_specs=[pl.BlockSpec((tm, LANES), lambda i: (i, 0)),
                   pl.BlockSpec((tm, LANES), lambda i: (i, 0))],
        out_shape=[jax.ShapeDtypeStruct((T, LANES), jnp.int32),
                   jax.ShapeDtypeStruct((T, LANES), F32)],
        compiler_params=_params(("parallel",), 16),
        name="router_topk",
    )(logits, bias)


def _row(ref, t, n_s):
    return ref.at[pl.ds(pl.multiple_of(t * n_s, n_s), n_s)]


def _gather_kernel(idx_ref, src_ref, out_ref, sem, *, n_s):
    G = idx_ref.shape[2]

    def issue(r, c):
        pltpu.make_async_copy(_row(src_ref, idx_ref[0, 0, r], n_s), _row(out_ref, r, n_s),
                              sem.at[0]).start()
        return c
    lax.fori_loop(0, G, issue, 0, unroll=8)
    pltpu.make_async_copy(src_ref.at[pl.ds(0, G * n_s)], out_ref, sem.at[0]).wait()


def _gather_call(src, idx, n_s):
    M = idx.shape[0]
    G = GATHER_ROWS
    nb = M // G
    kern = functools.partial(_gather_kernel, n_s=n_s)
    return pl.pallas_call(
        kern,
        grid=(nb,),
        in_specs=[pl.BlockSpec((1, 1, G), lambda i: (i, 0, 0), memory_space=pltpu.SMEM),
                  pl.BlockSpec(memory_space=pl.ANY)],
        out_specs=pl.BlockSpec((G * n_s, LANES), lambda i: (i, 0)),
        out_shape=jax.ShapeDtypeStruct((M * n_s, LANES), src.dtype),
        scratch_shapes=[pltpu.SemaphoreType.DMA((1,))],
        compiler_params=_params(("arbitrary",), 16),
        name="row_gather",
    )(idx.reshape(nb, 1, G), src)


def _dispatch_kernel(zr_ref, na_ref, dest_ref, h_ref, out_ref, zbuf, sem, zsem, *, n_s, block):
    n_assign = dest_ref.shape[2]
    tm = n_assign // TOP_K
    nb = out_ref.shape[0] // (block * n_s)

    def zero_block(b):
        start = pl.multiple_of(b * (block * n_s), block * n_s)
        return pltpu.make_async_copy(zbuf, out_ref.at[pl.ds(start, block * n_s)], zsem.at[0])

    @pl.when(pl.program_id(0) == 0)
    def _():
        zbuf[...] = jnp.zeros(zbuf.shape, zbuf.dtype)
        n_exp = zr_ref.shape[0]
        lax.fori_loop(0, n_exp, lambda e, c: (zero_block(zr_ref[e]).start(), c)[1], 0)
        lax.fori_loop(na_ref[0], nb, lambda b, c: (zero_block(b).start(), c)[1], 0)
        lax.fori_loop(0, n_exp, lambda e, c: (zero_block(0).wait(), c)[1], 0)
        lax.fori_loop(na_ref[0], nb, lambda b, c: (zero_block(0).wait(), c)[1], 0)

    def issue(a, c):
        pltpu.make_async_copy(_row(h_ref, a // TOP_K, n_s), _row(out_ref, dest_ref[0, 0, a], n_s),
                              sem.at[0]).start()
        return c
    lax.fori_loop(0, n_assign, issue, 0, unroll=8)
    for _ in range(TOP_K):
        pltpu.make_async_copy(h_ref, out_ref.at[pl.ds(0, tm * n_s)], sem.at[0]).wait()


def _dispatch_call(h2p, dest, zero_rows, n_act, n_rows, n_s):
    T = h2p.shape[0] // n_s
    tm = min(1024, T)
    nt = T // tm
    kern = functools.partial(_dispatch_kernel, n_s=n_s, block=MOE_BLOCK)
    grid_spec = pltpu.PrefetchScalarGridSpec(
        num_scalar_prefetch=2,
        grid=(nt,),
        in_specs=[pl.BlockSpec((1, 1, tm * TOP_K), lambda i, zr, na: (i, 0, 0),
                               memory_space=pltpu.SMEM),
                  pl.BlockSpec((tm * n_s, LANES), lambda i, zr, na: (i, 0))],
        out_specs=pl.BlockSpec(memory_space=pl.ANY),
        scratch_shapes=[pltpu.VMEM((MOE_BLOCK * n_s, LANES), h2p.dtype),
                        pltpu.SemaphoreType.DMA((1,)),
                        pltpu.SemaphoreType.DMA((1,))],
    )
    return pl.pallas_call(
        kern,
        grid_spec=grid_spec,
        out_shape=jax.ShapeDtypeStruct((n_rows * n_s, LANES), h2p.dtype),
        compiler_params=_params(("arbitrary",), 16),
        name="row_dispatch",
    )(zero_rows, n_act, dest.reshape(nt, 1, tm * TOP_K), h2p)


def _stage_expert_weights(layer, be_ref, na_ref, nx_ref, i, hbm_refs, stage, bf_refs, sem):
    e = be_ref[i]
    first = (i == 0) | (e != be_ref[jnp.maximum(i - 1, 0)])

    def copies(expert):
        return [pltpu.make_async_copy(w.at[layer, expert], stage.at[k], sem.at[k])
                for k, w in enumerate(hbm_refs)]

    @pl.when(i == 0)
    def _():
        for cp in copies(e):
            cp.start()

    @pl.when((i < na_ref[0]) & first)
    def _():
        for k, cp in enumerate(copies(e)):
            cp.wait()
            bf_refs[k][...] = stage[k].astype(BF16)
        nxt = nx_ref[i]

        @pl.when(nxt >= 0)
        def _():
            for cp in copies(nxt):
                cp.start()


def _expert_up_kernel(be_ref, na_ref, nx_ref, xs_ref, wg_hbm, wu_hbm, hid_ref, stage, wgb, wub, sem,
                      *, layer):
    i = pl.program_id(0)
    active = i < na_ref[0]
    _stage_expert_weights(layer, be_ref, na_ref, nx_ref, i, (wg_hbm, wu_hbm), stage, (wgb, wub), sem)

    @pl.when(active)
    def _():
        rows = hid_ref.shape[0]
        los, his = _load_packed(xs_ref, rows, xs_ref.shape[0] // rows)
        x = jnp.concatenate([p.astype(BF16) for p in los + his], axis=1)
        a = _dot(x, wgb[...])
        b = _dot(x, wub[...])
        hid_ref[...] = (a * _sigmoid(a) * b).astype(hid_ref.dtype)

    @pl.when(jnp.logical_not(active))
    def _():
        hid_ref[...] = jnp.zeros(hid_ref.shape, hid_ref.dtype)


def _expert_down_kernel(be_ref, na_ref, nx_ref, hid_ref, wd_hbm, ys_ref, stage, wdb, sem, *, layer):
    i = pl.program_id(0)
    active = i < na_ref[0]
    _stage_expert_weights(layer, be_ref, na_ref, nx_ref, i, (wd_hbm,), stage, (wdb,), sem)

    @pl.when(active)
    def _():
        _store_packed(ys_ref, _dot(hid_ref[...], wdb[...]))

    @pl.when(jnp.logical_not(active))
    def _():
        ys_ref[...] = jnp.zeros(ys_ref.shape, ys_ref.dtype)


def _experts_call(xs, block_e, n_act, next_e, w_gate, w_up, w_down, l):
    D, DE = w_gate.shape[-2], w_gate.shape[-1]
    n_s = D // (2 * LANES)
    n_rows = xs.shape[0] // n_s
    BM = MOE_BLOCK
    nb = n_rows // BM

    def blk(i, na):
        return jnp.minimum(i, na[0] - 1)

    up_spec = pltpu.PrefetchScalarGridSpec(
        num_scalar_prefetch=3,
        grid=(nb,),
        in_specs=[pl.BlockSpec((BM * n_s, LANES), lambda i, be, na, nx: (blk(i, na), 0)),
                  pl.BlockSpec(memory_space=pl.ANY),
                  pl.BlockSpec(memory_space=pl.ANY)],
        out_specs=pl.BlockSpec((BM, DE), lambda i, be, na, nx: (i, 0)),
        scratch_shapes=[pltpu.VMEM((2, D, DE), F32),
                        pltpu.VMEM((D, DE), BF16), pltpu.VMEM((D, DE), BF16),
                        pltpu.SemaphoreType.DMA((2,))],
    )
    hid = pl.pallas_call(
        functools.partial(_expert_up_kernel, layer=l),
        grid_spec=up_spec,
        out_shape=jax.ShapeDtypeStruct((n_rows, DE), BF16),
        compiler_params=_params(("arbitrary",), 48),
        name="expert_up",
    )(block_e, n_act, next_e, xs, w_gate, w_up)

    down_spec = pltpu.PrefetchScalarGridSpec(
        num_scalar_prefetch=3,
        grid=(nb,),
        in_specs=[pl.BlockSpec((BM, DE), lambda i, be, na, nx: (blk(i, na), 0)),
                  pl.BlockSpec(memory_space=pl.ANY)],
        out_specs=pl.BlockSpec((BM * n_s, LANES), lambda i, be, na, nx: (i, 0)),
        scratch_shapes=[pltpu.VMEM((1, DE, D), F32),
                        pltpu.VMEM((DE, D), BF16),
                        pltpu.SemaphoreType.DMA((1,))],
    )
    return pl.pallas_call(
        functools.partial(_expert_down_kernel, layer=l),
        grid_spec=down_spec,
        out_shape=jax.ShapeDtypeStruct((n_rows * n_s, LANES), jnp.uint32),
        compiler_params=_params(("arbitrary",), 32),
        name="expert_down",
    )(block_e, n_act, next_e, hid, w_down)


def _combine_kernel(x_ref, yg_ref, w_ref, g2_ref, fg_ref, o_ref, *, final_norm):
    tm, d = x_ref.shape
    half = d // 2
    n_s = half // LANES
    w = w_ref[...]
    w0 = w[:, 0:1]
    w1 = w[:, 1:2]
    lo0, hi0 = _load_packed(yg_ref, tm, n_s, offset=0, group=TOP_K)
    lo1, hi1 = _load_packed(yg_ref, tm, n_s, offset=1, group=TOP_K)
    ss = jnp.zeros((tm, 1), F32)
    for s in range(n_s):
        for base, p0, p1 in ((0, lo0[s], lo1[s]), (half, hi0[s], hi1[s])):
            cs = slice(base + s * LANES, base + (s + 1) * LANES)
            xv = x_ref[:, cs] + g2_ref[0, :, cs] * (w0 * p0 + w1 * p1)
            o_ref[:, cs] = xv
            if final_norm:
                ss = ss + jnp.sum(xv * xv, axis=-1, keepdims=True)
    if final_norm:
        o_ref[...] = o_ref[...] * lax.rsqrt(ss * (1.0 / d) + EPS) * fg_ref[...]


def _combine_call(x1, yg, wts, g2, final_g, S, final_norm):
    T, D = x1.shape
    tm = min(512, S)
    per_b = S // tm
    n_s = D // (2 * LANES)
    kern = functools.partial(_combine_kernel, final_norm=final_norm)
    return pl.pallas_call(
        kern,
        grid=(T // tm,),
        in_specs=[pl.BlockSpec((tm, D), lambda i: (i, 0)),
                  pl.BlockSpec((tm * TOP_K * n_s, LANES), lambda i: (i, 0)),
                  pl.BlockSpec((tm, LANES), lambda i: (i, 0)),
                  pl.BlockSpec((1, 1, D), lambda i: (i // per_b, 0, 0)),
                  pl.BlockSpec((1, D), lambda i: (0, 0))],
        out_specs=pl.BlockSpec((tm, D), lambda i: (i, 0)),
        out_shape=jax.ShapeDtypeStruct((T, D), F32),
        compiler_params=_params(("parallel",), 40),
        name="moe_combine",
    )(x1, yg, wts, g2, final_g)


def _rank_kernel(e_ref, rank_ref, cnt_ref, carry):
    @pl.when(pl.program_id(0) == 0)
    def _():
        carry[...] = jnp.zeros(carry.shape, F32)

    e = e_ref[...]
    tb = e.shape[0]
    lane = lax.broadcasted_iota(jnp.int32, e.shape, 1)
    oh0 = lane == e[:, 0:1]
    oh1 = lane == e[:, 1:2]
    both = oh0.astype(F32) + oh1.astype(F32)
    r = lax.broadcasted_iota(jnp.int32, (tb, tb), 0)
    c = lax.broadcasted_iota(jnp.int32, (tb, tb), 1)
    strict = (r > c).astype(BF16)
    before = _dot(strict, both.astype(BF16)) + carry[0:1, :]
    r0 = jnp.sum(jnp.where(oh0, before, 0.0), axis=-1, keepdims=True)
    r1 = jnp.sum(jnp.where(oh1, before, 0.0), axis=-1, keepdims=True)
    rank_ref[...] = jnp.where(lane == 0, r0, jnp.where(lane == 1, r1, 0.0)).astype(jnp.int32)
    carry[...] = carry[...] + jnp.sum(both, axis=0, keepdims=True)
    cnt_ref[...] = carry[...]


def _rank_call(eid):
    T = eid.shape[0]
    tb = min(512, T)
    return pl.pallas_call(
        _rank_kernel,
        grid=(T // tb,),
        in_specs=[pl.BlockSpec((tb, LANES), lambda i: (i, 0))],
        out_specs=[pl.BlockSpec((tb, LANES), lambda i: (i, 0)),
                   pl.BlockSpec((8, LANES), lambda i: (0, 0))],
        out_shape=[jax.ShapeDtypeStruct((T, LANES), jnp.int32),
                   jax.ShapeDtypeStruct((8, LANES), F32)],
        scratch_shapes=[pltpu.VMEM((8, LANES), F32)],
        compiler_params=_params(("arbitrary",), 16),
        name="assign_rank",
    )(eid)


def _dispatch_plan(eid_full, n_experts):
    T = eid_full.shape[0]
    A = T * TOP_K
    BM = MOE_BLOCK
    rank_full, cnt = _rank_call(eid_full)
    eid = eid_full[:, :TOP_K]
    counts = cnt[0, :n_experts].astype(jnp.int32)
    padded = (counts + BM - 1) // BM * BM
    pad_end = jnp.cumsum(padded)
    pad_start = pad_end - padded
    onehot = eid[:, :, None] == jnp.arange(n_experts, dtype=jnp.int32)[None, None, :]
    dest = (jnp.sum(jnp.where(onehot, pad_start[None, None, :], 0), axis=-1)
            + rank_full[:, :TOP_K]).reshape(A)
    nb = (A + n_experts * (BM - 1) + BM - 1) // BM
    block_start = jnp.arange(nb, dtype=jnp.int32) * BM
    block_e = jnp.minimum(jnp.sum(block_start[:, None] >= pad_end[None, :], axis=1),
                          n_experts - 1).astype(jnp.int32)
    n_act = (pad_end[-1] // BM).astype(jnp.int32).reshape(1)
    last_block = jnp.maximum(pad_end // BM - 1, 0).astype(jnp.int32)
    ex = jnp.arange(n_experts, dtype=jnp.int32)
    later = (ex[None, :] > ex[:, None]) & (counts[None, :] > 0)
    next_of = jnp.min(jnp.where(later, ex[None, :], n_experts), axis=1)
    next_of = jnp.where(next_of < n_experts, next_of, -1)
    next_e = jnp.sum(jnp.where(block_e[:, None] == ex[None, :], next_of[None, :], 0),
                     axis=1).astype(jnp.int32)
    return dest.astype(jnp.int32), block_e, n_act, next_e, last_block, nb * BM


def kernel(x, c, ada_w, ada_b, norm1_g, w_in, gm_vn_g, gm_vn_b, gm_ws, gm_bs, gla_wa2, gla_ba,
           gla_on_g, w_pa, w_pb, w_out, norm2_g, w_rg, b_rg, w_re, b_re, w_e_gate, w_e_up,
           w_e_down, final_g):
    B, S, D = x.shape
    T = B * S
    L = ada_w.shape[0]
    W = gm_vn_g.shape[1]
    G, C = gm_ws.shape[1], gm_ws.shape[2]
    RANK, DK = gla_wa2.shape[1], gla_wa2.shape[2]
    DV = gla_on_g.shape[1]
    E = w_e_gate.shape[1]
    per_group = E // MOE_GROUPS
    SUB = D // (2 * LANES)
    assert B <= 8 and MOE_GROUPS + E <= LANES and RANK <= LANES

    c8 = jnp.zeros((8, D), F32).at[:B].set(c)
    mod = _ada_call(c8, ada_w, ada_b.reshape(L, 1, 6 * D))

    o_alr = 2 * W + 2 * DK + 2 * DV
    x2 = x.reshape(T, D)
    for l in range(L):
        sh1, sc1, g1, sh2, sc2, g2 = [mod[l, :B, k * D:(k + 1) * D].reshape(B, 1, D) for k in range(6)]
        wl = w_in[l]
        w_main = jnp.concatenate([wl[:, :o_alr], wl[:, o_alr + RANK:]], axis=1).astype(BF16)
        w_alr = jnp.zeros((D, LANES), BF16).at[:, :RANK].set(wl[:, o_alr:o_alr + RANK].astype(BF16))
        wa2p = jnp.zeros((LANES, DK), BF16).at[:RANK].set(gla_wa2[l].astype(BF16))
        bias_full = jnp.repeat(gm_bs[l].T, W // G, axis=1)
        wr = jnp.zeros((D, LANES), BF16).at[:, :MOE_GROUPS + E].set(
            jnp.concatenate([w_rg[l], w_re[l]], axis=1).astype(BF16))
        br = jnp.zeros((1, LANES), F32).at[0, :MOE_GROUPS + E].set(
            jnp.concatenate([b_rg[l], b_re[l]]))

        proj, alr = _inproj_call(x2, norm1_g[l].reshape(1, D), sc1, sh1, w_main, w_alr, S)
        ya = _gmlp_call(proj, gm_vn_g[l].reshape(1, W), gm_vn_b[l].reshape(1, W), gm_ws[l], bias_full)
        yb = _gla_call(proj, alr, wa2p, gla_ba[l].reshape(1, DK), gla_on_g[l].reshape(1, DV), B, S, W)
        x1, h2, logits = _merge_call(ya, yb, proj, w_pa[l].astype(BF16), w_pb[l].astype(BF16),
                                     w_out[l].astype(BF16), x2, g1, norm2_g[l].reshape(1, D),
                                     sc2, sh2, wr, S, W)
        eid, wts = _router_call(logits, br, per_group)
        dest, block_e, n_act, next_e, last_block, n_rows = _dispatch_plan(eid, E)
        xs = _dispatch_call(h2, dest, last_block, n_act, n_rows, SUB)
        ys = _experts_call(xs, block_e, n_act, next_e, w_e_gate, w_e_up, w_e_down, l)
        yg = _gather_call(ys, dest, SUB)
        x2 = _combine_call(x1, yg, wts, g2, final_g.reshape(1, D), S, final_norm=(l == L - 1))
    return x2.reshape(B, S, D)
```

```python
import functools

import jax
import jax.numpy as jnp
from jax import lax
from jax.experimental import pallas as pl
from jax.experimental.pallas import tpu as pltpu

F32 = jnp.float32
BF16 = jnp.bfloat16

EPS = 1e-6
GLA_HEADS = 4
GLA_TAU = 16.0
GLA_CHUNK = 64
MOE_GROUPS = 4
TOP_K = 2

LANES = 128
MIB = 1024 * 1024
MOE_BLOCK = 256
ISSUE_UNROLL = 8

NT_DIMS = (((1,), (1,)), ((), ()))
TN_DIMS = (((0,), (0,)), ((), ()))


def _params(semantics, vmem_mib):
    return pltpu.CompilerParams(dimension_semantics=semantics,
                                vmem_limit_bytes=int(vmem_mib * MIB))


def _dot(a, b):
    return jnp.dot(a, b, preferred_element_type=F32)


def _sigmoid(x):
    return 1.0 / (1.0 + jnp.exp(-x))


def _gelu(x):
    return 0.5 * x * (1.0 + lax.erf(x * (2.0 ** -0.5)))


def _pack_pairs(lo, hi):
    lo_w = pltpu.bitcast(lo.astype(BF16).astype(F32), jnp.uint32) >> 16
    hi_w = pltpu.bitcast(hi.astype(BF16).astype(F32), jnp.uint32) & jnp.uint32(0xFFFF0000)
    return lo_w | hi_w


def _unpack_pairs(w):
    lo = pltpu.bitcast(w << 16, F32)
    hi = pltpu.bitcast(w & jnp.uint32(0xFFFF0000), F32)
    return lo, hi


def _store_packed(ref, x):
    rows, d = x.shape
    half = d // 2
    n_s = half // LANES
    for s in range(n_s):
        lo = x[:, s * LANES:(s + 1) * LANES]
        hi = x[:, half + s * LANES:half + (s + 1) * LANES]
        ref[pl.ds(s, rows, stride=n_s), :] = _pack_pairs(lo, hi)


def _load_packed(ref, rows, n_s, offset=0, group=1):
    los, his = [], []
    for s in range(n_s):
        lo, hi = _unpack_pairs(ref[pl.ds(offset * n_s + s, rows, stride=n_s * group), :])
        los.append(lo)
        his.append(hi)
    return los, his


def _ada_kernel(c_ref, w_ref, b_ref, o_ref):
    c = c_ref[...]
    cond = c * _sigmoid(c)
    o_ref[0] = _dot(cond.astype(BF16), w_ref[0].astype(BF16)) + b_ref[0]


def _ada_call(c8, ada_w, ada_b3):
    L, D, N = ada_w.shape
    tn = min(1024, N)
    return pl.pallas_call(
        _ada_kernel,
        grid=(L, N // tn),
        in_specs=[pl.BlockSpec((8, D), lambda l, j: (0, 0)),
                  pl.BlockSpec((1, D, tn), lambda l, j: (l, 0, j)),
                  pl.BlockSpec((1, 1, tn), lambda l, j: (l, 0, j))],
        out_specs=pl.BlockSpec((1, 8, tn), lambda l, j: (l, 0, j)),
        out_shape=jax.ShapeDtypeStruct((L, 8, N), F32),
        compiler_params=_params(("parallel", "parallel"), 40),
        name="ada_mod",
    )(c8, ada_w, ada_b3)


def _inproj_kernel(x_ref, g_ref, sc_ref, sh_ref, w_ref, walr_ref, o_ref, alr_ref, h_scr):
    @pl.when(pl.program_id(1) == 0)
    def _():
        x = x_ref[...]
        ms = jnp.mean(x * x, axis=-1, keepdims=True)
        y = x * lax.rsqrt(ms + EPS) * g_ref[...]
        h = (y * (1.0 + sc_ref[0]) + sh_ref[0]).astype(BF16)
        h_scr[...] = h
        alr_ref[...] = _dot(h, walr_ref[...])

    o_ref[...] = _dot(h_scr[...], w_ref[...]).astype(o_ref.dtype)


def _inproj_call(x2, g, sc, sh, w_main, w_alr, S):
    T, D = x2.shape
    NM = w_main.shape[1]
    tm = min(1024, S)
    tn = min(1024, NM)
    per_b = S // tm
    return pl.pallas_call(
        _inproj_kernel,
        grid=(T // tm, NM // tn),
        in_specs=[pl.BlockSpec((tm, D), lambda i, j: (i, 0)),
                  pl.BlockSpec((1, D), lambda i, j: (0, 0)),
                  pl.BlockSpec((1, 1, D), lambda i, j: (i // per_b, 0, 0)),
                  pl.BlockSpec((1, 1, D), lambda i, j: (i // per_b, 0, 0)),
                  pl.BlockSpec((D, tn), lambda i, j: (0, j)),
                  pl.BlockSpec((D, LANES), lambda i, j: (0, 0))],
        out_specs=[pl.BlockSpec((tm, tn), lambda i, j: (i, j)),
                   pl.BlockSpec((tm, LANES), lambda i, j: (i, 0))],
        out_shape=[jax.ShapeDtypeStruct((T, NM), BF16),
                   jax.ShapeDtypeStruct((T, LANES), F32)],
        scratch_shapes=[pltpu.VMEM((tm, D), BF16)],
        compiler_params=_params(("parallel", "arbitrary"), 56),
        name="in_proj",
    )(x2, g, sc, sh, w_main, w_alr)


def _gmlp_kernel(u_ref, v_ref, g_ref, b_ref, ws_ref, bias_ref, o_ref, vn_scr, *, groups, chunk):
    rows, width = v_ref.shape
    gc = width // groups
    r = lax.broadcasted_iota(jnp.int32, (chunk, chunk), 0)
    c = lax.broadcasted_iota(jnp.int32, (chunk, chunk), 1)
    causal = r >= c
    for ci in range(rows // chunk):
        rs = slice(ci * chunk, (ci + 1) * chunk)
        gv = _gelu(v_ref[rs, :].astype(F32))
        mu = jnp.mean(gv, axis=-1, keepdims=True)
        xc = gv - mu
        var = jnp.mean(xc * xc, axis=-1, keepdims=True)
        vn = xc * lax.rsqrt(var + EPS) * g_ref[...] + b_ref[...]
        vn_scr[...] = vn.astype(BF16)
        for gi in range(groups):
            cs = slice(gi * gc, (gi + 1) * gc)
            wm = jnp.where(causal, ws_ref[gi], 0.0).astype(BF16)
            mixed = _dot(wm, vn_scr[:, cs]) + bias_ref[:, cs]
            gu = _gelu(u_ref[rs, cs].astype(F32))
            o_ref[rs, cs] = (gu * mixed).astype(o_ref.dtype)


def _gmlp_call(proj, vn_g, vn_b, ws, bias_full):
    T = proj.shape[0]
    G, C, _ = ws.shape
    W = vn_g.shape[1]
    R = min(2 * C, T)
    kern = functools.partial(_gmlp_kernel, groups=G, chunk=C)
    return pl.pallas_call(
        kern,
        grid=(T // R,),
        in_specs=[pl.BlockSpec((R, W), lambda i: (i, 0)),
                  pl.BlockSpec((R, W), lambda i: (i, 1)),
                  pl.BlockSpec((1, W), lambda i: (0, 0)),
                  pl.BlockSpec((1, W), lambda i: (0, 0)),
                  pl.BlockSpec((G, C, C), lambda i: (0, 0, 0)),
                  pl.BlockSpec((C, W), lambda i: (0, 0))],
        out_specs=pl.BlockSpec((R, W), lambda i: (i, 0)),
        out_shape=jax.ShapeDtypeStruct((T, W), BF16),
        scratch_shapes=[pltpu.VMEM((C, W), BF16)],
        compiler_params=_params(("parallel",), 32),
        name="gmlp_gate",
    )(proj, proj, vn_g, vn_b, ws, bias_full)


def _gla_kernel(q_ref, k_ref, v_ref, r_ref, alr_ref, wa2_ref, ba_ref, g_ref, o_ref,
                st_ref, la_ref, *, heads):
    rows, dk = q_ref.shape
    dv = v_ref.shape[1]
    hk = dk // heads
    hv = dv // heads
    C = GLA_CHUNK
    scale = hk ** -0.5

    @pl.when(pl.program_id(1) == 0)
    def _():
        st_ref[...] = jnp.zeros(st_ref.shape, F32)

    z = _dot(alr_ref[...].astype(BF16), wa2_ref[...]) + ba_ref[...]
    la_ref[...] = (jnp.minimum(z, 0.0) - jnp.log(1.0 + jnp.exp(-jnp.abs(z)))) * (1.0 / GLA_TAU)

    ri = lax.broadcasted_iota(jnp.int32, (C, C), 0)
    ci = lax.broadcasted_iota(jnp.int32, (C, C), 1)
    causal = ri >= ci
    tri = causal.astype(BF16)

    def chunk_step(c, carry):
        rs = pl.ds(pl.multiple_of(c * C, C), C)
        la = la_ref[rs, :]
        hi = la.astype(BF16)
        r1 = la - hi.astype(F32)
        mid = r1.astype(BF16)
        lo = (r1 - mid.astype(F32)).astype(BF16)
        cum = _dot(tri, hi) + _dot(tri, mid) + _dot(tri, lo)
        cl = cum[C - 1:C, :]
        q = q_ref[rs, :].astype(F32) * scale
        k = k_ref[rs, :].astype(F32)
        qd = (q * jnp.exp(cum)).astype(BF16)
        ki = (k * jnp.exp(-cum)).astype(BF16)
        ke = (k * jnp.exp(cl - cum)).astype(BF16)
        dec = jnp.exp(cl)
        for h in range(heads):
            ks = slice(h * hk, (h + 1) * hk)
            vs = slice(h * hv, (h + 1) * hv)
            vh = v_ref[rs, vs]
            s = lax.dot_general(qd[:, ks], ki[:, ks], NT_DIMS, preferred_element_type=F32)
            s = jnp.where(causal, s, 0.0).astype(BF16)
            st = st_ref[h]
            o = _dot(s, vh) + lax.dot_general(qd[:, ks], st.astype(BF16), NT_DIMS,
                                              preferred_element_type=F32)
            st_ref[h] = st * dec[:, ks] + lax.dot_general(vh, ke[:, ks], TN_DIMS,
                                                          preferred_element_type=F32)
            ms = jnp.mean(o * o, axis=-1, keepdims=True)
            on = o * lax.rsqrt(ms + EPS) * g_ref[:, vs]
            rr = r_ref[rs, vs].astype(F32)
            o_ref[rs, vs] = (on * (rr * _sigmoid(rr))).astype(o_ref.dtype)
        return carry

    lax.fori_loop(0, rows // C, chunk_step, 0)


def _gla_call(proj, alr, wa2p, ba, on_g, B, S, W):
    DK = wa2p.shape[1]
    DV = on_g.shape[1]
    H = GLA_HEADS
    Cb = min(256, S)
    nb = S // Cb
    q_blk = (2 * W) // DK
    k_blk = (2 * W + DK) // DK
    v_blk = (2 * W + 2 * DK) // DV
    r_blk = (2 * W + 2 * DK + DV) // DV
    kern = functools.partial(_gla_kernel, heads=H)
    return pl.pallas_call(
        kern,
        grid=(B, nb),
        in_specs=[pl.BlockSpec((Cb, DK), lambda b, i: (b * nb + i, q_blk)),
                  pl.BlockSpec((Cb, DK), lambda b, i: (b * nb + i, k_blk)),
                  pl.BlockSpec((Cb, DV), lambda b, i: (b * nb + i, v_blk)),
                  pl.BlockSpec((Cb, DV), lambda b, i: (b * nb + i, r_blk)),
                  pl.BlockSpec((Cb, LANES), lambda b, i: (b * nb + i, 0)),
                  pl.BlockSpec((LANES, DK), lambda b, i: (0, 0)),
                  pl.BlockSpec((1, DK), lambda b, i: (0, 0)),
                  pl.BlockSpec((1, DV), lambda b, i: (0, 0))],
        out_specs=pl.BlockSpec((Cb, DV), lambda b, i: (b * nb + i, 0)),
        out_shape=jax.ShapeDtypeStruct((B * S, DV), BF16),
        scratch_shapes=[pltpu.VMEM((H, DV // H, DK // H), F32),
                        pltpu.VMEM((Cb, DK), F32)],
        compiler_params=_params(("parallel", "arbitrary"), 32),
        name="gla",
    )(proj, proj, proj, proj, alr, wa2p, ba, on_g)


def _merge_kernel(ya_ref, yb_ref, ga_ref, gb_ref, wpa_ref, wpb_ref, wout_ref, x_ref, g1_ref,
                  n2_ref, sc_ref, sh_ref, wr_ref, x1_ref, h2_ref, lg_ref):
    a = _dot(ya_ref[...], wpa_ref[...])
    b = _dot(yb_ref[...], wpb_ref[...])
    y = _sigmoid(ga_ref[...].astype(F32)) * a + _sigmoid(gb_ref[...].astype(F32)) * b
    out = _dot(y.astype(BF16), wout_ref[...])
    x1 = x_ref[...] + g1_ref[0] * out
    x1_ref[...] = x1
    ms = jnp.mean(x1 * x1, axis=-1, keepdims=True)
    h2 = (x1 * lax.rsqrt(ms + EPS) * n2_ref[...]) * (1.0 + sc_ref[0]) + sh_ref[0]
    _store_packed(h2_ref, h2)
    lg_ref[...] = _dot(h2.astype(BF16), wr_ref[...])


def _merge_call(ya, yb, proj, wpa, wpb, wout, x2, g1, n2g, sc2, sh2, wr, S, W):
    T, D = x2.shape
    ga_blk = (proj.shape[1] - 2 * D) // D
    gb_blk = ga_blk + 1
    tm = min(256, S)
    per_b = S // tm
    n_s = D // (2 * LANES)
    const = dict(pipeline_mode=pl.Buffered(1))
    return pl.pallas_call(
        _merge_kernel,
        grid=(T // tm,),
        in_specs=[pl.BlockSpec((tm, W), lambda i: (i, 0)),
                  pl.BlockSpec((tm, W), lambda i: (i, 0)),
                  pl.BlockSpec((tm, D), lambda i: (i, ga_blk)),
                  pl.BlockSpec((tm, D), lambda i: (i, gb_blk)),
                  pl.BlockSpec((W, D), lambda i: (0, 0), **const),
                  pl.BlockSpec((W, D), lambda i: (0, 0), **const),
                  pl.BlockSpec((D, D), lambda i: (0, 0), **const),
                  pl.BlockSpec((tm, D), lambda i: (i, 0)),
                  pl.BlockSpec((1, 1, D), lambda i: (i // per_b, 0, 0)),
                  pl.BlockSpec((1, D), lambda i: (0, 0)),
                  pl.BlockSpec((1, 1, D), lambda i: (i // per_b, 0, 0)),
                  pl.BlockSpec((1, 1, D), lambda i: (i // per_b, 0, 0)),
                  pl.BlockSpec((D, LANES), lambda i: (0, 0), **const)],
        out_specs=[pl.BlockSpec((tm, D), lambda i: (i, 0)),
                   pl.BlockSpec((tm * n_s, LANES), lambda i: (i, 0)),
                   pl.BlockSpec((tm, LANES), lambda i: (i, 0))],
        out_shape=[jax.ShapeDtypeStruct((T, D), F32),
                   jax.ShapeDtypeStruct((T * n_s, LANES), jnp.uint32),
                   jax.ShapeDtypeStruct((T, LANES), F32)],
        compiler_params=_params(("parallel",), 56),
        name="merge_out",
    )(ya, yb, proj, proj, wpa, wpb, wout, x2, g1, n2g, sc2, sh2, wr)


def _router_kernel(lg_ref, b_ref, e_ref, w_ref, *, groups, per_group):
    lg = lg_ref[...] + b_ref[...]
    lane = lax.broadcasted_iota(jnp.int32, lg.shape, 1)
    lane_f = lane.astype(F32)
    neg = jnp.float32(-1e30)
    big = jnp.float32(LANES)

    def first_argmax(vals):
        m = jnp.max(vals, axis=-1, keepdims=True)
        idx = jnp.min(jnp.where(vals == m, lane_f, big), axis=-1, keepdims=True)
        return m, idx

    gmask = lane < groups
    gl = jnp.where(gmask, lg, neg)
    gmax, gidx = first_argmax(gl)
    gsum = jnp.sum(jnp.where(gmask, jnp.exp(gl - gmax), 0.0), axis=-1, keepdims=True)
    gtop = 1.0 / gsum

    lo = groups + gidx * per_group
    emask = (lane_f >= lo) & (lane_f < lo + per_group)
    el = jnp.where(emask, lg, neg)
    m1, i1 = first_argmax(el)
    el2 = jnp.where(lane_f == i1, neg, el)
    m2, i2 = first_argmax(el2)
    den = jnp.sum(jnp.where(emask, jnp.exp(el - m1), 0.0), axis=-1, keepdims=True)
    p1 = 1.0 / den
    p2 = jnp.exp(m2 - m1) / den
    ps = p1 + p2
    w1 = gtop * p1 / ps
    w2 = gtop * p2 / ps
    e1 = (i1 - groups).astype(jnp.int32)
    e2 = (i2 - groups).astype(jnp.int32)
    e_ref[...] = jnp.where(lane == 0, e1, jnp.where(lane == 1, e2, 0))
    w_ref[...] = jnp.where(lane == 0, w1, jnp.where(lane == 1, w2, 0.0))


def _router_call(logits, bias, per_group):
    T = logits.shape[0]
    tm = min(1024, T)
    kern = functools.partial(_router_kernel, groups=MOE_GROUPS, per_group=per_group)
    return pl.pallas_call(
        kern,
        grid=(T // tm,),
        in_specs=[pl.BlockSpec((tm, LANES), lambda i: (i, 0)),
                  pl.BlockSpec((1, LANES), lambda i: (0, 0))],
        out_specs=[pl.BlockSpec((tm, LANES), lambda i: (i, 0)),
                   pl.BlockSpec((tm, LANES), lambda i: (i, 0))],
        out_shape=[jax.ShapeDtypeStruct((T, LANES), jnp.int32),
                   jax.ShapeDtypeStruct((T, LANES), F32)],
        compiler_params=_params(("parallel",), 16),
        name="router_topk",
    )(logits, bias)


def _row(ref, t, n_s):
    return ref.at[pl.ds(pl.multiple_of(t * n_s, n_s), n_s)]


def _dispatch_kernel(zr_ref, na_ref, dest_ref, h_ref, out_ref, zbuf, sem, zsem, *, n_s, block):
    n_assign = dest_ref.shape[2]
    tm = n_assign // TOP_K
    nb = out_ref.shape[0] // (block * n_s)

    def zero_block(b):
        start = pl.multiple_of(b * (block * n_s), block * n_s)
        return pltpu.make_async_copy(zbuf, out_ref.at[pl.ds(start, block * n_s)], zsem.at[0])

    @pl.when(pl.program_id(0) == 0)
    def _():
        zbuf[...] = jnp.zeros(zbuf.shape, zbuf.dtype)
        n_exp = zr_ref.shape[0]
        lax.fori_loop(0, n_exp, lambda e, c: (zero_block(zr_ref[e]).start(), c)[1], 0)
        lax.fori_loop(na_ref[0], nb, lambda b, c: (zero_block(b).start(), c)[1], 0)
        lax.fori_loop(0, n_exp, lambda e, c: (zero_block(0).wait(), c)[1], 0)
        lax.fori_loop(na_ref[0], nb, lambda b, c: (zero_block(0).wait(), c)[1], 0)

    tok_unroll = ISSUE_UNROLL // TOP_K

    def issue(g, c):
        for u in range(tok_unroll):
            t = g * tok_unroll + u
            for k in range(TOP_K):
                pltpu.make_async_copy(_row(h_ref, t, n_s),
                                      _row(out_ref, dest_ref[0, 0, t * TOP_K + k], n_s),
                                      sem.at[0]).start(priority=k % 2)
        return c
    lax.fori_loop(0, tm // tok_unroll, issue, 0)
    for _ in range(TOP_K):
        pltpu.make_async_copy(h_ref, out_ref.at[pl.ds(0, tm * n_s)], sem.at[0]).wait()


def _dispatch_call(h2p, dest, zero_rows, n_act, n_rows, n_s):
    T = h2p.shape[0] // n_s
    tm = min(1024, T)
    nt = T // tm
    kern = functools.partial(_dispatch_kernel, n_s=n_s, block=MOE_BLOCK)
    grid_spec = pltpu.PrefetchScalarGridSpec(
        num_scalar_prefetch=2,
        grid=(nt,),
        in_specs=[pl.BlockSpec((1, 1, tm * TOP_K), lambda i, zr, na: (i, 0, 0),
                               memory_space=pltpu.SMEM),
                  pl.BlockSpec((tm * n_s, LANES), lambda i, zr, na: (i, 0))],
        out_specs=pl.BlockSpec(memory_space=pl.ANY),
        scratch_shapes=[pltpu.VMEM((MOE_BLOCK * n_s, LANES), h2p.dtype),
                        pltpu.SemaphoreType.DMA((1,)),
                        pltpu.SemaphoreType.DMA((1,))],
    )
    return pl.pallas_call(
        kern,
        grid_spec=grid_spec,
        out_shape=jax.ShapeDtypeStruct((n_rows * n_s, LANES), h2p.dtype),
        compiler_params=_params(("arbitrary",), 16),
        name="row_dispatch",
    )(zero_rows, n_act, dest.reshape(nt, 1, tm * TOP_K), h2p)


def _stage_expert_weights(layer, be_ref, na_ref, nx_ref, i, hbm_refs, stage, bf_refs, sem):
    e = be_ref[i]
    first = (i == 0) | (e != be_ref[jnp.maximum(i - 1, 0)])

    def copies(expert):
        return [pltpu.make_async_copy(w.at[layer, expert], stage.at[k], sem.at[k])
                for k, w in enumerate(hbm_refs)]

    @pl.when(i == 0)
    def _():
        for cp in copies(e):
            cp.start()

    @pl.when((i < na_ref[0]) & first)
    def _():
        for k, cp in enumerate(copies(e)):
            cp.wait()
            bf_refs[k][...] = stage[k].astype(BF16)
        nxt = nx_ref[i]

        @pl.when(nxt >= 0)
        def _():
            for cp in copies(nxt):
                cp.start()


def _expert_up_kernel(be_ref, na_ref, nx_ref, xs_ref, wg_hbm, wu_hbm, hid_ref, stage, wgb, wub, sem,
                      *, layer):
    i = pl.program_id(0)
    active = i < na_ref[0]
    _stage_expert_weights(layer, be_ref, na_ref, nx_ref, i, (wg_hbm, wu_hbm), stage, (wgb, wub), sem)

    @pl.when(active)
    def _():
        rows = hid_ref.shape[0]
        los, his = _load_packed(xs_ref, rows, xs_ref.shape[0] // rows)
        x = jnp.concatenate([p.astype(BF16) for p in los + his], axis=1)
        a = _dot(x, wgb[...])
        b = _dot(x, wub[...])
        hid_ref[...] = (a * _sigmoid(a) * b).astype(hid_ref.dtype)

    @pl.when(jnp.logical_not(active))
    def _():
        hid_ref[...] = jnp.zeros(hid_ref.shape, hid_ref.dtype)


def _expert_down_kernel(be_ref, na_ref, nx_ref, hid_ref, wd_hbm, ys_ref, stage, wdb, sem, *, layer):
    i = pl.program_id(0)
    active = i < na_ref[0]
    _stage_expert_weights(layer, be_ref, na_ref, nx_ref, i, (wd_hbm,), stage, (wdb,), sem)

    @pl.when(active)
    def _():
        _store_packed(ys_ref, _dot(hid_ref[...], wdb[...]))

    @pl.when(jnp.logical_not(active))
    def _():
        ys_ref[...] = jnp.zeros(ys_ref.shape, ys_ref.dtype)


def _experts_call(xs, block_e, n_act, next_e, w_gate, w_up, w_down, l):
    D, DE = w_gate.shape[-2], w_gate.shape[-1]
    n_s = D // (2 * LANES)
    n_rows = xs.shape[0] // n_s
    BM = MOE_BLOCK
    nb = n_rows // BM

    def blk(i, na):
        return jnp.minimum(i, na[0] - 1)

    up_spec = pltpu.PrefetchScalarGridSpec(
        num_scalar_prefetch=3,
        grid=(nb,),
        in_specs=[pl.BlockSpec((BM * n_s, LANES), lambda i, be, na, nx: (blk(i, na), 0)),
                  pl.BlockSpec(memory_space=pl.ANY),
                  pl.BlockSpec(memory_space=pl.ANY)],
        out_specs=pl.BlockSpec((BM, DE), lambda i, be, na, nx: (i, 0)),
        scratch_shapes=[pltpu.VMEM((2, D, DE), F32),
                        pltpu.VMEM((D, DE), BF16), pltpu.VMEM((D, DE), BF16),
                        pltpu.SemaphoreType.DMA((2,))],
    )
    hid = pl.pallas_call(
        functools.partial(_expert_up_kernel, layer=l),
        grid_spec=up_spec,
        out_shape=jax.ShapeDtypeStruct((n_rows, DE), BF16),
        compiler_params=_params(("arbitrary",), 48),
        name="expert_up",
    )(block_e, n_act, next_e, xs, w_gate, w_up)

    down_spec = pltpu.PrefetchScalarGridSpec(
        num_scalar_prefetch=3,
        grid=(nb,),
        in_specs=[pl.BlockSpec((BM, DE), lambda i, be, na, nx: (blk(i, na), 0)),
                  pl.BlockSpec(memory_space=pl.ANY)],
        out_specs=pl.BlockSpec((BM * n_s, LANES), lambda i, be, na, nx: (i, 0)),
        scratch_shapes=[pltpu.VMEM((1, DE, D), F32),
                        pltpu.VMEM((DE, D), BF16),
                        pltpu.SemaphoreType.DMA((1,))],
    )
    return pl.pallas_call(
        functools.partial(_expert_down_kernel, layer=l),
        grid_spec=down_spec,
        out_shape=jax.ShapeDtypeStruct((n_rows * n_s, LANES), jnp.uint32),
        compiler_params=_params(("arbitrary",), 32),
        name="expert_down",
    )(block_e, n_act, next_e, hid, w_down)


def _combine_kernel(cur_ref, nxt_ref, x_ref, ys_ref, w_ref, g2_ref, fg_ref, o_ref, ybuf, sem,
                    *, final_norm):
    i = pl.program_id(0)
    n = pl.num_programs(0)
    tm, d = x_ref.shape
    half = d // 2
    n_s = half // LANES
    n_assign = tm * TOP_K
    slot = i % 2

    def gather(idx_ref, s):
        def issue(g, c):
            for u in range(ISSUE_UNROLL):
                r = g * ISSUE_UNROLL + u
                pltpu.make_async_copy(_row(ys_ref, idx_ref[0, 0, r], n_s), _row(ybuf.at[s], r, n_s),
                                      sem.at[s]).start(priority=u % 2)
            return c
        lax.fori_loop(0, n_assign // ISSUE_UNROLL, issue, 0)

    @pl.when(i == 0)
    def _():
        gather(cur_ref, 0)

    @pl.when(i + 1 < n)
    def _():
        gather(nxt_ref, 1 - slot)

    pltpu.make_async_copy(ys_ref.at[pl.ds(0, n_assign * n_s)], ybuf.at[slot], sem.at[slot]).wait()

    yg_ref = ybuf.at[slot]
    w = w_ref[...]
    w0 = w[:, 0:1]
    w1 = w[:, 1:2]
    lo0, hi0 = _load_packed(yg_ref, tm, n_s, offset=0, group=TOP_K)
    lo1, hi1 = _load_packed(yg_ref, tm, n_s, offset=1, group=TOP_K)
    ss = jnp.zeros((tm, 1), F32)
    for s in range(n_s):
        for base, p0, p1 in ((0, lo0[s], lo1[s]), (half, hi0[s], hi1[s])):
            cs = slice(base + s * LANES, base + (s + 1) * LANES)
            xv = x_ref[:, cs] + g2_ref[0, :, cs] * (w0 * p0 + w1 * p1)
            o_ref[:, cs] = xv
            if final_norm:
                ss = ss + jnp.sum(xv * xv, axis=-1, keepdims=True)
    if final_norm:
        o_ref[...] = o_ref[...] * lax.rsqrt(ss * (1.0 / d) + EPS) * fg_ref[...]


def _combine_call(x1, ys, dest, wts, g2, final_g, S, final_norm):
    T, D = x1.shape
    tm = min(512, S)
    nt = T // tm
    per_b = S // tm
    n_s = D // (2 * LANES)
    dest3 = dest.reshape(nt, 1, tm * TOP_K)
    kern = functools.partial(_combine_kernel, final_norm=final_norm)
    return pl.pallas_call(
        kern,
        grid=(nt,),
        in_specs=[pl.BlockSpec((1, 1, tm * TOP_K), lambda i: (i, 0, 0), memory_space=pltpu.SMEM),
                  pl.BlockSpec((1, 1, tm * TOP_K), lambda i: (jnp.minimum(i + 1, nt - 1), 0, 0),
                               memory_space=pltpu.SMEM),
                  pl.BlockSpec((tm, D), lambda i: (i, 0)),
                  pl.BlockSpec(memory_space=pl.ANY),
                  pl.BlockSpec((tm, LANES), lambda i: (i, 0)),
                  pl.BlockSpec((1, 1, D), lambda i: (i // per_b, 0, 0)),
                  pl.BlockSpec((1, D), lambda i: (0, 0))],
        out_specs=pl.BlockSpec((tm, D), lambda i: (i, 0)),
        out_shape=jax.ShapeDtypeStruct((T, D), F32),
        scratch_shapes=[pltpu.VMEM((2, tm * TOP_K * n_s, LANES), jnp.uint32),
                        pltpu.SemaphoreType.DMA((2,))],
        compiler_params=_params(("arbitrary",), 48),
        name="moe_combine",
    )(dest3, dest3, x1, ys, wts, g2, final_g)


def _rank_kernel(e_ref, rank_ref, cnt_ref, carry):
    @pl.when(pl.program_id(0) == 0)
    def _():
        carry[...] = jnp.zeros(carry.shape, F32)

    e = e_ref[...]
    tb = e.shape[0]
    lane = lax.broadcasted_iota(jnp.int32, e.shape, 1)
    oh0 = lane == e[:, 0:1]
    oh1 = lane == e[:, 1:2]
    both = oh0.astype(F32) + oh1.astype(F32)
    r = lax.broadcasted_iota(jnp.int32, (tb, tb), 0)
    c = lax.broadcasted_iota(jnp.int32, (tb, tb), 1)
    strict = (r > c).astype(BF16)
    before = _dot(strict, both.astype(BF16)) + carry[0:1, :]
    r0 = jnp.sum(jnp.where(oh0, before, 0.0), axis=-1, keepdims=True)
    r1 = jnp.sum(jnp.where(oh1, before, 0.0), axis=-1, keepdims=True)
    rank_ref[...] = jnp.where(lane == 0, r0, jnp.where(lane == 1, r1, 0.0)).astype(jnp.int32)
    carry[...] = carry[...] + jnp.sum(both, axis=0, keepdims=True)
    cnt_ref[...] = carry[...]


def _rank_call(eid):
    T = eid.shape[0]
    tb = min(512, T)
    return pl.pallas_call(
        _rank_kernel,
        grid=(T // tb,),
        in_specs=[pl.BlockSpec((tb, LANES), lambda i: (i, 0))],
        out_specs=[pl.BlockSpec((tb, LANES), lambda i: (i, 0)),
                   pl.BlockSpec((8, LANES), lambda i: (0, 0))],
        out_shape=[jax.ShapeDtypeStruct((T, LANES), jnp.int32),
                   jax.ShapeDtypeStruct((8, LANES), F32)],
        scratch_shapes=[pltpu.VMEM((8, LANES), F32)],
        compiler_params=_params(("arbitrary",), 16),
        name="assign_rank",
    )(eid)


def _dispatch_plan(eid_full, n_experts):
    T = eid_full.shape[0]
    A = T * TOP_K
    BM = MOE_BLOCK
    rank_full, cnt = _rank_call(eid_full)
    eid = eid_full[:, :TOP_K]
    counts = cnt[0, :n_experts].astype(jnp.int32)
    padded = (counts + BM - 1) // BM * BM
    pad_end = jnp.cumsum(padded)
    pad_start = pad_end - padded
    onehot = eid[:, :, None] == jnp.arange(n_experts, dtype=jnp.int32)[None, None, :]
    dest = (jnp.sum(jnp.where(onehot, pad_start[None, None, :], 0), axis=-1)
            + rank_full[:, :TOP_K]).reshape(A)
    nb = (A + n_experts * (BM - 1) + BM - 1) // BM
    block_start = jnp.arange(nb, dtype=jnp.int32) * BM
    block_e = jnp.minimum(jnp.sum(block_start[:, None] >= pad_end[None, :], axis=1),
                          n_experts - 1).astype(jnp.int32)
    n_act = (pad_end[-1] // BM).astype(jnp.int32).reshape(1)
    last_block = jnp.maximum(pad_end // BM - 1, 0).astype(jnp.int32)
    ex = jnp.arange(n_experts, dtype=jnp.int32)
    later = (ex[None, :] > ex[:, None]) & (counts[None, :] > 0)
    next_of = jnp.min(jnp.where(later, ex[None, :], n_experts), axis=1)
    next_of = jnp.where(next_of < n_experts, next_of, -1)
    next_e = jnp.sum(jnp.where(block_e[:, None] == ex[None, :], next_of[None, :], 0),
                     axis=1).astype(jnp.int32)
    return dest.astype(jnp.int32), block_e, n_act, next_e, last_block, nb * BM


def kernel(x, c, ada_w, ada_b, norm1_g, w_in, gm_vn_g, gm_vn_b, gm_ws, gm_bs, gla_wa2, gla_ba,
           gla_on_g, w_pa, w_pb, w_out, norm2_g, w_rg, b_rg, w_re, b_re, w_e_gate, w_e_up,
           w_e_down, final_g):
    B, S, D = x.shape
    T = B * S
    L = ada_w.shape[0]
    W = gm_vn_g.shape[1]
    G, C = gm_ws.shape[1], gm_ws.shape[2]
    RANK, DK = gla_wa2.shape[1], gla_wa2.shape[2]
    DV = gla_on_g.shape[1]
    E = w_e_gate.shape[1]
    per_group = E // MOE_GROUPS
    SUB = D // (2 * LANES)
    assert B <= 8 and MOE_GROUPS + E <= LANES and RANK <= LANES

    c8 = jnp.zeros((8, D), F32).at[:B].set(c)
    mod = _ada_call(c8, ada_w, ada_b.reshape(L, 1, 6 * D))

    o_alr = 2 * W + 2 * DK + 2 * DV
    x2 = x.reshape(T, D)
    for l in range(L):
        sh1, sc1, g1, sh2, sc2, g2 = [mod[l, :B, k * D:(k + 1) * D].reshape(B, 1, D) for k in range(6)]
        wl = w_in[l]
        w_main = jnp.concatenate([wl[:, :o_alr], wl[:, o_alr + RANK:]], axis=1).astype(BF16)
        w_alr = jnp.zeros((D, LANES), BF16).at[:, :RANK].set(wl[:, o_alr:o_alr + RANK].astype(BF16))
        wa2p = jnp.zeros((LANES, DK), BF16).at[:RANK].set(gla_wa2[l].astype(BF16))
        bias_full = jnp.repeat(gm_bs[l].T, W // G, axis=1)
        wr = jnp.zeros((D, LANES), BF16).at[:, :MOE_GROUPS + E].set(
            jnp.concatenate([w_rg[l], w_re[l]], axis=1).astype(BF16))
        br = jnp.zeros((1, LANES), F32).at[0, :MOE_GROUPS + E].set(
            jnp.concatenate([b_rg[l], b_re[l]]))

        proj, alr = _inproj_call(x2, norm1_g[l].reshape(1, D), sc1, sh1, w_main, w_alr, S)
        ya = _gmlp_call(proj, gm_vn_g[l].reshape(1, W), gm_vn_b[l].reshape(1, W), gm_ws[l], bias_full)
        yb = _gla_call(proj, alr, wa2p, gla_ba[l].reshape(1, DK), gla_on_g[l].reshape(1, DV), B, S, W)
        x1, h2, logits = _merge_call(ya, yb, proj, w_pa[l].astype(BF16), w_pb[l].astype(BF16),
                                     w_out[l].astype(BF16), x2, g1, norm2_g[l].reshape(1, D),
                                     sc2, sh2, wr, S, W)
        eid, wts = _router_call(logits, br, per_group)
        dest, block_e, n_act, next_e, last_block, n_rows = _dispatch_plan(eid, E)
        xs = _dispatch_call(h2, dest, last_block, n_act, n_rows, SUB)
        ys = _experts_call(xs, block_e, n_act, next_e, w_e_gate, w_e_up, w_e_down, l)
        x2 = _combine_call(x1, ys, dest, wts, g2, final_g.reshape(1, D), S, final_norm=(l == L - 1))
    return x2.reshape(B, S, D)
```

```python
import functools

import jax
import jax.numpy as jnp
from jax import lax
from jax.experimental import pallas as pl
from jax.experimental.pallas import tpu as pltpu

F32 = jnp.float32
BF16 = jnp.bfloat16

EPS = 1e-6
GLA_HEADS = 4
GLA_TAU = 16.0
GLA_CHUNK = 64
MOE_GROUPS = 4
TOP_K = 2

LANES = 128
MIB = 1024 * 1024
MOE_BLOCK = 256
ISSUE_UNROLL = 8

NT_DIMS = (((1,), (1,)), ((), ()))
TN_DIMS = (((0,), (0,)), ((), ()))


def _params(semantics, vmem_mib):
    return pltpu.CompilerParams(dimension_semantics=semantics,
                                vmem_limit_bytes=int(vmem_mib * MIB))


def _dot(a, b):
    return jnp.dot(a, b, preferred_element_type=F32)


def _sigmoid(x):
    return 1.0 / (1.0 + jnp.exp(-x))


def _gelu(x):
    return 0.5 * x * (1.0 + lax.erf(x * (2.0 ** -0.5)))


def _pack_pairs(lo, hi):
    lo_w = pltpu.bitcast(lo.astype(BF16).astype(F32), jnp.uint32) >> 16
    hi_w = pltpu.bitcast(hi.astype(BF16).astype(F32), jnp.uint32) & jnp.uint32(0xFFFF0000)
    return lo_w | hi_w


def _unpack_pairs(w):
    lo = pltpu.bitcast(w << 16, F32)
    hi = pltpu.bitcast(w & jnp.uint32(0xFFFF0000), F32)
    return lo, hi


def _store_packed(ref, x):
    rows, d = x.shape
    half = d // 2
    n_s = half // LANES
    for s in range(n_s):
        lo = x[:, s * LANES:(s + 1) * LANES]
        hi = x[:, half + s * LANES:half + (s + 1) * LANES]
        ref[pl.ds(s, rows, stride=n_s), :] = _pack_pairs(lo, hi)


def _load_packed(ref, rows, n_s, offset=0, group=1):
    los, his = [], []
    for s in range(n_s):
        lo, hi = _unpack_pairs(ref[pl.ds(offset * n_s + s, rows, stride=n_s * group), :])
        los.append(lo)
        his.append(hi)
    return los, his


def _ada_kernel(c_ref, w_ref, b_ref, o_ref):
    c = c_ref[...]
    cond = c * _sigmoid(c)
    o_ref[0] = _dot(cond.astype(BF16), w_ref[0].astype(BF16)) + b_ref[0]


def _ada_call(c8, ada_w, ada_b3):
    L, D, N = ada_w.shape
    tn = min(1024, N)
    return pl.pallas_call(
        _ada_kernel,
        grid=(L, N // tn),
        in_specs=[pl.BlockSpec((8, D), lambda l, j: (0, 0)),
                  pl.BlockSpec((1, D, tn), lambda l, j: (l, 0, j)),
                  pl.BlockSpec((1, 1, tn), lambda l, j: (l, 0, j))],
        out_specs=pl.BlockSpec((1, 8, tn), lambda l, j: (l, 0, j)),
        out_shape=jax.ShapeDtypeStruct((L, 8, N), F32),
        compiler_params=_params(("parallel", "parallel"), 40),
        name="ada_mod",
    )(c8, ada_w, ada_b3)


def _inproj_kernel(x_ref, g_ref, sc_ref, sh_ref, w_ref, wt_ref, walr_ref, o_ref, alr_ref, h_scr,
                   *, n_head):
    j = pl.program_id(1)

    @pl.when(j == 0)
    def _():
        x = x_ref[...]
        ms = jnp.mean(x * x, axis=-1, keepdims=True)
        y = x * lax.rsqrt(ms + EPS) * g_ref[...]
        h = (y * (1.0 + sc_ref[0]) + sh_ref[0]).astype(BF16)
        h_scr[...] = h
        alr_ref[...] = _dot(h, walr_ref[...])

    @pl.when(j < n_head)
    def _():
        o_ref[...] = _dot(h_scr[...], w_ref[0]).astype(o_ref.dtype)

    @pl.when(j >= n_head)
    def _():
        o_ref[...] = _dot(h_scr[...], wt_ref[...]).astype(o_ref.dtype)


def _inproj_call(x2, g, sc, sh, w_all, w_tail, w_alr, S, l, head_cols):
    T, D = x2.shape
    NM = head_cols + w_tail.shape[1]
    tm = min(1024, S)
    tn = min(1024, NM)
    n_head = head_cols // tn
    assert head_cols % tn == 0 and w_tail.shape[1] % tn == 0
    per_b = S // tm
    kern = functools.partial(_inproj_kernel, n_head=n_head)
    return pl.pallas_call(
        kern,
        grid=(T // tm, NM // tn),
        in_specs=[pl.BlockSpec((tm, D), lambda i, j: (i, 0)),
                  pl.BlockSpec((1, D), lambda i, j: (0, 0)),
                  pl.BlockSpec((1, 1, D), lambda i, j: (i // per_b, 0, 0)),
                  pl.BlockSpec((1, 1, D), lambda i, j: (i // per_b, 0, 0)),
                  pl.BlockSpec((1, D, tn), lambda i, j: (l, 0, jnp.minimum(j, n_head - 1))),
                  pl.BlockSpec((D, tn), lambda i, j: (0, jnp.maximum(j - n_head, 0))),
                  pl.BlockSpec((D, LANES), lambda i, j: (0, 0))],
        out_specs=[pl.BlockSpec((tm, tn), lambda i, j: (i, j)),
                   pl.BlockSpec((tm, LANES), lambda i, j: (i, 0))],
        out_shape=[jax.ShapeDtypeStruct((T, NM), BF16),
                   jax.ShapeDtypeStruct((T, LANES), F32)],
        scratch_shapes=[pltpu.VMEM((tm, D), BF16)],
        compiler_params=_params(("parallel", "arbitrary"), 56),
        name="in_proj",
    )(x2, g, sc, sh, w_all, w_tail, w_alr)


def _gmlp_kernel(u_ref, v_ref, g_ref, b_ref, ws_ref, bias_ref, o_ref, vn_scr, *, groups, chunk):
    rows, width = v_ref.shape
    gc = width // groups
    r = lax.broadcasted_iota(jnp.int32, (chunk, chunk), 0)
    c = lax.broadcasted_iota(jnp.int32, (chunk, chunk), 1)
    causal = r >= c
    for ci in range(rows // chunk):
        rs = slice(ci * chunk, (ci + 1) * chunk)
        gv = _gelu(v_ref[rs, :].astype(F32))
        mu = jnp.mean(gv, axis=-1, keepdims=True)
        xc = gv - mu
        var = jnp.mean(xc * xc, axis=-1, keepdims=True)
        vn = xc * lax.rsqrt(var + EPS) * g_ref[...] + b_ref[...]
        vn_scr[...] = vn.astype(BF16)
        for gi in range(groups):
            cs = slice(gi * gc, (gi + 1) * gc)
            wm = jnp.where(causal, ws_ref[gi], 0.0).astype(BF16)
            mixed = _dot(wm, vn_scr[:, cs]) + bias_ref[:, cs]
            gu = _gelu(u_ref[rs, cs].astype(F32))
            o_ref[rs, cs] = (gu * mixed).astype(o_ref.dtype)


def _gmlp_call(proj, vn_g, vn_b, ws, bias_full):
    T = proj.shape[0]
    G, C, _ = ws.shape
    W = vn_g.shape[1]
    R = min(2 * C, T)
    kern = functools.partial(_gmlp_kernel, groups=G, chunk=C)
    return pl.pallas_call(
        kern,
        grid=(T // R,),
        in_specs=[pl.BlockSpec((R, W), lambda i: (i, 0)),
                  pl.BlockSpec((R, W), lambda i: (i, 1)),
                  pl.BlockSpec((1, W), lambda i: (0, 0)),
                  pl.BlockSpec((1, W), lambda i: (0, 0)),
                  pl.BlockSpec((G, C, C), lambda i: (0, 0, 0)),
                  pl.BlockSpec((C, W), lambda i: (0, 0))],
        out_specs=pl.BlockSpec((R, W), lambda i: (i, 0)),
        out_shape=jax.ShapeDtypeStruct((T, W), BF16),
        scratch_shapes=[pltpu.VMEM((C, W), BF16)],
        compiler_params=_params(("parallel",), 32),
        name="gmlp_gate",
    )(proj, proj, vn_g, vn_b, ws, bias_full)


def _gla_kernel(q_ref, k_ref, v_ref, r_ref, alr_ref, wa2_ref, ba_ref, g_ref, o_ref,
                st_ref, la_ref, *, heads):
    rows, dk = q_ref.shape
    dv = v_ref.shape[1]
    hk = dk // heads
    hv = dv // heads
    C = GLA_CHUNK
    scale = hk ** -0.5

    @pl.when(pl.program_id(1) == 0)
    def _():
        st_ref[...] = jnp.zeros(st_ref.shape, F32)

    z = _dot(alr_ref[...].astype(BF16), wa2_ref[...]) + ba_ref[...]
    la_ref[...] = (jnp.minimum(z, 0.0) - jnp.log(1.0 + jnp.exp(-jnp.abs(z)))) * (1.0 / GLA_TAU)

    ri = lax.broadcasted_iota(jnp.int32, (C, C), 0)
    ci = lax.broadcasted_iota(jnp.int32, (C, C), 1)
    causal = ri >= ci
    tri = causal.astype(BF16)

    def chunk_step(c, carry):
        rs = pl.ds(pl.multiple_of(c * C, C), C)
        la = la_ref[rs, :]
        hi = la.astype(BF16)
        r1 = la - hi.astype(F32)
        mid = r1.astype(BF16)
        lo = (r1 - mid.astype(F32)).astype(BF16)
        cum = _dot(tri, hi) + _dot(tri, mid) + _dot(tri, lo)
        cl = cum[C - 1:C, :]
        q = q_ref[rs, :].astype(F32) * scale
        k = k_ref[rs, :].astype(F32)
        qd = (q * jnp.exp(cum)).astype(BF16)
        ki = (k * jnp.exp(-cum)).astype(BF16)
        ke = (k * jnp.exp(cl - cum)).astype(BF16)
        dec = jnp.exp(cl)
        for h in range(heads):
            ks = slice(h * hk, (h + 1) * hk)
            vs = slice(h * hv, (h + 1) * hv)
            vh = v_ref[rs, vs]
            s = lax.dot_general(qd[:, ks], ki[:, ks], NT_DIMS, preferred_element_type=F32)
            s = jnp.where(causal, s, 0.0).astype(BF16)
            st = st_ref[h]
            o = _dot(s, vh) + lax.dot_general(qd[:, ks], st.astype(BF16), NT_DIMS,
                                              preferred_element_type=F32)
            st_ref[h] = st * dec[:, ks] + lax.dot_general(vh, ke[:, ks], TN_DIMS,
                                                          preferred_element_type=F32)
            ms = jnp.mean(o * o, axis=-1, keepdims=True)
            on = o * lax.rsqrt(ms + EPS) * g_ref[:, vs]
            rr = r_ref[rs, vs].astype(F32)
            o_ref[rs, vs] = (on * (rr * _sigmoid(rr))).astype(o_ref.dtype)
        return carry

    lax.fori_loop(0, rows // C, chunk_step, 0)


def _gla_call(proj, alr, wa2p, ba, on_g, B, S, W):
    DK = wa2p.shape[1]
    DV = on_g.shape[1]
    H = GLA_HEADS
    Cb = min(256, S)
    nb = S // Cb
    q_blk = (2 * W) // DK
    k_blk = (2 * W + DK) // DK
    v_blk = (2 * W + 2 * DK) // DV
    r_blk = (2 * W + 2 * DK + DV) // DV
    kern = functools.partial(_gla_kernel, heads=H)
    return pl.pallas_call(
        kern,
        grid=(B, nb),
        in_specs=[pl.BlockSpec((Cb, DK), lambda b, i: (b * nb + i, q_blk)),
                  pl.BlockSpec((Cb, DK), lambda b, i: (b * nb + i, k_blk)),
                  pl.BlockSpec((Cb, DV), lambda b, i: (b * nb + i, v_blk)),
                  pl.BlockSpec((Cb, DV), lambda b, i: (b * nb + i, r_blk)),
                  pl.BlockSpec((Cb, LANES), lambda b, i: (b * nb + i, 0)),
                  pl.BlockSpec((LANES, DK), lambda b, i: (0, 0)),
                  pl.BlockSpec((1, DK), lambda b, i: (0, 0)),
                  pl.BlockSpec((1, DV), lambda b, i: (0, 0))],
        out_specs=pl.BlockSpec((Cb, DV), lambda b, i: (b * nb + i, 0)),
        out_shape=jax.ShapeDtypeStruct((B * S, DV), BF16),
        scratch_shapes=[pltpu.VMEM((H, DV // H, DK // H), F32),
                        pltpu.VMEM((Cb, DK), F32)],
        compiler_params=_params(("parallel", "arbitrary"), 32),
        name="gla",
    )(proj, proj, proj, proj, alr, wa2p, ba, on_g)


def _merge_kernel(ya_ref, yb_ref, ga_ref, gb_ref, wpa_ref, wpb_ref, wout_ref, x_ref, g1_ref,
                  n2_ref, sc_ref, sh_ref, wr_ref, x1_ref, h2_ref, lg_ref):
    a = _dot(ya_ref[...], wpa_ref[0])
    b = _dot(yb_ref[...], wpb_ref[0])
    y = _sigmoid(ga_ref[...].astype(F32)) * a + _sigmoid(gb_ref[...].astype(F32)) * b
    out = _dot(y.astype(BF16), wout_ref[0])
    x1 = x_ref[...] + g1_ref[0] * out
    x1_ref[...] = x1
    ms = jnp.mean(x1 * x1, axis=-1, keepdims=True)
    h2 = (x1 * lax.rsqrt(ms + EPS) * n2_ref[...]) * (1.0 + sc_ref[0]) + sh_ref[0]
    _store_packed(h2_ref, h2)
    lg_ref[...] = _dot(h2.astype(BF16), wr_ref[...])


def _merge_call(ya, yb, proj, wpa, wpb, wout, x2, g1, n2g, sc2, sh2, wr, S, W, l):
    T, D = x2.shape
    ga_blk = (proj.shape[1] - 2 * D) // D
    gb_blk = ga_blk + 1
    tm = min(256, S)
    per_b = S // tm
    n_s = D // (2 * LANES)
    const = dict(pipeline_mode=pl.Buffered(1))
    return pl.pallas_call(
        _merge_kernel,
        grid=(T // tm,),
        in_specs=[pl.BlockSpec((tm, W), lambda i: (i, 0)),
                  pl.BlockSpec((tm, W), lambda i: (i, 0)),
                  pl.BlockSpec((tm, D), lambda i: (i, ga_blk)),
                  pl.BlockSpec((tm, D), lambda i: (i, gb_blk)),
                  pl.BlockSpec((1, W, D), lambda i: (l, 0, 0), **const),
                  pl.BlockSpec((1, W, D), lambda i: (l, 0, 0), **const),
                  pl.BlockSpec((1, D, D), lambda i: (l, 0, 0), **const),
                  pl.BlockSpec((tm, D), lambda i: (i, 0)),
                  pl.BlockSpec((1, 1, D), lambda i: (i // per_b, 0, 0)),
                  pl.BlockSpec((1, D), lambda i: (0, 0)),
                  pl.BlockSpec((1, 1, D), lambda i: (i // per_b, 0, 0)),
                  pl.BlockSpec((1, 1, D), lambda i: (i // per_b, 0, 0)),
                  pl.BlockSpec((D, LANES), lambda i: (0, 0), **const)],
        out_specs=[pl.BlockSpec((tm, D), lambda i: (i, 0)),
                   pl.BlockSpec((tm * n_s, LANES), lambda i: (i, 0)),
                   pl.BlockSpec((tm, LANES), lambda i: (i, 0))],
        out_shape=[jax.ShapeDtypeStruct((T, D), F32),
                   jax.ShapeDtypeStruct((T * n_s, LANES), jnp.uint32),
                   jax.ShapeDtypeStruct((T, LANES), F32)],
        compiler_params=_params(("parallel",), 56),
        name="merge_out",
    )(ya, yb, proj, proj, wpa, wpb, wout, x2, g1, n2g, sc2, sh2, wr)


def _router_kernel(lg_ref, b_ref, e_ref, w_ref, *, groups, per_group):
    lg = lg_ref[...] + b_ref[...]
    lane = lax.broadcasted_iota(jnp.int32, lg.shape, 1)
    lane_f = lane.astype(F32)
    neg = jnp.float32(-1e30)
    big = jnp.float32(LANES)

    def first_argmax(vals):
        m = jnp.max(vals, axis=-1, keepdims=True)
        idx = jnp.min(jnp.where(vals == m, lane_f, big), axis=-1, keepdims=True)
        return m, idx

    gmask = lane < groups
    gl = jnp.where(gmask, lg, neg)
    gmax, gidx = first_argmax(gl)
    gsum = jnp.sum(jnp.where(gmask, jnp.exp(gl - gmax), 0.0), axis=-1, keepdims=True)
    gtop = 1.0 / gsum

    lo = groups + gidx * per_group
    emask = (lane_f >= lo) & (lane_f < lo + per_group)
    el = jnp.where(emask, lg, neg)
    m1, i1 = first_argmax(el)
    el2 = jnp.where(lane_f == i1, neg, el)
    m2, i2 = first_argmax(el2)
    den = jnp.sum(jnp.where(emask, jnp.exp(el - m1), 0.0), axis=-1, keepdims=True)
    p1 = 1.0 / den
    p2 = jnp.exp(m2 - m1) / den
    ps = p1 + p2
    w1 = gtop * p1 / ps
    w2 = gtop * p2 / ps
    e1 = (i1 - groups).astype(jnp.int32)
    e2 = (i2 - groups).astype(jnp.int32)
    e_ref[...] = jnp.where(lane == 0, e1, jnp.where(lane == 1, e2, 0))
    w_ref[...] = jnp.where(lane == 0, w1, jnp.where(lane == 1, w2, 0.0))


def _router_call(logits, bias, per_group):
    T = logits.shape[0]
    tm = min(1024, T)
    kern = functools.partial(_router_kernel, groups=MOE_GROUPS, per_group=per_group)
    return pl.pallas_call(
        kern,
        grid=(T // tm,),
        in_specs=[pl.BlockSpec((tm, LANES), lambda i: (i, 0)),
                  pl.BlockSpec((1, LANES), lambda i: (0, 0))],
        out_specs=[pl.BlockSpec((tm, LANES), lambda i: (i, 0)),
                   pl.BlockSpec((tm, LANES), lambda i: (i, 0))],
        out_shape=[jax.ShapeDtypeStruct((T, LANES), jnp.int32),
                   jax.ShapeDtypeStruct((T, LANES), F32)],
        compiler_params=_params(("parallel",), 16),
        name="router_topk",
    )(logits, bias)


def _row(ref, t, n_s):
    return ref.at[pl.ds(pl.multiple_of(t * n_s, n_s), n_s)]


def _dispatch_kernel(zr_ref, na_ref, dest_ref, h_ref, out_ref, zbuf, sem, zsem, *, n_s, block):
    n_assign = dest_ref.shape[2]
    tm = n_assign // TOP_K
    nb = out_ref.shape[0] // (block * n_s)

    def zero_block(b):
        start = pl.multiple_of(b * (block * n_s), block * n_s)
        return pltpu.make_async_copy(zbuf, out_ref.at[pl.ds(start, block * n_s)], zsem.at[0])

    @pl.when(pl.program_id(0) == 0)
    def _():
        zbuf[...] = jnp.zeros(zbuf.shape, zbuf.dtype)
        n_exp = zr_ref.shape[0]
        lax.fori_loop(0, n_exp, lambda e, c: (zero_block(zr_ref[e]).start(), c)[1], 0)
        lax.fori_loop(na_ref[0], nb, lambda b, c: (zero_block(b).start(), c)[1], 0)
        lax.fori_loop(0, n_exp, lambda e, c: (zero_block(0).wait(), c)[1], 0)
        lax.fori_loop(na_ref[0], nb, lambda b, c: (zero_block(0).wait(), c)[1], 0)

    tok_unroll = ISSUE_UNROLL // TOP_K

    def issue(g, c):
        for u in range(tok_unroll):
            t = g * tok_unroll + u
            for k in range(TOP_K):
                pltpu.make_async_copy(_row(h_ref, t, n_s),
                                      _row(out_ref, dest_ref[0, 0, t * TOP_K + k], n_s),
                                      sem.at[0]).start(priority=k % 2)
        return c
    lax.fori_loop(0, tm // tok_unroll, issue, 0)
    for _ in range(TOP_K):
        pltpu.make_async_copy(h_ref, out_ref.at[pl.ds(0, tm * n_s)], sem.at[0]).wait()


def _dispatch_call(h2p, dest, zero_rows, n_act, n_rows, n_s):
    T = h2p.shape[0] // n_s
    tm = min(1024, T)
    nt = T // tm
    kern = functools.partial(_dispatch_kernel, n_s=n_s, block=MOE_BLOCK)
    grid_spec = pltpu.PrefetchScalarGridSpec(
        num_scalar_prefetch=2,
        grid=(nt,),
        in_specs=[pl.BlockSpec((1, 1, tm * TOP_K), lambda i, zr, na: (i, 0, 0),
                               memory_space=pltpu.SMEM),
                  pl.BlockSpec((tm * n_s, LANES), lambda i, zr, na: (i, 0))],
        out_specs=pl.BlockSpec(memory_space=pl.ANY),
        scratch_shapes=[pltpu.VMEM((MOE_BLOCK * n_s, LANES), h2p.dtype),
                        pltpu.SemaphoreType.DMA((1,)),
                        pltpu.SemaphoreType.DMA((1,))],
    )
    return pl.pallas_call(
        kern,
        grid_spec=grid_spec,
        out_shape=jax.ShapeDtypeStruct((n_rows * n_s, LANES), h2p.dtype),
        compiler_params=_params(("arbitrary",), 16),
        name="row_dispatch",
    )(zero_rows, n_act, dest.reshape(nt, 1, tm * TOP_K), h2p)


def _stage_expert_weights(layer, be_ref, na_ref, nx_ref, i, hbm_refs, stage, bf_refs, sem):
    e = be_ref[i]
    first = (i == 0) | (e != be_ref[jnp.maximum(i - 1, 0)])

    def copies(expert):
        return [pltpu.make_async_copy(w.at[layer, expert], stage[k], sem.at[k])
                for k, w in enumerate(hbm_refs)]

    @pl.when(i == 0)
    def _():
        for cp in copies(e):
            cp.start()

    @pl.when((i < na_ref[0]) & first)
    def _():
        for k, cp in enumerate(copies(e)):
            cp.wait()
            bf_refs[k][...] = stage[k][...].astype(BF16)
        nxt = nx_ref[i]

        @pl.when(nxt >= 0)
        def _():
            for cp in copies(nxt):
                cp.start()


def _expert_kernel(be_ref, na_ref, nx_ref, xs_ref, wg_hbm, wu_hbm, wd_hbm, ys_ref,
                   sg, su, sd, wgb, wub, wdb, sem, *, layer):
    i = pl.program_id(0)
    active = i < na_ref[0]
    _stage_expert_weights(layer, be_ref, na_ref, nx_ref, i, (wg_hbm, wu_hbm, wd_hbm),
                          (sg, su, sd), (wgb, wub, wdb), sem)

    @pl.when(active)
    def _():
        n_s = sd.shape[1] // (2 * LANES)
        rows = xs_ref.shape[0] // n_s
        los, his = _load_packed(xs_ref, rows, n_s)
        x = jnp.concatenate([p.astype(BF16) for p in los + his], axis=1)
        a = _dot(x, wgb[...])
        b = _dot(x, wub[...])
        hid = (a * _sigmoid(a) * b).astype(BF16)
        _store_packed(ys_ref, _dot(hid, wdb[...]))

    @pl.when(jnp.logical_not(active))
    def _():
        ys_ref[...] = jnp.zeros(ys_ref.shape, ys_ref.dtype)


def _experts_call(xs, block_e, n_act, next_e, w_gate, w_up, w_down, l):
    D, DE = w_gate.shape[-2], w_gate.shape[-1]
    n_s = D // (2 * LANES)
    n_rows = xs.shape[0] // n_s
    BM = MOE_BLOCK
    nb = n_rows // BM
    any_spec = pl.BlockSpec(memory_space=pl.ANY)
    grid_spec = pltpu.PrefetchScalarGridSpec(
        num_scalar_prefetch=3,
        grid=(nb,),
        in_specs=[pl.BlockSpec((BM * n_s, LANES),
                               lambda i, be, na, nx: (jnp.minimum(i, na[0] - 1), 0)),
                  any_spec, any_spec, any_spec],
        out_specs=pl.BlockSpec((BM * n_s, LANES), lambda i, be, na, nx: (i, 0)),
        scratch_shapes=[pltpu.VMEM((D, DE), F32), pltpu.VMEM((D, DE), F32), pltpu.VMEM((DE, D), F32),
                        pltpu.VMEM((D, DE), BF16), pltpu.VMEM((D, DE), BF16),
                        pltpu.VMEM((DE, D), BF16),
                        pltpu.SemaphoreType.DMA((3,))],
    )
    return pl.pallas_call(
        functools.partial(_expert_kernel, layer=l),
        grid_spec=grid_spec,
        out_shape=jax.ShapeDtypeStruct((n_rows * n_s, LANES), jnp.uint32),
        compiler_params=_params(("arbitrary",), 56),
        name="expert_mlp",
    )(block_e, n_act, next_e, xs, w_gate, w_up, w_down)


def _combine_kernel(cur_ref, nxt_ref, x_ref, ys_ref, w_ref, g2_ref, fg_ref, o_ref, ybuf, sem,
                    *, final_norm):
    i = pl.program_id(0)
    n = pl.num_programs(0)
    tm, d = x_ref.shape
    half = d // 2
    n_s = half // LANES
    n_assign = tm * TOP_K
    slot = i % 2

    def gather(idx_ref, s):
        def issue(g, c):
            for u in range(ISSUE_UNROLL):
                r = g * ISSUE_UNROLL + u
                pltpu.make_async_copy(_row(ys_ref, idx_ref[0, 0, r], n_s), _row(ybuf.at[s], r, n_s),
                                      sem.at[s]).start(priority=u % 2)
            return c
        lax.fori_loop(0, n_assign // ISSUE_UNROLL, issue, 0)

    @pl.when(i == 0)
    def _():
        gather(cur_ref, 0)

    @pl.when(i + 1 < n)
    def _():
        gather(nxt_ref, 1 - slot)

    pltpu.make_async_copy(ys_ref.at[pl.ds(0, n_assign * n_s)], ybuf.at[slot], sem.at[slot]).wait()

    yg_ref = ybuf.at[slot]
    w = w_ref[...]
    w0 = w[:, 0:1]
    w1 = w[:, 1:2]
    lo0, hi0 = _load_packed(yg_ref, tm, n_s, offset=0, group=TOP_K)
    lo1, hi1 = _load_packed(yg_ref, tm, n_s, offset=1, group=TOP_K)
    ss = jnp.zeros((tm, 1), F32)
    for s in range(n_s):
        for base, p0, p1 in ((0, lo0[s], lo1[s]), (half, hi0[s], hi1[s])):
            cs = slice(base + s * LANES, base + (s + 1) * LANES)
            xv = x_ref[:, cs] + g2_ref[0, :, cs] * (w0 * p0 + w1 * p1)
            o_ref[:, cs] = xv
            if final_norm:
                ss = ss + jnp.sum(xv * xv, axis=-1, keepdims=True)
    if final_norm:
        o_ref[...] = o_ref[...] * lax.rsqrt(ss * (1.0 / d) + EPS) * fg_ref[...]


def _combine_call(x1, ys, dest, wts, g2, final_g, S, final_norm):
    T, D = x1.shape
    tm = min(512, S)
    nt = T // tm
    per_b = S // tm
    n_s = D // (2 * LANES)
    dest3 = dest.reshape(nt, 1, tm * TOP_K)
    kern = functools.partial(_combine_kernel, final_norm=final_norm)
    return pl.pallas_call(
        kern,
        grid=(nt,),
        in_specs=[pl.BlockSpec((1, 1, tm * TOP_K), lambda i: (i, 0, 0), memory_space=pltpu.SMEM),
                  pl.BlockSpec((1, 1, tm * TOP_K), lambda i: (jnp.minimum(i + 1, nt - 1), 0, 0),
                               memory_space=pltpu.SMEM),
                  pl.BlockSpec((tm, D), lambda i: (i, 0)),
                  pl.BlockSpec(memory_space=pl.ANY),
                  pl.BlockSpec((tm, LANES), lambda i: (i, 0)),
                  pl.BlockSpec((1, 1, D), lambda i: (i // per_b, 0, 0)),
                  pl.BlockSpec((1, D), lambda i: (0, 0))],
        out_specs=pl.BlockSpec((tm, D), lambda i: (i, 0)),
        out_shape=jax.ShapeDtypeStruct((T, D), F32),
        scratch_shapes=[pltpu.VMEM((2, tm * TOP_K * n_s, LANES), jnp.uint32),
                        pltpu.SemaphoreType.DMA((2,))],
        compiler_params=_params(("arbitrary",), 48),
        name="moe_combine",
    )(dest3, dest3, x1, ys, wts, g2, final_g)


def _rank_kernel(e_ref, rank_ref, cnt_ref, carry):
    @pl.when(pl.program_id(0) == 0)
    def _():
        carry[...] = jnp.zeros(carry.shape, F32)

    e = e_ref[...]
    tb = e.shape[0]
    lane = lax.broadcasted_iota(jnp.int32, e.shape, 1)
    oh0 = lane == e[:, 0:1]
    oh1 = lane == e[:, 1:2]
    both = oh0.astype(F32) + oh1.astype(F32)
    r = lax.broadcasted_iota(jnp.int32, (tb, tb), 0)
    c = lax.broadcasted_iota(jnp.int32, (tb, tb), 1)
    strict = (r > c).astype(BF16)
    before = _dot(strict, both.astype(BF16)) + carry[0:1, :]
    r0 = jnp.sum(jnp.where(oh0, before, 0.0), axis=-1, keepdims=True)
    r1 = jnp.sum(jnp.where(oh1, before, 0.0), axis=-1, keepdims=True)
    rank_ref[...] = jnp.where(lane == 0, r0, jnp.where(lane == 1, r1, 0.0)).astype(jnp.int32)
    carry[...] = carry[...] + jnp.sum(both, axis=0, keepdims=True)
    cnt_ref[...] = carry[...]


def _rank_call(eid):
    T = eid.shape[0]
    tb = min(512, T)
    return pl.pallas_call(
        _rank_kernel,
        grid=(T // tb,),
        in_specs=[pl.BlockSpec((tb, LANES), lambda i: (i, 0))],
        out_specs=[pl.BlockSpec((tb, LANES), lambda i: (i, 0)),
                   pl.BlockSpec((8, LANES), lambda i: (0, 0))],
        out_shape=[jax.ShapeDtypeStruct((T, LANES), jnp.int32),
                   jax.ShapeDtypeStruct((8, LANES), F32)],
        scratch_shapes=[pltpu.VMEM((8, LANES), F32)],
        compiler_params=_params(("arbitrary",), 16),
        name="assign_rank",
    )(eid)


def _dispatch_plan(eid_full, n_experts):
    T = eid_full.shape[0]
    A = T * TOP_K
    BM = MOE_BLOCK
    rank_full, cnt = _rank_call(eid_full)
    eid = eid_full[:, :TOP_K]
    counts = cnt[0, :n_experts].astype(jnp.int32)
    padded = (counts + BM - 1) // BM * BM
    pad_end = jnp.cumsum(padded)
    pad_start = pad_end - padded
    onehot = eid[:, :, None] == jnp.arange(n_experts, dtype=jnp.int32)[None, None, :]
    dest = (jnp.sum(jnp.where(onehot, pad_start[None, None, :], 0), axis=-1)
            + rank_full[:, :TOP_K]).reshape(A)
    nb = (A + n_experts * (BM - 1) + BM - 1) // BM
    block_start = jnp.arange(nb, dtype=jnp.int32) * BM
    block_e = jnp.minimum(jnp.sum(block_start[:, None] >= pad_end[None, :], axis=1),
                          n_experts - 1).astype(jnp.int32)
    n_act = (pad_end[-1] // BM).astype(jnp.int32).reshape(1)
    last_block = jnp.maximum(pad_end // BM - 1, 0).astype(jnp.int32)
    ex = jnp.arange(n_experts, dtype=jnp.int32)
    later = (ex[None, :] > ex[:, None]) & (counts[None, :] > 0)
    next_of = jnp.min(jnp.where(later, ex[None, :], n_experts), axis=1)
    next_of = jnp.where(next_of < n_experts, next_of, -1)
    next_e = jnp.sum(jnp.where(block_e[:, None] == ex[None, :], next_of[None, :], 0),
                     axis=1).astype(jnp.int32)
    return dest.astype(jnp.int32), block_e, n_act, next_e, last_block, nb * BM


def kernel(x, c, ada_w, ada_b, norm1_g, w_in, gm_vn_g, gm_vn_b, gm_ws, gm_bs, gla_wa2, gla_ba,
           gla_on_g, w_pa, w_pb, w_out, norm2_g, w_rg, b_rg, w_re, b_re, w_e_gate, w_e_up,
           w_e_down, final_g):
    B, S, D = x.shape
    T = B * S
    L = ada_w.shape[0]
    W = gm_vn_g.shape[1]
    G, C = gm_ws.shape[1], gm_ws.shape[2]
    RANK, DK = gla_wa2.shape[1], gla_wa2.shape[2]
    DV = gla_on_g.shape[1]
    E = w_e_gate.shape[1]
    per_group = E // MOE_GROUPS
    SUB = D // (2 * LANES)
    assert B <= 8 and MOE_GROUPS + E <= LANES and RANK <= LANES

    c8 = jnp.zeros((8, D), F32).at[:B].set(c)
    mod = _ada_call(c8, ada_w, ada_b.reshape(L, 1, 6 * D))

    o_alr = 2 * W + 2 * DK + 2 * DV
    x2 = x.reshape(T, D)
    w_in_b = w_in.astype(BF16)
    w_pa_b, w_pb_b, w_out_b = w_pa.astype(BF16), w_pb.astype(BF16), w_out.astype(BF16)
    for l in range(L):
        sh1, sc1, g1, sh2, sc2, g2 = [mod[l, :B, k * D:(k + 1) * D].reshape(B, 1, D) for k in range(6)]
        w_tail = w_in_b[l, :, o_alr + RANK:]
        w_alr = jnp.zeros((D, LANES), BF16).at[:, :RANK].set(w_in_b[l, :, o_alr:o_alr + RANK])
        wa2p = jnp.zeros((LANES, DK), BF16).at[:RANK].set(gla_wa2[l].astype(BF16))
        bias_full = jnp.repeat(gm_bs[l].T, W // G, axis=1)
        wr = jnp.zeros((D, LANES), BF16).at[:, :MOE_GROUPS + E].set(
            jnp.concatenate([w_rg[l], w_re[l]], axis=1).astype(BF16))
        br = jnp.zeros((1, LANES), F32).at[0, :MOE_GROUPS + E].set(
            jnp.concatenate([b_rg[l], b_re[l]]))

        proj, alr = _inproj_call(x2, norm1_g[l].reshape(1, D), sc1, sh1, w_in_b, w_tail, w_alr, S, l,
                                 o_alr)
        ya = _gmlp_call(proj, gm_vn_g[l].reshape(1, W), gm_vn_b[l].reshape(1, W), gm_ws[l], bias_full)
        yb = _gla_call(proj, alr, wa2p, gla_ba[l].reshape(1, DK), gla_on_g[l].reshape(1, DV), B, S, W)
        x1, h2, logits = _merge_call(ya, yb, proj, w_pa_b, w_pb_b, w_out_b, x2, g1,
                                     norm2_g[l].reshape(1, D), sc2, sh2, wr, S, W, l)
        eid, wts = _router_call(logits, br, per_group)
        dest, block_e, n_act, next_e, last_block, n_rows = _dispatch_plan(eid, E)
        xs = _dispatch_call(h2, dest, last_block, n_act, n_rows, SUB)
        ys = _experts_call(xs, block_e, n_act, next_e, w_e_gate, w_e_up, w_e_down, l)
        x2 = _combine_call(x1, ys, dest, wts, g2, final_g.reshape(1, D), S, final_norm=(l == L - 1))
    return x2.reshape(B, S, D)
```

```python
import functools

import jax
import jax.numpy as jnp
from jax import lax
from jax.experimental import pallas as pl
from jax.experimental.pallas import tpu as pltpu

F32 = jnp.float32
BF16 = jnp.bfloat16

EPS = 1e-6
GLA_HEADS = 4
GLA_TAU = 16.0
GLA_CHUNK = 64
MOE_GROUPS = 4
TOP_K = 2

LANES = 128
MIB = 1024 * 1024
MOE_BLOCK = 256
ISSUE_UNROLL = 8

NT_DIMS = (((1,), (1,)), ((), ()))
TN_DIMS = (((0,), (0,)), ((), ()))


def _params(semantics, vmem_mib):
    return pltpu.CompilerParams(dimension_semantics=semantics,
                                vmem_limit_bytes=int(vmem_mib * MIB))


def _dot(a, b):
    return jnp.dot(a, b, preferred_element_type=F32)


def _sigmoid(x):
    return 1.0 / (1.0 + jnp.exp(-x))


def _gelu(x):
    return 0.5 * x * (1.0 + lax.erf(x * (2.0 ** -0.5)))


def _pack_pairs(lo, hi):
    lo_w = pltpu.bitcast(lo.astype(BF16).astype(F32), jnp.uint32) >> 16
    hi_w = pltpu.bitcast(hi.astype(BF16).astype(F32), jnp.uint32) & jnp.uint32(0xFFFF0000)
    return lo_w | hi_w


def _unpack_pairs(w):
    lo = pltpu.bitcast(w << 16, F32)
    hi = pltpu.bitcast(w & jnp.uint32(0xFFFF0000), F32)
    return lo, hi


def _store_packed(ref, x):
    rows, d = x.shape
    half = d // 2
    n_s = half // LANES
    for s in range(n_s):
        lo = x[:, s * LANES:(s + 1) * LANES]
        hi = x[:, half + s * LANES:half + (s + 1) * LANES]
        ref[pl.ds(s, rows, stride=n_s), :] = _pack_pairs(lo, hi)


def _load_packed(ref, rows, n_s, offset=0, group=1):
    los, his = [], []
    for s in range(n_s):
        lo, hi = _unpack_pairs(ref[pl.ds(offset * n_s + s, rows, stride=n_s * group), :])
        los.append(lo)
        his.append(hi)
    return los, his


def _ada_kernel(c_ref, w_ref, b_ref, o_ref):
    c = c_ref[...]
    cond = c * _sigmoid(c)
    o_ref[0] = _dot(cond.astype(BF16), w_ref[0].astype(BF16)) + b_ref[0]


def _ada_call(c8, ada_w, ada_b3):
    L, D, N = ada_w.shape
    tn = min(1024, N)
    return pl.pallas_call(
        _ada_kernel,
        grid=(L, N // tn),
        in_specs=[pl.BlockSpec((8, D), lambda l, j: (0, 0)),
                  pl.BlockSpec((1, D, tn), lambda l, j: (l, 0, j)),
                  pl.BlockSpec((1, 1, tn), lambda l, j: (l, 0, j))],
        out_specs=pl.BlockSpec((1, 8, tn), lambda l, j: (l, 0, j)),
        out_shape=jax.ShapeDtypeStruct((L, 8, N), F32),
        compiler_params=_params(("parallel", "parallel"), 40),
        name="ada_mod",
    )(c8, ada_w, ada_b3)


def _inproj_kernel(x_ref, g_ref, sc_ref, sh_ref, w_ref, walr_ref, o_ref, alr_ref, h_scr):
    @pl.when(pl.program_id(1) == 0)
    def _():
        x = x_ref[...]
        ms = jnp.mean(x * x, axis=-1, keepdims=True)
        y = x * lax.rsqrt(ms + EPS) * g_ref[...]
        h = (y * (1.0 + sc_ref[0]) + sh_ref[0]).astype(BF16)
        h_scr[...] = h
        alr_ref[...] = _dot(h, walr_ref[0])

    o_ref[...] = _dot(h_scr[...], w_ref[0]).astype(o_ref.dtype)


def _inproj_call(x2, g, sc, sh, w_main, w_alr, S, l):
    T, D = x2.shape
    NM = w_main.shape[2]
    tm = min(1024, S)
    tn = min(1024, NM)
    per_b = S // tm
    return pl.pallas_call(
        _inproj_kernel,
        grid=(T // tm, NM // tn),
        in_specs=[pl.BlockSpec((tm, D), lambda i, j: (i, 0)),
                  pl.BlockSpec((1, D), lambda i, j: (0, 0)),
                  pl.BlockSpec((1, 1, D), lambda i, j: (i // per_b, 0, 0)),
                  pl.BlockSpec((1, 1, D), lambda i, j: (i // per_b, 0, 0)),
                  pl.BlockSpec((1, D, tn), lambda i, j: (l, 0, j)),
                  pl.BlockSpec((1, D, LANES), lambda i, j: (l, 0, 0))],
        out_specs=[pl.BlockSpec((tm, tn), lambda i, j: (i, j)),
                   pl.BlockSpec((tm, LANES), lambda i, j: (i, 0))],
        out_shape=[jax.ShapeDtypeStruct((T, NM), BF16),
                   jax.ShapeDtypeStruct((T, LANES), F32)],
        scratch_shapes=[pltpu.VMEM((tm, D), BF16)],
        compiler_params=_params(("parallel", "arbitrary"), 56),
        name="in_proj",
    )(x2, g, sc, sh, w_main, w_alr)


def _gmlp_kernel(u_ref, v_ref, g_ref, b_ref, ws_ref, bias_ref, o_ref, vn_scr, *, groups, chunk):
    rows, width = v_ref.shape
    gc = width // groups
    r = lax.broadcasted_iota(jnp.int32, (chunk, chunk), 0)
    c = lax.broadcasted_iota(jnp.int32, (chunk, chunk), 1)
    causal = r >= c
    for ci in range(rows // chunk):
        rs = slice(ci * chunk, (ci + 1) * chunk)
        gv = _gelu(v_ref[rs, :].astype(F32))
        mu = jnp.mean(gv, axis=-1, keepdims=True)
        xc = gv - mu
        var = jnp.mean(xc * xc, axis=-1, keepdims=True)
        vn = xc * lax.rsqrt(var + EPS) * g_ref[...] + b_ref[...]
        vn_scr[...] = vn.astype(BF16)
        for gi in range(groups):
            cs = slice(gi * gc, (gi + 1) * gc)
            wm = jnp.where(causal, ws_ref[gi], 0.0).astype(BF16)
            mixed = _dot(wm, vn_scr[:, cs]) + bias_ref[:, cs]
            gu = _gelu(u_ref[rs, cs].astype(F32))
            o_ref[rs, cs] = (gu * mixed).astype(o_ref.dtype)


def _gmlp_call(proj, vn_g, vn_b, ws, bias_full):
    T = proj.shape[0]
    G, C, _ = ws.shape
    W = vn_g.shape[1]
    R = min(2 * C, T)
    kern = functools.partial(_gmlp_kernel, groups=G, chunk=C)
    return pl.pallas_call(
        kern,
        grid=(T // R,),
        in_specs=[pl.BlockSpec((R, W), lambda i: (i, 0)),
                  pl.BlockSpec((R, W), lambda i: (i, 1)),
                  pl.BlockSpec((1, W), lambda i: (0, 0)),
                  pl.BlockSpec((1, W), lambda i: (0, 0)),
                  pl.BlockSpec((G, C, C), lambda i: (0, 0, 0)),
                  pl.BlockSpec((C, W), lambda i: (0, 0))],
        out_specs=pl.BlockSpec((R, W), lambda i: (i, 0)),
        out_shape=jax.ShapeDtypeStruct((T, W), BF16),
        scratch_shapes=[pltpu.VMEM((C, W), BF16)],
        compiler_params=_params(("parallel",), 32),
        name="gmlp_gate",
    )(proj, proj, vn_g, vn_b, ws, bias_full)


def _gla_kernel(q_ref, k_ref, v_ref, r_ref, alr_ref, wa2_ref, ba_ref, g_ref, o_ref,
                st_ref, la_ref, *, heads):
    rows, dk = q_ref.shape
    dv = v_ref.shape[1]
    hk = dk // heads
    hv = dv // heads
    C = GLA_CHUNK
    scale = hk ** -0.5

    @pl.when(pl.program_id(1) == 0)
    def _():
        st_ref[...] = jnp.zeros(st_ref.shape, F32)

    z = _dot(alr_ref[...].astype(BF16), wa2_ref[...]) + ba_ref[...]
    la_ref[...] = (jnp.minimum(z, 0.0) - jnp.log(1.0 + jnp.exp(-jnp.abs(z)))) * (1.0 / GLA_TAU)

    ri = lax.broadcasted_iota(jnp.int32, (C, C), 0)
    ci = lax.broadcasted_iota(jnp.int32, (C, C), 1)
    causal = ri >= ci
    tri = causal.astype(BF16)

    def chunk_step(c, carry):
        rs = pl.ds(pl.multiple_of(c * C, C), C)
        la = la_ref[rs, :]
        hi = la.astype(BF16)
        r1 = la - hi.astype(F32)
        mid = r1.astype(BF16)
        lo = (r1 - mid.astype(F32)).astype(BF16)
        cum = _dot(tri, hi) + _dot(tri, mid) + _dot(tri, lo)
        cl = cum[C - 1:C, :]
        q = q_ref[rs, :].astype(F32) * scale
        k = k_ref[rs, :].astype(F32)
        qd = (q * jnp.exp(cum)).astype(BF16)
        ki = (k * jnp.exp(-cum)).astype(BF16)
        ke = (k * jnp.exp(cl - cum)).astype(BF16)
        dec = jnp.exp(cl)
        for h in range(heads):
            ks = slice(h * hk, (h + 1) * hk)
            vs = slice(h * hv, (h + 1) * hv)
            vh = v_ref[rs, vs]
            s = lax.dot_general(qd[:, ks], ki[:, ks], NT_DIMS, preferred_element_type=F32)
            s = jnp.where(causal, s, 0.0).astype(BF16)
            st = st_ref[h]
            o = _dot(s, vh) + lax.dot_general(qd[:, ks], st.astype(BF16), NT_DIMS,
                                              preferred_element_type=F32)
            st_ref[h] = st * dec[:, ks] + lax.dot_general(vh, ke[:, ks], TN_DIMS,
                                                          preferred_element_type=F32)
            ms = jnp.mean(o * o, axis=-1, keepdims=True)
            on = o * lax.rsqrt(ms + EPS) * g_ref[:, vs]
            rr = r_ref[rs, vs].astype(F32)
            o_ref[rs, vs] = (on * (rr * _sigmoid(rr))).astype(o_ref.dtype)
        return carry

    lax.fori_loop(0, rows // C, chunk_step, 0)


def _gla_call(proj, alr, wa2p, ba, on_g, B, S, W):
    DK = wa2p.shape[1]
    DV = on_g.shape[1]
    H = GLA_HEADS
    Cb = min(256, S)
    nb = S // Cb
    q_blk = (2 * W) // DK
    k_blk = (2 * W + DK) // DK
    v_blk = (2 * W + 2 * DK) // DV
    r_blk = (2 * W + 2 * DK + DV) // DV
    kern = functools.partial(_gla_kernel, heads=H)
    return pl.pallas_call(
        kern,
        grid=(B, nb),
        in_specs=[pl.BlockSpec((Cb, DK), lambda b, i: (b * nb + i, q_blk)),
                  pl.BlockSpec((Cb, DK), lambda b, i: (b * nb + i, k_blk)),
                  pl.BlockSpec((Cb, DV), lambda b, i: (b * nb + i, v_blk)),
                  pl.BlockSpec((Cb, DV), lambda b, i: (b * nb + i, r_blk)),
                  pl.BlockSpec((Cb, LANES), lambda b, i: (b * nb + i, 0)),
                  pl.BlockSpec((LANES, DK), lambda b, i: (0, 0)),
                  pl.BlockSpec((1, DK), lambda b, i: (0, 0)),
                  pl.BlockSpec((1, DV), lambda b, i: (0, 0))],
        out_specs=pl.BlockSpec((Cb, DV), lambda b, i: (b * nb + i, 0)),
        out_shape=jax.ShapeDtypeStruct((B * S, DV), BF16),
        scratch_shapes=[pltpu.VMEM((H, DV // H, DK // H), F32),
                        pltpu.VMEM((Cb, DK), F32)],
        compiler_params=_params(("parallel", "arbitrary"), 32),
        name="gla",
    )(proj, proj, proj, proj, alr, wa2p, ba, on_g)


def _merge_kernel(ya_ref, yb_ref, ga_ref, gb_ref, wpa_ref, wpb_ref, wout_ref, x_ref, g1_ref,
                  n2_ref, sc_ref, sh_ref, wr_ref, x1_ref, h2_ref, lg_ref):
    a = _dot(ya_ref[...], wpa_ref[0])
    b = _dot(yb_ref[...], wpb_ref[0])
    y = _sigmoid(ga_ref[...].astype(F32)) * a + _sigmoid(gb_ref[...].astype(F32)) * b
    out = _dot(y.astype(BF16), wout_ref[0])
    x1 = x_ref[...] + g1_ref[0] * out
    x1_ref[...] = x1
    ms = jnp.mean(x1 * x1, axis=-1, keepdims=True)
    h2 = (x1 * lax.rsqrt(ms + EPS) * n2_ref[...]) * (1.0 + sc_ref[0]) + sh_ref[0]
    _store_packed(h2_ref, h2)
    lg_ref[...] = _dot(h2.astype(BF16), wr_ref[...])


def _merge_call(ya, yb, proj, wpa, wpb, wout, x2, g1, n2g, sc2, sh2, wr, S, W, l):
    T, D = x2.shape
    ga_blk = (proj.shape[1] - 2 * D) // D
    gb_blk = ga_blk + 1
    tm = min(256, S)
    per_b = S // tm
    n_s = D // (2 * LANES)
    const = dict(pipeline_mode=pl.Buffered(1))
    return pl.pallas_call(
        _merge_kernel,
        grid=(T // tm,),
        in_specs=[pl.BlockSpec((tm, W), lambda i: (i, 0)),
                  pl.BlockSpec((tm, W), lambda i: (i, 0)),
                  pl.BlockSpec((tm, D), lambda i: (i, ga_blk)),
                  pl.BlockSpec((tm, D), lambda i: (i, gb_blk)),
                  pl.BlockSpec((1, W, D), lambda i: (l, 0, 0), **const),
                  pl.BlockSpec((1, W, D), lambda i: (l, 0, 0), **const),
                  pl.BlockSpec((1, D, D), lambda i: (l, 0, 0), **const),
                  pl.BlockSpec((tm, D), lambda i: (i, 0)),
                  pl.BlockSpec((1, 1, D), lambda i: (i // per_b, 0, 0)),
                  pl.BlockSpec((1, D), lambda i: (0, 0)),
                  pl.BlockSpec((1, 1, D), lambda i: (i // per_b, 0, 0)),
                  pl.BlockSpec((1, 1, D), lambda i: (i // per_b, 0, 0)),
                  pl.BlockSpec((D, LANES), lambda i: (0, 0), **const)],
        out_specs=[pl.BlockSpec((tm, D), lambda i: (i, 0)),
                   pl.BlockSpec((tm * n_s, LANES), lambda i: (i, 0)),
                   pl.BlockSpec((tm, LANES), lambda i: (i, 0))],
        out_shape=[jax.ShapeDtypeStruct((T, D), F32),
                   jax.ShapeDtypeStruct((T * n_s, LANES), jnp.uint32),
                   jax.ShapeDtypeStruct((T, LANES), F32)],
        compiler_params=_params(("parallel",), 56),
        name="merge_out",
    )(ya, yb, proj, proj, wpa, wpb, wout, x2, g1, n2g, sc2, sh2, wr)


def _router_kernel(lg_ref, b_ref, e_ref, w_ref, *, groups, per_group):
    lg = lg_ref[...] + b_ref[...]
    lane = lax.broadcasted_iota(jnp.int32, lg.shape, 1)
    lane_f = lane.astype(F32)
    neg = jnp.float32(-1e30)
    big = jnp.float32(LANES)

    def first_argmax(vals):
        m = jnp.max(vals, axis=-1, keepdims=True)
        idx = jnp.min(jnp.where(vals == m, lane_f, big), axis=-1, keepdims=True)
        return m, idx

    gmask = lane < groups
    gl = jnp.where(gmask, lg, neg)
    gmax, gidx = first_argmax(gl)
    gsum = jnp.sum(jnp.where(gmask, jnp.exp(gl - gmax), 0.0), axis=-1, keepdims=True)
    gtop = 1.0 / gsum

    lo = groups + gidx * per_group
    emask = (lane_f >= lo) & (lane_f < lo + per_group)
    el = jnp.where(emask, lg, neg)
    m1, i1 = first_argmax(el)
    el2 = jnp.where(lane_f == i1, neg, el)
    m2, i2 = first_argmax(el2)
    den = jnp.sum(jnp.where(emask, jnp.exp(el - m1), 0.0), axis=-1, keepdims=True)
    p1 = 1.0 / den
    p2 = jnp.exp(m2 - m1) / den
    ps = p1 + p2
    w1 = gtop * p1 / ps
    w2 = gtop * p2 / ps
    e1 = (i1 - groups).astype(jnp.int32)
    e2 = (i2 - groups).astype(jnp.int32)
    e_ref[...] = jnp.where(lane == 0, e1, jnp.where(lane == 1, e2, 0))
    w_ref[...] = jnp.where(lane == 0, w1, jnp.where(lane == 1, w2, 0.0))


def _router_call(logits, bias, per_group):
    T = logits.shape[0]
    tm = min(1024, T)
    kern = functools.partial(_router_kernel, groups=MOE_GROUPS, per_group=per_group)
    return pl.pallas_call(
        kern,
        grid=(T // tm,),
        in_specs=[pl.BlockSpec((tm, LANES), lambda i: (i, 0)),
                  pl.BlockSpec((1, LANES), lambda i: (0, 0))],
        out_specs=[pl.BlockSpec((tm, LANES), lambda i: (i, 0)),
                   pl.BlockSpec((tm, LANES), lambda i: (i, 0))],
        out_shape=[jax.ShapeDtypeStruct((T, LANES), jnp.int32),
                   jax.ShapeDtypeStruct((T, LANES), F32)],
        compiler_params=_params(("parallel",), 16),
        name="router_topk",
    )(logits, bias)


def _row(ref, t, n_s):
    return ref.at[pl.ds(pl.multiple_of(t * n_s, n_s), n_s)]


def _dispatch_kernel(zr_ref, na_ref, dest_ref, h_ref, out_ref, zbuf, sem, zsem, *, n_s, block):
    n_assign = dest_ref.shape[2]
    tm = n_assign // TOP_K
    nb = out_ref.shape[0] // (block * n_s)

    def zero_block(b):
        start = pl.multiple_of(b * (block * n_s), block * n_s)
        return pltpu.make_async_copy(zbuf, out_ref.at[pl.ds(start, block * n_s)], zsem.at[0])

    @pl.when(pl.program_id(0) == 0)
    def _():
        zbuf[...] = jnp.zeros(zbuf.shape, zbuf.dtype)
        n_exp = zr_ref.shape[0]
        lax.fori_loop(0, n_exp, lambda e, c: (zero_block(zr_ref[e]).start(), c)[1], 0)
        lax.fori_loop(na_ref[0], nb, lambda b, c: (zero_block(b).start(), c)[1], 0)
        lax.fori_loop(0, n_exp, lambda e, c: (zero_block(0).wait(), c)[1], 0)
        lax.fori_loop(na_ref[0], nb, lambda b, c: (zero_block(0).wait(), c)[1], 0)

    tok_unroll = ISSUE_UNROLL // TOP_K

    def issue(g, c):
        for u in range(tok_unroll):
            t = g * tok_unroll + u
            for k in range(TOP_K):
                pltpu.make_async_copy(_row(h_ref, t, n_s),
                                      _row(out_ref, dest_ref[0, 0, t * TOP_K + k], n_s),
                                      sem.at[0]).start(priority=k % 2)
        return c
    lax.fori_loop(0, tm // tok_unroll, issue, 0)
    for _ in range(TOP_K):
        pltpu.make_async_copy(h_ref, out_ref.at[pl.ds(0, tm * n_s)], sem.at[0]).wait()


def _dispatch_call(h2p, dest, zero_rows, n_act, n_rows, n_s):
    T = h2p.shape[0] // n_s
    tm = min(1024, T)
    nt = T // tm
    kern = functools.partial(_dispatch_kernel, n_s=n_s, block=MOE_BLOCK)
    grid_spec = pltpu.PrefetchScalarGridSpec(
        num_scalar_prefetch=2,
        grid=(nt,),
        in_specs=[pl.BlockSpec((1, 1, tm * TOP_K), lambda i, zr, na: (i, 0, 0),
                               memory_space=pltpu.SMEM),
                  pl.BlockSpec((tm * n_s, LANES), lambda i, zr, na: (i, 0))],
        out_specs=pl.BlockSpec(memory_space=pl.ANY),
        scratch_shapes=[pltpu.VMEM((MOE_BLOCK * n_s, LANES), h2p.dtype),
                        pltpu.SemaphoreType.DMA((1,)),
                        pltpu.SemaphoreType.DMA((1,))],
    )
    return pl.pallas_call(
        kern,
        grid_spec=grid_spec,
        out_shape=jax.ShapeDtypeStruct((n_rows * n_s, LANES), h2p.dtype),
        compiler_params=_params(("arbitrary",), 16),
        name="row_dispatch",
    )(zero_rows, n_act, dest.reshape(nt, 1, tm * TOP_K), h2p)


def _stage_expert_weights(layer, be_ref, na_ref, nx_ref, i, hbm_refs, stage, bf_refs, sem):
    e = be_ref[i]
    first = (i == 0) | (e != be_ref[jnp.maximum(i - 1, 0)])

    def copies(expert):
        return [pltpu.make_async_copy(w.at[layer, expert], stage[k], sem.at[k])
                for k, w in enumerate(hbm_refs)]

    @pl.when(i == 0)
    def _():
        for cp in copies(e):
            cp.start()

    @pl.when((i < na_ref[0]) & first)
    def _():
        for k, cp in enumerate(copies(e)):
            cp.wait()
            bf_refs[k][...] = stage[k][...].astype(BF16)
        nxt = nx_ref[i]

        @pl.when(nxt >= 0)
        def _():
            for cp in copies(nxt):
                cp.start()


def _expert_kernel(be_ref, na_ref, nx_ref, xs_ref, wg_hbm, wu_hbm, wd_hbm, ys_ref,
                   sg, su, sd, wgb, wub, wdb, sem, *, layer):
    i = pl.program_id(0)
    active = i < na_ref[0]
    _stage_expert_weights(layer, be_ref, na_ref, nx_ref, i, (wg_hbm, wu_hbm, wd_hbm),
                          (sg, su, sd), (wgb, wub, wdb), sem)

    @pl.when(active)
    def _():
        n_s = sd.shape[1] // (2 * LANES)
        rows = xs_ref.shape[0] // n_s
        los, his = _load_packed(xs_ref, rows, n_s)
        x = jnp.concatenate([p.astype(BF16) for p in los + his], axis=1)
        a = _dot(x, wgb[...])
        b = _dot(x, wub[...])
        hid = (a * _sigmoid(a) * b).astype(BF16)
        _store_packed(ys_ref, _dot(hid, wdb[...]))

    @pl.when(jnp.logical_not(active))
    def _():
        ys_ref[...] = jnp.zeros(ys_ref.shape, ys_ref.dtype)


def _experts_call(xs, block_e, n_act, next_e, w_gate, w_up, w_down, l):
    D, DE = w_gate.shape[-2], w_gate.shape[-1]
    n_s = D // (2 * LANES)
    n_rows = xs.shape[0] // n_s
    BM = MOE_BLOCK
    nb = n_rows // BM
    any_spec = pl.BlockSpec(memory_space=pl.ANY)
    grid_spec = pltpu.PrefetchScalarGridSpec(
        num_scalar_prefetch=3,
        grid=(nb,),
        in_specs=[pl.BlockSpec((BM * n_s, LANES),
                               lambda i, be, na, nx: (jnp.minimum(i, na[0] - 1), 0)),
                  any_spec, any_spec, any_spec],
        out_specs=pl.BlockSpec((BM * n_s, LANES), lambda i, be, na, nx: (i, 0)),
        scratch_shapes=[pltpu.VMEM((D, DE), F32), pltpu.VMEM((D, DE), F32), pltpu.VMEM((DE, D), F32),
                        pltpu.VMEM((D, DE), BF16), pltpu.VMEM((D, DE), BF16),
                        pltpu.VMEM((DE, D), BF16),
                        pltpu.SemaphoreType.DMA((3,))],
    )
    return pl.pallas_call(
        functools.partial(_expert_kernel, layer=l),
        grid_spec=grid_spec,
        out_shape=jax.ShapeDtypeStruct((n_rows * n_s, LANES), jnp.uint32),
        compiler_params=_params(("arbitrary",), 56),
        name="expert_mlp",
    )(block_e, n_act, next_e, xs, w_gate, w_up, w_down)


def _combine_kernel(cur_ref, nxt_ref, x_ref, ys_ref, w_ref, g2_ref, fg_ref, o_ref, ybuf, sem,
                    *, final_norm):
    i = pl.program_id(0)
    n = pl.num_programs(0)
    tm, d = x_ref.shape
    half = d // 2
    n_s = half // LANES
    n_assign = tm * TOP_K
    slot = i % 2

    def gather(idx_ref, s):
        def issue(g, c):
            for u in range(ISSUE_UNROLL):
                r = g * ISSUE_UNROLL + u
                pltpu.make_async_copy(_row(ys_ref, idx_ref[0, 0, r], n_s), _row(ybuf.at[s], r, n_s),
                                      sem.at[s]).start(priority=u % 2)
            return c
        lax.fori_loop(0, n_assign // ISSUE_UNROLL, issue, 0)

    @pl.when(i == 0)
    def _():
        gather(cur_ref, 0)

    @pl.when(i + 1 < n)
    def _():
        gather(nxt_ref, 1 - slot)

    pltpu.make_async_copy(ys_ref.at[pl.ds(0, n_assign * n_s)], ybuf.at[slot], sem.at[slot]).wait()

    yg_ref = ybuf.at[slot]
    w = w_ref[...]
    w0 = jnp.broadcast_to(w[:, 0:1], (tm, LANES))
    w1 = jnp.broadcast_to(w[:, 1:2], (tm, LANES))
    lo0, hi0 = _load_packed(yg_ref, tm, n_s, offset=0, group=TOP_K)
    lo1, hi1 = _load_packed(yg_ref, tm, n_s, offset=1, group=TOP_K)
    ss = jnp.zeros((tm, 1), F32)
    for s in range(n_s):
        for base, p0, p1 in ((0, lo0[s], lo1[s]), (half, hi0[s], hi1[s])):
            cs = slice(base + s * LANES, base + (s + 1) * LANES)
            xv = x_ref[:, cs] + g2_ref[0, :, cs] * (w0 * p0 + w1 * p1)
            o_ref[:, cs] = xv
            if final_norm:
                ss = ss + jnp.sum(xv * xv, axis=-1, keepdims=True)
    if final_norm:
        o_ref[...] = o_ref[...] * lax.rsqrt(ss * (1.0 / d) + EPS) * fg_ref[...]


def _combine_call(x1, ys, dest, wts, g2, final_g, S, final_norm):
    T, D = x1.shape
    tm = min(512, S)
    nt = T // tm
    per_b = S // tm
    n_s = D // (2 * LANES)
    dest3 = dest.reshape(nt, 1, tm * TOP_K)
    kern = functools.partial(_combine_kernel, final_norm=final_norm)
    return pl.pallas_call(
        kern,
        grid=(nt,),
        in_specs=[pl.BlockSpec((1, 1, tm * TOP_K), lambda i: (i, 0, 0), memory_space=pltpu.SMEM),
                  pl.BlockSpec((1, 1, tm * TOP_K), lambda i: (jnp.minimum(i + 1, nt - 1), 0, 0),
                               memory_space=pltpu.SMEM),
                  pl.BlockSpec((tm, D), lambda i: (i, 0)),
                  pl.BlockSpec(memory_space=pl.ANY),
                  pl.BlockSpec((tm, LANES), lambda i: (i, 0)),
                  pl.BlockSpec((1, 1, D), lambda i: (i // per_b, 0, 0)),
                  pl.BlockSpec((1, D), lambda i: (0, 0))],
        out_specs=pl.BlockSpec((tm, D), lambda i: (i, 0)),
        out_shape=jax.ShapeDtypeStruct((T, D), F32),
        scratch_shapes=[pltpu.VMEM((2, tm * TOP_K * n_s, LANES), jnp.uint32),
                        pltpu.SemaphoreType.DMA((2,))],
        compiler_params=_params(("arbitrary",), 48),
        name="moe_combine",
    )(dest3, dest3, x1, ys, wts, g2, final_g)


def _rank_kernel(e_ref, rank_ref, cnt_ref, carry):
    @pl.when(pl.program_id(0) == 0)
    def _():
        carry[...] = jnp.zeros(carry.shape, F32)

    e = e_ref[...]
    tb = e.shape[0]
    lane = lax.broadcasted_iota(jnp.int32, e.shape, 1)
    oh0 = lane == e[:, 0:1]
    oh1 = lane == e[:, 1:2]
    both = oh0.astype(F32) + oh1.astype(F32)
    r = lax.broadcasted_iota(jnp.int32, (tb, tb), 0)
    c = lax.broadcasted_iota(jnp.int32, (tb, tb), 1)
    strict = (r > c).astype(BF16)
    before = _dot(strict, both.astype(BF16)) + carry[0:1, :]
    r0 = jnp.sum(jnp.where(oh0, before, 0.0), axis=-1, keepdims=True)
    r1 = jnp.sum(jnp.where(oh1, before, 0.0), axis=-1, keepdims=True)
    rank_ref[...] = jnp.where(lane == 0, r0, jnp.where(lane == 1, r1, 0.0)).astype(jnp.int32)
    carry[...] = carry[...] + jnp.sum(both, axis=0, keepdims=True)
    cnt_ref[...] = carry[...]


def _rank_call(eid):
    T = eid.shape[0]
    tb = min(512, T)
    return pl.pallas_call(
        _rank_kernel,
        grid=(T // tb,),
        in_specs=[pl.BlockSpec((tb, LANES), lambda i: (i, 0))],
        out_specs=[pl.BlockSpec((tb, LANES), lambda i: (i, 0)),
                   pl.BlockSpec((8, LANES), lambda i: (0, 0))],
        out_shape=[jax.ShapeDtypeStruct((T, LANES), jnp.int32),
                   jax.ShapeDtypeStruct((8, LANES), F32)],
        scratch_shapes=[pltpu.VMEM((8, LANES), F32)],
        compiler_params=_params(("arbitrary",), 16),
        name="assign_rank",
    )(eid)


def _dispatch_plan(eid_full, n_experts):
    T = eid_full.shape[0]
    A = T * TOP_K
    BM = MOE_BLOCK
    rank_full, cnt = _rank_call(eid_full)
    eid = eid_full[:, :TOP_K]
    counts = cnt[0, :n_experts].astype(jnp.int32)
    padded = (counts + BM - 1) // BM * BM
    pad_end = jnp.cumsum(padded)
    pad_start = pad_end - padded
    onehot = eid[:, :, None] == jnp.arange(n_experts, dtype=jnp.int32)[None, None, :]
    dest = (jnp.sum(jnp.where(onehot, pad_start[None, None, :], 0), axis=-1)
            + rank_full[:, :TOP_K]).reshape(A)
    nb = (A + n_experts * (BM - 1) + BM - 1) // BM
    block_start = jnp.arange(nb, dtype=jnp.int32) * BM
    block_e = jnp.minimum(jnp.sum(block_start[:, None] >= pad_end[None, :], axis=1),
                          n_experts - 1).astype(jnp.int32)
    n_act = (pad_end[-1] // BM).astype(jnp.int32).reshape(1)
    last_block = jnp.maximum(pad_end // BM - 1, 0).astype(jnp.int32)
    ex = jnp.arange(n_experts, dtype=jnp.int32)
    later = (ex[None, :] > ex[:, None]) & (counts[None, :] > 0)
    next_of = jnp.min(jnp.where(later, ex[None, :], n_experts), axis=1)
    next_of = jnp.where(next_of < n_experts, next_of, -1)
    next_e = jnp.sum(jnp.where(block_e[:, None] == ex[None, :], next_of[None, :], 0),
                     axis=1).astype(jnp.int32)
    return dest.astype(jnp.int32), block_e, n_act, next_e, last_block, nb * BM


def kernel(x, c, ada_w, ada_b, norm1_g, w_in, gm_vn_g, gm_vn_b, gm_ws, gm_bs, gla_wa2, gla_ba,
           gla_on_g, w_pa, w_pb, w_out, norm2_g, w_rg, b_rg, w_re, b_re, w_e_gate, w_e_up,
           w_e_down, final_g):
    B, S, D = x.shape
    T = B * S
    L = ada_w.shape[0]
    W = gm_vn_g.shape[1]
    G, C = gm_ws.shape[1], gm_ws.shape[2]
    RANK, DK = gla_wa2.shape[1], gla_wa2.shape[2]
    DV = gla_on_g.shape[1]
    E = w_e_gate.shape[1]
    per_group = E // MOE_GROUPS
    SUB = D // (2 * LANES)
    assert B <= 8 and MOE_GROUPS + E <= LANES and RANK <= LANES

    c8 = jnp.zeros((8, D), F32).at[:B].set(c)
    mod = _ada_call(c8, ada_w, ada_b.reshape(L, 1, 6 * D))

    o_alr = 2 * W + 2 * DK + 2 * DV
    x2 = x.reshape(T, D)
    w_main = jnp.concatenate([w_in[:, :, :o_alr], w_in[:, :, o_alr + RANK:]], axis=2).astype(BF16)
    w_alr = jnp.zeros((L, D, LANES), BF16).at[:, :, :RANK].set(
        w_in[:, :, o_alr:o_alr + RANK].astype(BF16))
    w_pa_b, w_pb_b, w_out_b = w_pa.astype(BF16), w_pb.astype(BF16), w_out.astype(BF16)
    for l in range(L):
        sh1, sc1, g1, sh2, sc2, g2 = [mod[l, :B, k * D:(k + 1) * D].reshape(B, 1, D) for k in range(6)]
        wa2p = jnp.zeros((LANES, DK), BF16).at[:RANK].set(gla_wa2[l].astype(BF16))
        bias_full = jnp.repeat(gm_bs[l].T, W // G, axis=1)
        wr = jnp.zeros((D, LANES), BF16).at[:, :MOE_GROUPS + E].set(
            jnp.concatenate([w_rg[l], w_re[l]], axis=1).astype(BF16))
        br = jnp.zeros((1, LANES), F32).at[0, :MOE_GROUPS + E].set(
            jnp.concatenate([b_rg[l], b_re[l]]))

        proj, alr = _inproj_call(x2, norm1_g[l].reshape(1, D), sc1, sh1, w_main, w_alr, S, l)
        ya = _gmlp_call(proj, gm_vn_g[l].reshape(1, W), gm_vn_b[l].reshape(1, W), gm_ws[l], bias_full)
        yb = _gla_call(proj, alr, wa2p, gla_ba[l].reshape(1, DK), gla_on_g[l].reshape(1, DV), B, S, W)
        x1, h2, logits = _merge_call(ya, yb, proj, w_pa_b, w_pb_b, w_out_b, x2, g1,
                                     norm2_g[l].reshape(1, D), sc2, sh2, wr, S, W, l)
        eid, wts = _router_call(logits, br, per_group)
        dest, block_e, n_act, next_e, last_block, n_rows = _dispatch_plan(eid, E)
        xs = _dispatch_call(h2, dest, last_block, n_act, n_rows, SUB)
        ys = _experts_call(xs, block_e, n_act, next_e, w_e_gate, w_e_up, w_e_down, l)
        x2 = _combine_call(x1, ys, dest, wts, g2, final_g.reshape(1, D), S, final_norm=(l == L - 1))
    return x2.reshape(B, S, D)
```

```python
import functools

import jax
import jax.numpy as jnp
from jax import lax
from jax.experimental import pallas as pl
from jax.experimental.pallas import tpu as pltpu

F32 = jnp.float32
BF16 = jnp.bfloat16

EPS = 1e-6
GLA_HEADS = 4
GLA_TAU = 16.0
GLA_CHUNK = 64
MOE_GROUPS = 4
TOP_K = 2

LANES = 128
MIB = 1024 * 1024
MOE_BLOCK = 256
ISSUE_UNROLL = 8

NT_DIMS = (((1,), (1,)), ((), ()))
TN_DIMS = (((0,), (0,)), ((), ()))


def _params(semantics, vmem_mib):
    return pltpu.CompilerParams(dimension_semantics=semantics,
                                vmem_limit_bytes=int(vmem_mib * MIB))


def _dot(a, b):
    return jnp.dot(a, b, preferred_element_type=F32)


def _sigmoid(x):
    return 1.0 / (1.0 + jnp.exp(-x))


def _gelu(x):
    return 0.5 * x * (1.0 + lax.erf(x * (2.0 ** -0.5)))


def _pack_pairs(lo, hi):
    lo_w = pltpu.bitcast(lo.astype(BF16).astype(F32), jnp.uint32) >> 16
    hi_w = pltpu.bitcast(hi.astype(BF16).astype(F32), jnp.uint32) & jnp.uint32(0xFFFF0000)
    return lo_w | hi_w


def _unpack_pairs(w):
    lo = pltpu.bitcast(w << 16, F32)
    hi = pltpu.bitcast(w & jnp.uint32(0xFFFF0000), F32)
    return lo, hi


def _store_packed(ref, x):
    rows, d = x.shape
    half = d // 2
    n_s = half // LANES
    for s in range(n_s):
        lo = x[:, s * LANES:(s + 1) * LANES]
        hi = x[:, half + s * LANES:half + (s + 1) * LANES]
        ref[pl.ds(s, rows, stride=n_s), :] = _pack_pairs(lo, hi)


def _load_packed(ref, rows, n_s, offset=0, group=1):
    los, his = [], []
    for s in range(n_s):
        lo, hi = _unpack_pairs(ref[pl.ds(offset * n_s + s, rows, stride=n_s * group), :])
        los.append(lo)
        his.append(hi)
    return los, his


def _ada_kernel(c_ref, w_ref, b_ref, o_ref):
    c = c_ref[...]
    cond = c * _sigmoid(c)
    o_ref[0] = _dot(cond.astype(BF16), w_ref[0].astype(BF16)) + b_ref[0]


def _ada_call(c8, ada_w, ada_b3):
    L, D, N = ada_w.shape
    tn = min(1024, N)
    return pl.pallas_call(
        _ada_kernel,
        grid=(L, N // tn),
        in_specs=[pl.BlockSpec((8, D), lambda l, j: (0, 0)),
                  pl.BlockSpec((1, D, tn), lambda l, j: (l, 0, j)),
                  pl.BlockSpec((1, 1, tn), lambda l, j: (l, 0, j))],
        out_specs=pl.BlockSpec((1, 8, tn), lambda l, j: (l, 0, j)),
        out_shape=jax.ShapeDtypeStruct((L, 8, N), F32),
        compiler_params=_params(("parallel", "parallel"), 40),
        name="ada_mod",
    )(c8, ada_w, ada_b3)


def _inproj_kernel(x_ref, g_ref, sc_ref, sh_ref, w_ref, walr_ref, o_ref, alr_ref, h_scr):
    @pl.when(pl.program_id(1) == 0)
    def _():
        x = x_ref[...]
        ms = jnp.mean(x * x, axis=-1, keepdims=True)
        y = x * lax.rsqrt(ms + EPS) * g_ref[...]
        h = (y * (1.0 + sc_ref[0]) + sh_ref[0]).astype(BF16)
        h_scr[...] = h
        alr_ref[...] = _dot(h, walr_ref[0])

    o_ref[...] = _dot(h_scr[...], w_ref[0]).astype(o_ref.dtype)


def _inproj_call(x2, g, sc, sh, w_main, w_alr, S, l):
    T, D = x2.shape
    NM = w_main.shape[2]
    tm = min(1024, S)
    tn = min(1024, NM)
    per_b = S // tm
    return pl.pallas_call(
        _inproj_kernel,
        grid=(T // tm, NM // tn),
        in_specs=[pl.BlockSpec((tm, D), lambda i, j: (i, 0)),
                  pl.BlockSpec((1, D), lambda i, j: (0, 0)),
                  pl.BlockSpec((1, 1, D), lambda i, j: (i // per_b, 0, 0)),
                  pl.BlockSpec((1, 1, D), lambda i, j: (i // per_b, 0, 0)),
                  pl.BlockSpec((1, D, tn), lambda i, j: (l, 0, j)),
                  pl.BlockSpec((1, D, LANES), lambda i, j: (l, 0, 0))],
        out_specs=[pl.BlockSpec((tm, tn), lambda i, j: (i, j)),
                   pl.BlockSpec((tm, LANES), lambda i, j: (i, 0))],
        out_shape=[jax.ShapeDtypeStruct((T, NM), BF16),
                   jax.ShapeDtypeStruct((T, LANES), F32)],
        scratch_shapes=[pltpu.VMEM((tm, D), BF16)],
        compiler_params=_params(("parallel", "arbitrary"), 56),
        name="in_proj",
    )(x2, g, sc, sh, w_main, w_alr)


def _gmlp_kernel(u_ref, v_ref, g_ref, b_ref, ws_ref, bias_ref, o_ref, vn_scr, *, groups, chunk):
    rows, width = v_ref.shape
    gc = width // groups
    r = lax.broadcasted_iota(jnp.int32, (chunk, chunk), 0)
    c = lax.broadcasted_iota(jnp.int32, (chunk, chunk), 1)
    causal = r >= c
    for ci in range(rows // chunk):
        rs = slice(ci * chunk, (ci + 1) * chunk)
        gv = _gelu(v_ref[rs, :].astype(F32))
        mu = jnp.mean(gv, axis=-1, keepdims=True)
        xc = gv - mu
        var = jnp.mean(xc * xc, axis=-1, keepdims=True)
        vn = xc * lax.rsqrt(var + EPS) * g_ref[...] + b_ref[...]
        vn_scr[...] = vn.astype(BF16)
        for gi in range(groups):
            cs = slice(gi * gc, (gi + 1) * gc)
            wm = jnp.where(causal, ws_ref[gi], 0.0).astype(BF16)
            mixed = _dot(wm, vn_scr[:, cs]) + bias_ref[:, cs]
            gu = _gelu(u_ref[rs, cs].astype(F32))
            o_ref[rs, cs] = (gu * mixed).astype(o_ref.dtype)


def _gmlp_call(proj, vn_g, vn_b, ws, bias_full):
    T = proj.shape[0]
    G, C, _ = ws.shape
    W = vn_g.shape[1]
    R = min(2 * C, T)
    kern = functools.partial(_gmlp_kernel, groups=G, chunk=C)
    return pl.pallas_call(
        kern,
        grid=(T // R,),
        in_specs=[pl.BlockSpec((R, W), lambda i: (i, 0)),
                  pl.BlockSpec((R, W), lambda i: (i, 1)),
                  pl.BlockSpec((1, W), lambda i: (0, 0)),
                  pl.BlockSpec((1, W), lambda i: (0, 0)),
                  pl.BlockSpec((G, C, C), lambda i: (0, 0, 0)),
                  pl.BlockSpec((C, W), lambda i: (0, 0))],
        out_specs=pl.BlockSpec((R, W), lambda i: (i, 0)),
        out_shape=jax.ShapeDtypeStruct((T, W), BF16),
        scratch_shapes=[pltpu.VMEM((C, W), BF16)],
        compiler_params=_params(("parallel",), 32),
        name="gmlp_gate",
    )(proj, proj, vn_g, vn_b, ws, bias_full)


def _gla_kernel(q_ref, k_ref, v_ref, r_ref, alr_ref, wa2_ref, ba_ref, g_ref, o_ref,
                st_ref, la_ref, *, heads):
    rows, dk = q_ref.shape
    dv = v_ref.shape[1]
    hk = dk // heads
    hv = dv // heads
    C = GLA_CHUNK
    scale = hk ** -0.5

    @pl.when(pl.program_id(1) == 0)
    def _():
        st_ref[...] = jnp.zeros(st_ref.shape, F32)

    z = _dot(alr_ref[...].astype(BF16), wa2_ref[...]) + ba_ref[...]
    la_ref[...] = (jnp.minimum(z, 0.0) - jnp.log(1.0 + jnp.exp(-jnp.abs(z)))) * (1.0 / GLA_TAU)

    ri = lax.broadcasted_iota(jnp.int32, (C, C), 0)
    ci = lax.broadcasted_iota(jnp.int32, (C, C), 1)
    causal = ri >= ci
    tri = causal.astype(BF16)

    def chunk_step(c, carry):
        rs = pl.ds(pl.multiple_of(c * C, C), C)
        la = la_ref[rs, :]
        hi = la.astype(BF16)
        r1 = la - hi.astype(F32)
        mid = r1.astype(BF16)
        lo = (r1 - mid.astype(F32)).astype(BF16)
        cum = _dot(tri, hi) + _dot(tri, mid) + _dot(tri, lo)
        cl = cum[C - 1:C, :]
        q = q_ref[rs, :].astype(F32) * scale
        k = k_ref[rs, :].astype(F32)
        qd = (q * jnp.exp(cum)).astype(BF16)
        ki = (k * jnp.exp(-cum)).astype(BF16)
        ke = (k * jnp.exp(cl - cum)).astype(BF16)
        dec = jnp.exp(cl)
        for h in range(heads):
            ks = slice(h * hk, (h + 1) * hk)
            vs = slice(h * hv, (h + 1) * hv)
            vh = v_ref[rs, vs]
            s = lax.dot_general(qd[:, ks], ki[:, ks], NT_DIMS, preferred_element_type=F32)
            s = jnp.where(causal, s, 0.0).astype(BF16)
            st = st_ref[h]
            o = _dot(s, vh) + lax.dot_general(qd[:, ks], st.astype(BF16), NT_DIMS,
                                              preferred_element_type=F32)
            st_ref[h] = st * dec[:, ks] + lax.dot_general(vh, ke[:, ks], TN_DIMS,
                                                          preferred_element_type=F32)
            ms = jnp.mean(o * o, axis=-1, keepdims=True)
            on = o * lax.rsqrt(ms + EPS) * g_ref[:, vs]
            rr = r_ref[rs, vs].astype(F32)
            o_ref[rs, vs] = (on * (rr * _sigmoid(rr))).astype(o_ref.dtype)
        return carry

    lax.fori_loop(0, rows // C, chunk_step, 0)


def _gla_call(proj, alr, wa2p, ba, on_g, B, S, W):
    DK = wa2p.shape[1]
    DV = on_g.shape[1]
    H = GLA_HEADS
    Cb = min(512, S)
    nb = S // Cb
    q_blk = (2 * W) // DK
    k_blk = (2 * W + DK) // DK
    v_blk = (2 * W + 2 * DK) // DV
    r_blk = (2 * W + 2 * DK + DV) // DV
    kern = functools.partial(_gla_kernel, heads=H)
    return pl.pallas_call(
        kern,
        grid=(B, nb),
        in_specs=[pl.BlockSpec((Cb, DK), lambda b, i: (b * nb + i, q_blk)),
                  pl.BlockSpec((Cb, DK), lambda b, i: (b * nb + i, k_blk)),
                  pl.BlockSpec((Cb, DV), lambda b, i: (b * nb + i, v_blk)),
                  pl.BlockSpec((Cb, DV), lambda b, i: (b * nb + i, r_blk)),
                  pl.BlockSpec((Cb, LANES), lambda b, i: (b * nb + i, 0)),
                  pl.BlockSpec((LANES, DK), lambda b, i: (0, 0)),
                  pl.BlockSpec((1, DK), lambda b, i: (0, 0)),
                  pl.BlockSpec((1, DV), lambda b, i: (0, 0))],
        out_specs=pl.BlockSpec((Cb, DV), lambda b, i: (b * nb + i, 0)),
        out_shape=jax.ShapeDtypeStruct((B * S, DV), BF16),
        scratch_shapes=[pltpu.VMEM((H, DV // H, DK // H), F32),
                        pltpu.VMEM((Cb, DK), F32)],
        compiler_params=_params(("parallel", "arbitrary"), 32),
        name="gla",
    )(proj, proj, proj, proj, alr, wa2p, ba, on_g)


def _merge_kernel(ya_ref, yb_ref, ga_ref, gb_ref, wpa_ref, wpb_ref, wout_ref, x_ref, g1_ref,
                  n2_ref, sc_ref, sh_ref, wr_ref, x1_ref, h2_ref, lg_ref):
    a = _dot(ya_ref[...], wpa_ref[0])
    b = _dot(yb_ref[...], wpb_ref[0])
    y = _sigmoid(ga_ref[...].astype(F32)) * a + _sigmoid(gb_ref[...].astype(F32)) * b
    out = _dot(y.astype(BF16), wout_ref[0])
    x1 = x_ref[...] + g1_ref[0] * out
    x1_ref[...] = x1
    ms = jnp.mean(x1 * x1, axis=-1, keepdims=True)
    h2 = (x1 * lax.rsqrt(ms + EPS) * n2_ref[...]) * (1.0 + sc_ref[0]) + sh_ref[0]
    _store_packed(h2_ref, h2)
    lg_ref[...] = _dot(h2.astype(BF16), wr_ref[...])


def _merge_call(ya, yb, proj, wpa, wpb, wout, x2, g1, n2g, sc2, sh2, wr, S, W, l):
    T, D = x2.shape
    ga_blk = (proj.shape[1] - 2 * D) // D
    gb_blk = ga_blk + 1
    tm = min(256, S)
    per_b = S // tm
    n_s = D // (2 * LANES)
    const = dict(pipeline_mode=pl.Buffered(1))
    return pl.pallas_call(
        _merge_kernel,
        grid=(T // tm,),
        in_specs=[pl.BlockSpec((tm, W), lambda i: (i, 0)),
                  pl.BlockSpec((tm, W), lambda i: (i, 0)),
                  pl.BlockSpec((tm, D), lambda i: (i, ga_blk)),
                  pl.BlockSpec((tm, D), lambda i: (i, gb_blk)),
                  pl.BlockSpec((1, W, D), lambda i: (l, 0, 0), **const),
                  pl.BlockSpec((1, W, D), lambda i: (l, 0, 0), **const),
                  pl.BlockSpec((1, D, D), lambda i: (l, 0, 0), **const),
                  pl.BlockSpec((tm, D), lambda i: (i, 0)),
                  pl.BlockSpec((1, 1, D), lambda i: (i // per_b, 0, 0)),
                  pl.BlockSpec((1, D), lambda i: (0, 0)),
                  pl.BlockSpec((1, 1, D), lambda i: (i // per_b, 0, 0)),
                  pl.BlockSpec((1, 1, D), lambda i: (i // per_b, 0, 0)),
                  pl.BlockSpec((D, LANES), lambda i: (0, 0), **const)],
        out_specs=[pl.BlockSpec((tm, D), lambda i: (i, 0)),
                   pl.BlockSpec((tm * n_s, LANES), lambda i: (i, 0)),
                   pl.BlockSpec((tm, LANES), lambda i: (i, 0))],
        out_shape=[jax.ShapeDtypeStruct((T, D), F32),
                   jax.ShapeDtypeStruct((T * n_s, LANES), jnp.uint32),
                   jax.ShapeDtypeStruct((T, LANES), F32)],
        compiler_params=_params(("parallel",), 56),
        name="merge_out",
    )(ya, yb, proj, proj, wpa, wpb, wout, x2, g1, n2g, sc2, sh2, wr)


def _router_kernel(lg_ref, b_ref, e_ref, w_ref, cnt_ref, carry, *, groups, per_group):
    @pl.when(pl.program_id(0) == 0)
    def _():
        carry[...] = jnp.zeros(carry.shape, F32)

    lg = lg_ref[...] + b_ref[...]
    lane = lax.broadcasted_iota(jnp.int32, lg.shape, 1)
    lane_f = lane.astype(F32)
    neg = jnp.float32(-1e30)
    big = jnp.float32(LANES)

    def first_argmax(vals):
        m = jnp.max(vals, axis=-1, keepdims=True)
        idx = jnp.min(jnp.where(vals == m, lane_f, big), axis=-1, keepdims=True)
        return m, idx

    gmask = lane < groups
    gl = jnp.where(gmask, lg, neg)
    gmax, gidx = first_argmax(gl)
    gsum = jnp.sum(jnp.where(gmask, jnp.exp(gl - gmax), 0.0), axis=-1, keepdims=True)
    gtop = 1.0 / gsum

    lo = groups + gidx * per_group
    emask = (lane_f >= lo) & (lane_f < lo + per_group)
    el = jnp.where(emask, lg, neg)
    m1, i1 = first_argmax(el)
    el2 = jnp.where(lane_f == i1, neg, el)
    m2, i2 = first_argmax(el2)
    den = jnp.sum(jnp.where(emask, jnp.exp(el - m1), 0.0), axis=-1, keepdims=True)
    p1 = 1.0 / den
    p2 = jnp.exp(m2 - m1) / den
    ps = p1 + p2
    w1 = gtop * p1 / ps
    w2 = gtop * p2 / ps
    e1 = i1 - groups
    e2 = i2 - groups
    w_ref[...] = jnp.where(lane == 0, w1, jnp.where(lane == 1, w2, 0.0))

    oh0 = lane_f == e1
    oh1 = lane_f == e2
    both = oh0.astype(F32) + oh1.astype(F32)
    tb = lg.shape[0]
    r = lax.broadcasted_iota(jnp.int32, (tb, tb), 0)
    c = lax.broadcasted_iota(jnp.int32, (tb, tb), 1)
    strict = (r > c).astype(BF16)
    before = _dot(strict, both.astype(BF16)) + carry[0:1, :]
    r0 = jnp.sum(jnp.where(oh0, before, 0.0), axis=-1, keepdims=True)
    r1 = jnp.sum(jnp.where(oh1, before, 0.0), axis=-1, keepdims=True)
    packed = jnp.where(lane == 0, e1, jnp.where(lane == 1, e2,
                       jnp.where(lane == 2, r0, jnp.where(lane == 3, r1, 0.0))))
    e_ref[...] = packed.astype(jnp.int32)
    carry[...] = carry[...] + jnp.sum(both, axis=0, keepdims=True)
    cnt_ref[...] = carry[...]


def _router_call(logits, bias, per_group):
    T = logits.shape[0]
    tm = min(512, T)
    kern = functools.partial(_router_kernel, groups=MOE_GROUPS, per_group=per_group)
    return pl.pallas_call(
        kern,
        grid=(T // tm,),
        in_specs=[pl.BlockSpec((tm, LANES), lambda i: (i, 0)),
                  pl.BlockSpec((1, LANES), lambda i: (0, 0))],
        out_specs=[pl.BlockSpec((tm, LANES), lambda i: (i, 0)),
                   pl.BlockSpec((tm, LANES), lambda i: (i, 0)),
                   pl.BlockSpec((8, LANES), lambda i: (0, 0))],
        out_shape=[jax.ShapeDtypeStruct((T, LANES), jnp.int32),
                   jax.ShapeDtypeStruct((T, LANES), F32),
                   jax.ShapeDtypeStruct((8, LANES), F32)],
        scratch_shapes=[pltpu.VMEM((8, LANES), F32)],
        compiler_params=_params(("arbitrary",), 16),
        name="router_topk",
    )(logits, bias)


def _row(ref, t, n_s):
    return ref.at[pl.ds(pl.multiple_of(t * n_s, n_s), n_s)]


def _dispatch_kernel(zr_ref, na_ref, dest_ref, h_ref, out_ref, zbuf, sem, zsem, *, n_s, block):
    n_assign = dest_ref.shape[2]
    tm = n_assign // TOP_K
    nb = out_ref.shape[0] // (block * n_s)

    def zero_block(b):
        start = pl.multiple_of(b * (block * n_s), block * n_s)
        return pltpu.make_async_copy(zbuf, out_ref.at[pl.ds(start, block * n_s)], zsem.at[0])

    @pl.when(pl.program_id(0) == 0)
    def _():
        zbuf[...] = jnp.zeros(zbuf.shape, zbuf.dtype)
        n_exp = zr_ref.shape[0]
        lax.fori_loop(0, n_exp, lambda e, c: (zero_block(zr_ref[e]).start(), c)[1], 0)
        lax.fori_loop(na_ref[0], nb, lambda b, c: (zero_block(b).start(), c)[1], 0)
        lax.fori_loop(0, n_exp, lambda e, c: (zero_block(0).wait(), c)[1], 0)
        lax.fori_loop(na_ref[0], nb, lambda b, c: (zero_block(0).wait(), c)[1], 0)

    tok_unroll = ISSUE_UNROLL // TOP_K

    def issue(g, c):
        for u in range(tok_unroll):
            t = g * tok_unroll + u
            for k in range(TOP_K):
                pltpu.make_async_copy(_row(h_ref, t, n_s),
                                      _row(out_ref, dest_ref[0, 0, t * TOP_K + k], n_s),
                                      sem.at[0]).start(priority=k % 2)
        return c
    lax.fori_loop(0, tm // tok_unroll, issue, 0)
    for _ in range(TOP_K):
        pltpu.make_async_copy(h_ref, out_ref.at[pl.ds(0, tm * n_s)], sem.at[0]).wait()


def _dispatch_call(h2p, dest, zero_rows, n_act, n_rows, n_s):
    T = h2p.shape[0] // n_s
    tm = min(1024, T)
    nt = T // tm
    kern = functools.partial(_dispatch_kernel, n_s=n_s, block=MOE_BLOCK)
    grid_spec = pltpu.PrefetchScalarGridSpec(
        num_scalar_prefetch=2,
        grid=(nt,),
        in_specs=[pl.BlockSpec((1, 1, tm * TOP_K), lambda i, zr, na: (i, 0, 0),
                               memory_space=pltpu.SMEM),
                  pl.BlockSpec((tm * n_s, LANES), lambda i, zr, na: (i, 0))],
        out_specs=pl.BlockSpec(memory_space=pl.ANY),
        scratch_shapes=[pltpu.VMEM((MOE_BLOCK * n_s, LANES), h2p.dtype),
                        pltpu.SemaphoreType.DMA((1,)),
                        pltpu.SemaphoreType.DMA((1,))],
    )
    return pl.pallas_call(
        kern,
        grid_spec=grid_spec,
        out_shape=jax.ShapeDtypeStruct((n_rows * n_s, LANES), h2p.dtype),
        compiler_params=_params(("arbitrary",), 16),
        name="row_dispatch",
    )(zero_rows, n_act, dest.reshape(nt, 1, tm * TOP_K), h2p)


def _stage_expert_weights(layer, be_ref, na_ref, nx_ref, i, hbm_refs, stage, bf_refs, sem):
    e = be_ref[i]
    first = (i == 0) | (e != be_ref[jnp.maximum(i - 1, 0)])

    def copies(expert):
        return [pltpu.make_async_copy(w.at[layer, expert], stage[k], sem.at[k])
                for k, w in enumerate(hbm_refs)]

    @pl.when(i == 0)
    def _():
        for cp in copies(e):
            cp.start()

    @pl.when((i < na_ref[0]) & first)
    def _():
        for k, cp in enumerate(copies(e)):
            cp.wait()
            bf_refs[k][...] = stage[k][...].astype(BF16)
        nxt = nx_ref[i]

        @pl.when(nxt >= 0)
        def _():
            for cp in copies(nxt):
                cp.start()


def _expert_kernel(be_ref, na_ref, nx_ref, xs_ref, wg_hbm, wu_hbm, wd_hbm, ys_ref,
                   sg, su, sd, wgb, wub, wdb, sem, *, layer):
    i = pl.program_id(0)
    active = i < na_ref[0]
    _stage_expert_weights(layer, be_ref, na_ref, nx_ref, i, (wg_hbm, wu_hbm, wd_hbm),
                          (sg, su, sd), (wgb, wub, wdb), sem)

    @pl.when(active)
    def _():
        n_s = sd.shape[1] // (2 * LANES)
        rows = xs_ref.shape[0] // n_s
        los, his = _load_packed(xs_ref, rows, n_s)
        x = jnp.concatenate([p.astype(BF16) for p in los + his], axis=1)
        a = _dot(x, wgb[...])
        b = _dot(x, wub[...])
        hid = (a * _sigmoid(a) * b).astype(BF16)
        _store_packed(ys_ref, _dot(hid, wdb[...]))

    @pl.when(jnp.logical_not(active))
    def _():
        ys_ref[...] = jnp.zeros(ys_ref.shape, ys_ref.dtype)


def _experts_call(xs, block_e, n_act, next_e, w_gate, w_up, w_down, l):
    D, DE = w_gate.shape[-2], w_gate.shape[-1]
    n_s = D // (2 * LANES)
    n_rows = xs.shape[0] // n_s
    BM = MOE_BLOCK
    nb = n_rows // BM
    any_spec = pl.BlockSpec(memory_space=pl.ANY)
    grid_spec = pltpu.PrefetchScalarGridSpec(
        num_scalar_prefetch=3,
        grid=(nb,),
        in_specs=[pl.BlockSpec((BM * n_s, LANES),
                               lambda i, be, na, nx: (jnp.minimum(i, na[0] - 1), 0)),
                  any_spec, any_spec, any_spec],
        out_specs=pl.BlockSpec((BM * n_s, LANES), lambda i, be, na, nx: (i, 0)),
        scratch_shapes=[pltpu.VMEM((D, DE), F32), pltpu.VMEM((D, DE), F32), pltpu.VMEM((DE, D), F32),
                        pltpu.VMEM((D, DE), BF16), pltpu.VMEM((D, DE), BF16),
                        pltpu.VMEM((DE, D), BF16),
                        pltpu.SemaphoreType.DMA((3,))],
    )
    return pl.pallas_call(
        functools.partial(_expert_kernel, layer=l),
        grid_spec=grid_spec,
        out_shape=jax.ShapeDtypeStruct((n_rows * n_s, LANES), jnp.uint32),
        compiler_params=_params(("arbitrary",), 56),
        name="expert_mlp",
    )(block_e, n_act, next_e, xs, w_gate, w_up, w_down)


def _combine_kernel(cur_ref, nxt_ref, x_ref, ys_ref, w_ref, g2_ref, fg_ref, o_ref, ybuf0, ybuf1, sem,
                    *, final_norm):
    i = pl.program_id(0)
    n = pl.num_programs(0)
    tm, d = x_ref.shape
    half = d // 2
    n_s = half // LANES
    n_assign = tm * TOP_K
    bufs = (ybuf0, ybuf1)

    def row_copy(idx_ref, r, s):
        return pltpu.make_async_copy(_row(ys_ref, idx_ref[0, 0, r], n_s), _row(bufs[s], r, n_s),
                                     sem.at[s])

    def wait_all(s):
        pltpu.make_async_copy(ys_ref.at[pl.ds(0, n_assign * n_s)], bufs[s], sem.at[s]).wait()

    @pl.when(i == 0)
    def _():
        def issue(g, c):
            for u in range(ISSUE_UNROLL):
                row_copy(cur_ref, g * ISSUE_UNROLL + u, 0).start(priority=u % 2)
            return c
        lax.fori_loop(0, n_assign // ISSUE_UNROLL, issue, 0)

    def step(s):
        wait_all(s)
        for r in range(n_assign):
            row_copy(nxt_ref, r, 1 - s).start(priority=r % 2)
        yg_ref = bufs[s]
        w = w_ref[...]
        w0 = jnp.broadcast_to(w[:, 0:1], (tm, LANES))
        w1 = jnp.broadcast_to(w[:, 1:2], (tm, LANES))
        lo0, hi0 = _load_packed(yg_ref, tm, n_s, offset=0, group=TOP_K)
        lo1, hi1 = _load_packed(yg_ref, tm, n_s, offset=1, group=TOP_K)
        ss = jnp.zeros((tm, 1), F32)
        for c in range(n_s):
            for base, p0, p1 in ((0, lo0[c], lo1[c]), (half, hi0[c], hi1[c])):
                cs = slice(base + c * LANES, base + (c + 1) * LANES)
                xv = x_ref[:, cs] + g2_ref[0, :, cs] * (w0 * p0 + w1 * p1)
                o_ref[:, cs] = xv
                if final_norm:
                    ss = ss + jnp.sum(xv * xv, axis=-1, keepdims=True)
        if final_norm:
            o_ref[...] = o_ref[...] * lax.rsqrt(ss * (1.0 / d) + EPS) * fg_ref[...]

        @pl.when(i == n - 1)
        def _():
            wait_all(1 - s)

    for s in range(2):
        pl.when(i % 2 == s)(functools.partial(step, s))


def _combine_call(x1, ys, dest, wts, g2, final_g, S, final_norm):
    T, D = x1.shape
    tm = min(512, S)
    nt = T // tm
    per_b = S // tm
    n_s = D // (2 * LANES)
    dest3 = dest.reshape(nt, 1, tm * TOP_K)
    kern = functools.partial(_combine_kernel, final_norm=final_norm)
    return pl.pallas_call(
        kern,
        grid=(nt,),
        in_specs=[pl.BlockSpec((1, 1, tm * TOP_K), lambda i: (i, 0, 0), memory_space=pltpu.SMEM),
                  pl.BlockSpec((1, 1, tm * TOP_K), lambda i: (jnp.minimum(i + 1, nt - 1), 0, 0),
                               memory_space=pltpu.SMEM),
                  pl.BlockSpec((tm, D), lambda i: (i, 0)),
                  pl.BlockSpec(memory_space=pl.ANY),
                  pl.BlockSpec((tm, LANES), lambda i: (i, 0)),
                  pl.BlockSpec((1, 1, D), lambda i: (i // per_b, 0, 0)),
                  pl.BlockSpec((1, D), lambda i: (0, 0))],
        out_specs=pl.BlockSpec((tm, D), lambda i: (i, 0)),
        out_shape=jax.ShapeDtypeStruct((T, D), F32),
        scratch_shapes=[pltpu.VMEM((tm * TOP_K * n_s, LANES), jnp.uint32),
                        pltpu.VMEM((tm * TOP_K * n_s, LANES), jnp.uint32),
                        pltpu.SemaphoreType.DMA((2,))],
        compiler_params=_params(("arbitrary",), 48),
        name="moe_combine",
    )(dest3, dest3, x1, ys, wts, g2, final_g)


def _dispatch_plan(routed, cnt, n_experts):
    T = routed.shape[0]
    A = T * TOP_K
    BM = MOE_BLOCK
    eid = routed[:, :TOP_K]
    rank = routed[:, TOP_K:2 * TOP_K]
    counts = cnt[0, :n_experts].astype(jnp.int32)
    padded = (counts + BM - 1) // BM * BM
    pad_end = jnp.cumsum(padded)
    pad_start = pad_end - padded
    onehot = eid[:, :, None] == jnp.arange(n_experts, dtype=jnp.int32)[None, None, :]
    dest = (jnp.sum(jnp.where(onehot, pad_start[None, None, :], 0), axis=-1)
            + rank).reshape(A)
    nb = (A + n_experts * (BM - 1) + BM - 1) // BM
    block_start = jnp.arange(nb, dtype=jnp.int32) * BM
    block_e = jnp.minimum(jnp.sum(block_start[:, None] >= pad_end[None, :], axis=1),
                          n_experts - 1).astype(jnp.int32)
    n_act = (pad_end[-1] // BM).astype(jnp.int32).reshape(1)
    last_block = jnp.maximum(pad_end // BM - 1, 0).astype(jnp.int32)
    ex = jnp.arange(n_experts, dtype=jnp.int32)
    later = (ex[None, :] > ex[:, None]) & (counts[None, :] > 0)
    next_of = jnp.min(jnp.where(later, ex[None, :], n_experts), axis=1)
    next_of = jnp.where(next_of < n_experts, next_of, -1)
    next_e = jnp.sum(jnp.where(block_e[:, None] == ex[None, :], next_of[None, :], 0),
                     axis=1).astype(jnp.int32)
    return dest.astype(jnp.int32), block_e, n_act, next_e, last_block, nb * BM


def kernel(x, c, ada_w, ada_b, norm1_g, w_in, gm_vn_g, gm_vn_b, gm_ws, gm_bs, gla_wa2, gla_ba,
           gla_on_g, w_pa, w_pb, w_out, norm2_g, w_rg, b_rg, w_re, b_re, w_e_gate, w_e_up,
           w_e_down, final_g):
    B, S, D = x.shape
    T = B * S
    L = ada_w.shape[0]
    W = gm_vn_g.shape[1]
    G, C = gm_ws.shape[1], gm_ws.shape[2]
    RANK, DK = gla_wa2.shape[1], gla_wa2.shape[2]
    DV = gla_on_g.shape[1]
    E = w_e_gate.shape[1]
    per_group = E // MOE_GROUPS
    SUB = D // (2 * LANES)
    assert B <= 8 and MOE_GROUPS + E <= LANES and RANK <= LANES

    c8 = jnp.zeros((8, D), F32).at[:B].set(c)
    mod = _ada_call(c8, ada_w, ada_b.reshape(L, 1, 6 * D))

    o_alr = 2 * W + 2 * DK + 2 * DV
    x2 = x.reshape(T, D)
    w_main = jnp.concatenate([w_in[:, :, :o_alr], w_in[:, :, o_alr + RANK:]], axis=2).astype(BF16)
    w_alr = jnp.zeros((L, D, LANES), BF16).at[:, :, :RANK].set(
        w_in[:, :, o_alr:o_alr + RANK].astype(BF16))
    w_pa_b, w_pb_b, w_out_b = w_pa.astype(BF16), w_pb.astype(BF16), w_out.astype(BF16)
    for l in range(L):
        sh1, sc1, g1, sh2, sc2, g2 = [mod[l, :B, k * D:(k + 1) * D].reshape(B, 1, D) for k in range(6)]
        wa2p = jnp.zeros((LANES, DK), BF16).at[:RANK].set(gla_wa2[l].astype(BF16))
        bias_full = jnp.repeat(gm_bs[l].T, W // G, axis=1)
        wr = jnp.zeros((D, LANES), BF16).at[:, :MOE_GROUPS + E].set(
            jnp.concatenate([w_rg[l], w_re[l]], axis=1).astype(BF16))
        br = jnp.zeros((1, LANES), F32).at[0, :MOE_GROUPS + E].set(
            jnp.concatenate([b_rg[l], b_re[l]]))

        proj, alr = _inproj_call(x2, norm1_g[l].reshape(1, D), sc1, sh1, w_main, w_alr, S, l)
        ya = _gmlp_call(proj, gm_vn_g[l].reshape(1, W), gm_vn_b[l].reshape(1, W), gm_ws[l], bias_full)
        yb = _gla_call(proj, alr, wa2p, gla_ba[l].reshape(1, DK), gla_on_g[l].reshape(1, DV), B, S, W)
        x1, h2, logits = _merge_call(ya, yb, proj, w_pa_b, w_pb_b, w_out_b, x2, g1,
                                     norm2_g[l].reshape(1, D), sc2, sh2, wr, S, W, l)
        routed, wts, cnt = _router_call(logits, br, per_group)
        dest, block_e, n_act, next_e, last_block, n_rows = _dispatch_plan(routed, cnt, E)
        xs = _dispatch_call(h2, dest, last_block, n_act, n_rows, SUB)
        ys = _experts_call(xs, block_e, n_act, next_e, w_e_gate, w_e_up, w_e_down, l)
        x2 = _combine_call(x1, ys, dest, wts, g2, final_g.reshape(1, D), S, final_norm=(l == L - 1))
    return x2.reshape(B, S, D)
```

```python
import functools

import jax
import jax.numpy as jnp
from jax import lax
from jax.experimental import pallas as pl
from jax.experimental.pallas import tpu as pltpu

F32 = jnp.float32
BF16 = jnp.bfloat16

EPS = 1e-6
GLA_HEADS = 4
GLA_TAU = 16.0
GLA_CHUNK = 64
MOE_GROUPS = 4
TOP_K = 2

LANES = 128
MIB = 1024 * 1024
MOE_BLOCK = 256
ISSUE_UNROLL = 8

NT_DIMS = (((1,), (1,)), ((), ()))
TN_DIMS = (((0,), (0,)), ((), ()))


def _params(semantics, vmem_mib):
    return pltpu.CompilerParams(dimension_semantics=semantics,
                                vmem_limit_bytes=int(vmem_mib * MIB))


def _dot(a, b):
    return jnp.dot(a, b, preferred_element_type=F32)


def _sigmoid(x):
    return 1.0 / (1.0 + jnp.exp(-x))


def _gelu(x):
    return 0.5 * x * (1.0 + lax.erf(x * (2.0 ** -0.5)))


def _pack_pairs(lo, hi):
    lo_w = pltpu.bitcast(lo.astype(BF16).astype(F32), jnp.uint32) >> 16
    hi_w = pltpu.bitcast(hi.astype(BF16).astype(F32), jnp.uint32) & jnp.uint32(0xFFFF0000)
    return lo_w | hi_w


def _unpack_pairs(w):
    lo = pltpu.bitcast(w << 16, F32)
    hi = pltpu.bitcast(w & jnp.uint32(0xFFFF0000), F32)
    return lo, hi


def _store_packed(ref, x):
    rows, d = x.shape
    half = d // 2
    n_s = half // LANES
    for s in range(n_s):
        lo = x[:, s * LANES:(s + 1) * LANES]
        hi = x[:, half + s * LANES:half + (s + 1) * LANES]
        ref[pl.ds(s, rows, stride=n_s), :] = _pack_pairs(lo, hi)


def _load_packed(ref, rows, n_s, offset=0, group=1):
    los, his = [], []
    for s in range(n_s):
        lo, hi = _unpack_pairs(ref[pl.ds(offset * n_s + s, rows, stride=n_s * group), :])
        los.append(lo)
        his.append(hi)
    return los, his


def _ada_kernel(c_ref, w_ref, b_ref, o_ref):
    c = c_ref[...]
    cond = c * _sigmoid(c)
    o_ref[0] = _dot(cond.astype(BF16), w_ref[0].astype(BF16)) + b_ref[0]


def _ada_call(c8, ada_w, ada_b3):
    L, D, N = ada_w.shape
    tn = min(1024, N)
    return pl.pallas_call(
        _ada_kernel,
        grid=(L, N // tn),
        in_specs=[pl.BlockSpec((8, D), lambda l, j: (0, 0)),
                  pl.BlockSpec((1, D, tn), lambda l, j: (l, 0, j)),
                  pl.BlockSpec((1, 1, tn), lambda l, j: (l, 0, j))],
        out_specs=pl.BlockSpec((1, 8, tn), lambda l, j: (l, 0, j)),
        out_shape=jax.ShapeDtypeStruct((L, 8, N), F32),
        compiler_params=_params(("parallel", "parallel"), 40),
        name="ada_mod",
    )(c8, ada_w, ada_b3)


def _cast_kernel(w_ref, o_ref):
    o_ref[...] = w_ref[...].astype(o_ref.dtype)


def _cast_call(w):
    L, K, N = w.shape
    tn = min(1024, N)
    return pl.pallas_call(
        _cast_kernel,
        grid=(L, N // tn),
        in_specs=[pl.BlockSpec((1, K, tn), lambda l, j: (l, 0, j))],
        out_specs=pl.BlockSpec((1, K, tn), lambda l, j: (l, 0, j)),
        out_shape=jax.ShapeDtypeStruct(w.shape, BF16),
        compiler_params=_params(("parallel", "parallel"), 32),
        name="cast_bf16",
    )(w)


def _inproj_weight_kernel(a_ref, b_ref, o_ref, lr_ref, *, n_head, rank):
    j = pl.program_id(1)

    @pl.when(j < n_head)
    def _():
        o_ref[0] = a_ref[0].astype(BF16)

    @pl.when(j >= n_head)
    def _():
        o_ref[0] = jnp.concatenate([a_ref[0][:, rank:], b_ref[0][:, :rank]], axis=1).astype(BF16)

    @pl.when(j == n_head)
    def _():
        lane = lax.broadcasted_iota(jnp.int32, lr_ref.shape[1:], 1)
        lr_ref[0] = jnp.where(lane < rank, a_ref[0][:, :LANES], 0.0).astype(BF16)


def _inproj_weight_call(w_in, head_cols, rank):
    L, D, N = w_in.shape
    NM = N - rank
    tn = min(1024, NM)
    n_head = head_cols // tn
    per_tn = tn // LANES
    assert head_cols % tn == 0 and NM % tn == 0 and rank < LANES
    kern = functools.partial(_inproj_weight_kernel, n_head=n_head, rank=rank)
    return pl.pallas_call(
        kern,
        grid=(L, NM // tn),
        in_specs=[pl.BlockSpec((1, D, tn), lambda l, j: (l, 0, j)),
                  pl.BlockSpec((1, D, LANES),
                               lambda l, j: (l, 0, (jnp.maximum(j, n_head) + 1) * per_tn))],
        out_specs=[pl.BlockSpec((1, D, tn), lambda l, j: (l, 0, j)),
                   pl.BlockSpec((1, D, LANES), lambda l, j: (l, 0, 0))],
        out_shape=[jax.ShapeDtypeStruct((L, D, NM), BF16),
                   jax.ShapeDtypeStruct((L, D, LANES), BF16)],
        compiler_params=_params(("parallel", "arbitrary"), 48),
        name="in_proj_weight",
    )(w_in, w_in)


def _inproj_kernel(x_ref, g_ref, sc_ref, sh_ref, w_ref, walr_ref, o_ref, alr_ref, h_scr):
    @pl.when(pl.program_id(1) == 0)
    def _():
        x = x_ref[...]
        ms = jnp.mean(x * x, axis=-1, keepdims=True)
        y = x * lax.rsqrt(ms + EPS) * g_ref[...]
        h = (y * (1.0 + sc_ref[0]) + sh_ref[0]).astype(BF16)
        h_scr[...] = h
        alr_ref[...] = _dot(h, walr_ref[0])

    o_ref[...] = _dot(h_scr[...], w_ref[0]).astype(o_ref.dtype)


def _inproj_call(x2, g, sc, sh, w_main, w_alr, S, l):
    T, D = x2.shape
    NM = w_main.shape[2]
    tm = min(1024, S)
    tn = min(1024, NM)
    per_b = S // tm
    return pl.pallas_call(
        _inproj_kernel,
        grid=(T // tm, NM // tn),
        in_specs=[pl.BlockSpec((tm, D), lambda i, j: (i, 0)),
                  pl.BlockSpec((1, D), lambda i, j: (0, 0)),
                  pl.BlockSpec((1, 1, D), lambda i, j: (i // per_b, 0, 0)),
                  pl.BlockSpec((1, 1, D), lambda i, j: (i // per_b, 0, 0)),
                  pl.BlockSpec((1, D, tn), lambda i, j: (l, 0, j)),
                  pl.BlockSpec((1, D, LANES), lambda i, j: (l, 0, 0))],
        out_specs=[pl.BlockSpec((tm, tn), lambda i, j: (i, j)),
                   pl.BlockSpec((tm, LANES), lambda i, j: (i, 0))],
        out_shape=[jax.ShapeDtypeStruct((T, NM), BF16),
                   jax.ShapeDtypeStruct((T, LANES), F32)],
        scratch_shapes=[pltpu.VMEM((tm, D), BF16)],
        compiler_params=_params(("parallel", "arbitrary"), 56),
        name="in_proj",
    )(x2, g, sc, sh, w_main, w_alr)


def _gmlp_kernel(u_ref, v_ref, g_ref, b_ref, ws_ref, bias_ref, o_ref, vn_scr, *, groups, chunk):
    rows, width = v_ref.shape
    gc = width // groups
    r = lax.broadcasted_iota(jnp.int32, (chunk, chunk), 0)
    c = lax.broadcasted_iota(jnp.int32, (chunk, chunk), 1)
    causal = r >= c
    for ci in range(rows // chunk):
        rs = slice(ci * chunk, (ci + 1) * chunk)
        gv = _gelu(v_ref[rs, :].astype(F32))
        mu = jnp.mean(gv, axis=-1, keepdims=True)
        xc = gv - mu
        var = jnp.mean(xc * xc, axis=-1, keepdims=True)
        vn = xc * lax.rsqrt(var + EPS) * g_ref[...] + b_ref[...]
        vn_scr[...] = vn.astype(BF16)
        for gi in range(groups):
            cs = slice(gi * gc, (gi + 1) * gc)
            wm = jnp.where(causal, ws_ref[gi], 0.0).astype(BF16)
            mixed = _dot(wm, vn_scr[:, cs]) + bias_ref[:, cs]
            gu = _gelu(u_ref[rs, cs].astype(F32))
            o_ref[rs, cs] = (gu * mixed).astype(o_ref.dtype)


def _gmlp_call(proj, vn_g, vn_b, ws, bias_full):
    T = proj.shape[0]
    G, C, _ = ws.shape
    W = vn_g.shape[1]
    R = min(2 * C, T)
    kern = functools.partial(_gmlp_kernel, groups=G, chunk=C)
    return pl.pallas_call(
        kern,
        grid=(T // R,),
        in_specs=[pl.BlockSpec((R, W), lambda i: (i, 0)),
                  pl.BlockSpec((R, W), lambda i: (i, 1)),
                  pl.BlockSpec((1, W), lambda i: (0, 0)),
                  pl.BlockSpec((1, W), lambda i: (0, 0)),
                  pl.BlockSpec((G, C, C), lambda i: (0, 0, 0)),
                  pl.BlockSpec((C, W), lambda i: (0, 0))],
        out_specs=pl.BlockSpec((R, W), lambda i: (i, 0)),
        out_shape=jax.ShapeDtypeStruct((T, W), BF16),
        scratch_shapes=[pltpu.VMEM((C, W), BF16)],
        compiler_params=_params(("parallel",), 32),
        name="gmlp_gate",
    )(proj, proj, vn_g, vn_b, ws, bias_full)


def _gla_kernel(q_ref, k_ref, v_ref, r_ref, alr_ref, wa2_ref, ba_ref, g_ref, o_ref,
                st_ref, la_ref, *, heads):
    rows, dk = q_ref.shape
    dv = v_ref.shape[1]
    hk = dk // heads
    hv = dv // heads
    C = GLA_CHUNK
    scale = hk ** -0.5

    @pl.when(pl.program_id(1) == 0)
    def _():
        st_ref[...] = jnp.zeros(st_ref.shape, F32)

    z = _dot(alr_ref[...].astype(BF16), wa2_ref[...]) + ba_ref[...]
    la_ref[...] = (jnp.minimum(z, 0.0) - jnp.log(1.0 + jnp.exp(-jnp.abs(z)))) * (1.0 / GLA_TAU)

    ri = lax.broadcasted_iota(jnp.int32, (C, C), 0)
    ci = lax.broadcasted_iota(jnp.int32, (C, C), 1)
    causal = ri >= ci
    tri = causal.astype(BF16)

    def chunk_step(c, carry):
        rs = pl.ds(pl.multiple_of(c * C, C), C)
        la = la_ref[rs, :]
        hi = la.astype(BF16)
        r1 = la - hi.astype(F32)
        mid = r1.astype(BF16)
        lo = (r1 - mid.astype(F32)).astype(BF16)
        cum = _dot(tri, hi) + _dot(tri, mid) + _dot(tri, lo)
        cl = cum[C - 1:C, :]
        q = q_ref[rs, :].astype(F32) * scale
        k = k_ref[rs, :].astype(F32)
        qd = (q * jnp.exp(cum)).astype(BF16)
        ki = (k * jnp.exp(-cum)).astype(BF16)
        ke = (k * jnp.exp(cl - cum)).astype(BF16)
        dec = jnp.exp(cl)
        for h in range(heads):
            ks = slice(h * hk, (h + 1) * hk)
            vs = slice(h * hv, (h + 1) * hv)
            vh = v_ref[rs, vs]
            s = lax.dot_general(qd[:, ks], ki[:, ks], NT_DIMS, preferred_element_type=F32)
            s = jnp.where(causal, s, 0.0).astype(BF16)
            st = st_ref[h]
            o = _dot(s, vh) + lax.dot_general(qd[:, ks], st.astype(BF16), NT_DIMS,
                                              preferred_element_type=F32)
            st_ref[h] = st * dec[:, ks] + lax.dot_general(vh, ke[:, ks], TN_DIMS,
                                                          preferred_element_type=F32)
            ms = jnp.mean(o * o, axis=-1, keepdims=True)
            on = o * lax.rsqrt(ms + EPS) * g_ref[:, vs]
            rr = r_ref[rs, vs].astype(F32)
            o_ref[rs, vs] = (on * (rr * _sigmoid(rr))).astype(o_ref.dtype)
        return carry

    lax.fori_loop(0, rows // C, chunk_step, 0)


def _gla_call(proj, alr, wa2p, ba, on_g, B, S, W):
    DK = wa2p.shape[1]
    DV = on_g.shape[1]
    H = GLA_HEADS
    Cb = min(512, S)
    nb = S // Cb
    q_blk = (2 * W) // DK
    k_blk = (2 * W + DK) // DK
    v_blk = (2 * W + 2 * DK) // DV
    r_blk = (2 * W + 2 * DK + DV) // DV
    kern = functools.partial(_gla_kernel, heads=H)
    return pl.pallas_call(
        kern,
        grid=(B, nb),
        in_specs=[pl.BlockSpec((Cb, DK), lambda b, i: (b * nb + i, q_blk)),
                  pl.BlockSpec((Cb, DK), lambda b, i: (b * nb + i, k_blk)),
                  pl.BlockSpec((Cb, DV), lambda b, i: (b * nb + i, v_blk)),
                  pl.BlockSpec((Cb, DV), lambda b, i: (b * nb + i, r_blk)),
                  pl.BlockSpec((Cb, LANES), lambda b, i: (b * nb + i, 0)),
                  pl.BlockSpec((LANES, DK), lambda b, i: (0, 0)),
                  pl.BlockSpec((1, DK), lambda b, i: (0, 0)),
                  pl.BlockSpec((1, DV), lambda b, i: (0, 0))],
        out_specs=pl.BlockSpec((Cb, DV), lambda b, i: (b * nb + i, 0)),
        out_shape=jax.ShapeDtypeStruct((B * S, DV), BF16),
        scratch_shapes=[pltpu.VMEM((H, DV // H, DK // H), F32),
                        pltpu.VMEM((Cb, DK), F32)],
        compiler_params=_params(("parallel", "arbitrary"), 32),
        name="gla",
    )(proj, proj, proj, proj, alr, wa2p, ba, on_g)


def _merge_kernel(ya_ref, yb_ref, ga_ref, gb_ref, wpa_ref, wpb_ref, wout_ref, x_ref, g1_ref,
                  n2_ref, sc_ref, sh_ref, wr_ref, x1_ref, h2_ref, lg_ref):
    a = _dot(ya_ref[...], wpa_ref[0])
    b = _dot(yb_ref[...], wpb_ref[0])
    y = _sigmoid(ga_ref[...].astype(F32)) * a + _sigmoid(gb_ref[...].astype(F32)) * b
    out = _dot(y.astype(BF16), wout_ref[0])
    x1 = x_ref[...] + g1_ref[0] * out
    x1_ref[...] = x1
    ms = jnp.mean(x1 * x1, axis=-1, keepdims=True)
    h2 = (x1 * lax.rsqrt(ms + EPS) * n2_ref[...]) * (1.0 + sc_ref[0]) + sh_ref[0]
    _store_packed(h2_ref, h2)
    lg_ref[...] = _dot(h2.astype(BF16), wr_ref[...])


def _merge_call(ya, yb, proj, wpa, wpb, wout, x2, g1, n2g, sc2, sh2, wr, S, W, l):
    T, D = x2.shape
    ga_blk = (proj.shape[1] - 2 * D) // D
    gb_blk = ga_blk + 1
    tm = min(256, S)
    per_b = S // tm
    n_s = D // (2 * LANES)
    const = dict(pipeline_mode=pl.Buffered(1))
    return pl.pallas_call(
        _merge_kernel,
        grid=(T // tm,),
        in_specs=[pl.BlockSpec((tm, W), lambda i: (i, 0)),
                  pl.BlockSpec((tm, W), lambda i: (i, 0)),
                  pl.BlockSpec((tm, D), lambda i: (i, ga_blk)),
                  pl.BlockSpec((tm, D), lambda i: (i, gb_blk)),
                  pl.BlockSpec((1, W, D), lambda i: (l, 0, 0), **const),
                  pl.BlockSpec((1, W, D), lambda i: (l, 0, 0), **const),
                  pl.BlockSpec((1, D, D), lambda i: (l, 0, 0), **const),
                  pl.BlockSpec((tm, D), lambda i: (i, 0)),
                  pl.BlockSpec((1, 1, D), lambda i: (i // per_b, 0, 0)),
                  pl.BlockSpec((1, D), lambda i: (0, 0)),
                  pl.BlockSpec((1, 1, D), lambda i: (i // per_b, 0, 0)),
                  pl.BlockSpec((1, 1, D), lambda i: (i // per_b, 0, 0)),
                  pl.BlockSpec((D, LANES), lambda i: (0, 0), **const)],
        out_specs=[pl.BlockSpec((tm, D), lambda i: (i, 0)),
                   pl.BlockSpec((tm * n_s, LANES), lambda i: (i, 0)),
                   pl.BlockSpec((tm, LANES), lambda i: (i, 0))],
        out_shape=[jax.ShapeDtypeStruct((T, D), F32),
                   jax.ShapeDtypeStruct((T * n_s, LANES), jnp.uint32),
                   jax.ShapeDtypeStruct((T, LANES), F32)],
        compiler_params=_params(("parallel",), 56),
        name="merge_out",
    )(ya, yb, proj, proj, wpa, wpb, wout, x2, g1, n2g, sc2, sh2, wr)


def _router_kernel(lg_ref, b_ref, e_ref, w_ref, cnt_ref, carry, *, groups, per_group):
    @pl.when(pl.program_id(0) == 0)
    def _():
        carry[...] = jnp.zeros(carry.shape, F32)

    lg = lg_ref[...] + b_ref[...]
    lane = lax.broadcasted_iota(jnp.int32, lg.shape, 1)
    lane_f = lane.astype(F32)
    neg = jnp.float32(-1e30)
    big = jnp.float32(LANES)

    def first_argmax(vals):
        m = jnp.max(vals, axis=-1, keepdims=True)
        idx = jnp.min(jnp.where(vals == m, lane_f, big), axis=-1, keepdims=True)
        return m, idx

    gmask = lane < groups
    gl = jnp.where(gmask, lg, neg)
    gmax, gidx = first_argmax(gl)
    gsum = jnp.sum(jnp.where(gmask, jnp.exp(gl - gmax), 0.0), axis=-1, keepdims=True)
    gtop = 1.0 / gsum

    lo = groups + gidx * per_group
    emask = (lane_f >= lo) & (lane_f < lo + per_group)
    el = jnp.where(emask, lg, neg)
    m1, i1 = first_argmax(el)
    el2 = jnp.where(lane_f == i1, neg, el)
    m2, i2 = first_argmax(el2)
    den = jnp.sum(jnp.where(emask, jnp.exp(el - m1), 0.0), axis=-1, keepdims=True)
    p1 = 1.0 / den
    p2 = jnp.exp(m2 - m1) / den
    ps = p1 + p2
    w1 = gtop * p1 / ps
    w2 = gtop * p2 / ps
    e1 = i1 - groups
    e2 = i2 - groups
    w_ref[...] = jnp.where(lane == 0, w1, jnp.where(lane == 1, w2, 0.0))

    oh0 = lane_f == e1
    oh1 = lane_f == e2
    both = oh0.astype(F32) + oh1.astype(F32)
    tb = lg.shape[0]
    r = lax.broadcasted_iota(jnp.int32, (tb, tb), 0)
    c = lax.broadcasted_iota(jnp.int32, (tb, tb), 1)
    strict = (r > c).astype(BF16)
    before = _dot(strict, both.astype(BF16)) + carry[0:1, :]
    r0 = jnp.sum(jnp.where(oh0, before, 0.0), axis=-1, keepdims=True)
    r1 = jnp.sum(jnp.where(oh1, before, 0.0), axis=-1, keepdims=True)
    packed = jnp.where(lane == 0, e1, jnp.where(lane == 1, e2,
                       jnp.where(lane == 2, r0, jnp.where(lane == 3, r1, 0.0))))
    e_ref[...] = packed.astype(jnp.int32)
    carry[...] = carry[...] + jnp.sum(both, axis=0, keepdims=True)
    cnt_ref[...] = carry[...]


def _router_call(logits, bias, per_group):
    T = logits.shape[0]
    tm = min(512, T)
    kern = functools.partial(_router_kernel, groups=MOE_GROUPS, per_group=per_group)
    return pl.pallas_call(
        kern,
        grid=(T // tm,),
        in_specs=[pl.BlockSpec((tm, LANES), lambda i: (i, 0)),
                  pl.BlockSpec((1, LANES), lambda i: (0, 0))],
        out_specs=[pl.BlockSpec((tm, LANES), lambda i: (i, 0)),
                   pl.BlockSpec((tm, LANES), lambda i: (i, 0)),
                   pl.BlockSpec((8, LANES), lambda i: (0, 0))],
        out_shape=[jax.ShapeDtypeStruct((T, LANES), jnp.int32),
                   jax.ShapeDtypeStruct((T, LANES), F32),
                   jax.ShapeDtypeStruct((8, LANES), F32)],
        scratch_shapes=[pltpu.VMEM((8, LANES), F32)],
        compiler_params=_params(("arbitrary",), 16),
        name="router_topk",
    )(logits, bias)


def _row(ref, t, n_s):
    return ref.at[pl.ds(pl.multiple_of(t * n_s, n_s), n_s)]


def _dispatch_kernel(zr_ref, na_ref, dest_ref, h_ref, out_ref, zbuf, sem, zsem, *, n_s, block):
    n_assign = dest_ref.shape[2]
    tm = n_assign // TOP_K
    nb = out_ref.shape[0] // (block * n_s)

    def zero_block(b):
        start = pl.multiple_of(b * (block * n_s), block * n_s)
        return pltpu.make_async_copy(zbuf, out_ref.at[pl.ds(start, block * n_s)], zsem.at[0])

    @pl.when(pl.program_id(0) == 0)
    def _():
        zbuf[...] = jnp.zeros(zbuf.shape, zbuf.dtype)
        n_exp = zr_ref.shape[0]
        lax.fori_loop(0, n_exp, lambda e, c: (zero_block(zr_ref[e]).start(), c)[1], 0)
        lax.fori_loop(na_ref[0], nb, lambda b, c: (zero_block(b).start(), c)[1], 0)
        lax.fori_loop(0, n_exp, lambda e, c: (zero_block(0).wait(), c)[1], 0)
        lax.fori_loop(na_ref[0], nb, lambda b, c: (zero_block(0).wait(), c)[1], 0)

    tok_unroll = ISSUE_UNROLL // TOP_K

    def issue(g, c):
        for u in range(tok_unroll):
            t = g * tok_unroll + u
            for k in range(TOP_K):
                pltpu.make_async_copy(_row(h_ref, t, n_s),
                                      _row(out_ref, dest_ref[0, 0, t * TOP_K + k], n_s),
                                      sem.at[0]).start(priority=k % 2)
        return c
    lax.fori_loop(0, tm // tok_unroll, issue, 0)
    for _ in range(TOP_K):
        pltpu.make_async_copy(h_ref, out_ref.at[pl.ds(0, tm * n_s)], sem.at[0]).wait()


def _dispatch_call(h2p, dest, zero_rows, n_act, n_rows, n_s):
    T = h2p.shape[0] // n_s
    tm = min(1024, T)
    nt = T // tm
    kern = functools.partial(_dispatch_kernel, n_s=n_s, block=MOE_BLOCK)
    grid_spec = pltpu.PrefetchScalarGridSpec(
        num_scalar_prefetch=2,
        grid=(nt,),
        in_specs=[pl.BlockSpec((1, 1, tm * TOP_K), lambda i, zr, na: (i, 0, 0),
                               memory_space=pltpu.SMEM),
                  pl.BlockSpec((tm * n_s, LANES), lambda i, zr, na: (i, 0))],
        out_specs=pl.BlockSpec(memory_space=pl.ANY),
        scratch_shapes=[pltpu.VMEM((MOE_BLOCK * n_s, LANES), h2p.dtype),
                        pltpu.SemaphoreType.DMA((1,)),
                        pltpu.SemaphoreType.DMA((1,))],
    )
    return pl.pallas_call(
        kern,
        grid_spec=grid_spec,
        out_shape=jax.ShapeDtypeStruct((n_rows * n_s, LANES), h2p.dtype),
        compiler_params=_params(("arbitrary",), 16),
        name="row_dispatch",
    )(zero_rows, n_act, dest.reshape(nt, 1, tm * TOP_K), h2p)


def _stage_expert_weights(layer, be_ref, na_ref, nx_ref, i, hbm_refs, stage, bf_refs, sem):
    e = be_ref[i]
    first = (i == 0) | (e != be_ref[jnp.maximum(i - 1, 0)])

    def copies(expert):
        return [pltpu.make_async_copy(w.at[layer, expert], stage[k], sem.at[k])
                for k, w in enumerate(hbm_refs)]

    @pl.when(i == 0)
    def _():
        for cp in copies(e):
            cp.start()

    @pl.when((i < na_ref[0]) & first)
    def _():
        for k, cp in enumerate(copies(e)):
            cp.wait()
            bf_refs[k][...] = stage[k][...].astype(BF16)
        nxt = nx_ref[i]

        @pl.when(nxt >= 0)
        def _():
            for cp in copies(nxt):
                cp.start()


def _expert_kernel(be_ref, na_ref, nx_ref, xs_ref, wg_hbm, wu_hbm, wd_hbm, ys_ref,
                   sg, su, sd, wgb, wub, wdb, sem, *, layer):
    i = pl.program_id(0)
    active = i < na_ref[0]
    _stage_expert_weights(layer, be_ref, na_ref, nx_ref, i, (wg_hbm, wu_hbm, wd_hbm),
                          (sg, su, sd), (wgb, wub, wdb), sem)

    @pl.when(active)
    def _():
        n_s = sd.shape[1] // (2 * LANES)
        rows = xs_ref.shape[0] // n_s
        los, his = _load_packed(xs_ref, rows, n_s)
        x = jnp.concatenate([p.astype(BF16) for p in los + his], axis=1)
        a = _dot(x, wgb[...])
        b = _dot(x, wub[...])
        hid = (a * _sigmoid(a) * b).astype(BF16)
        _store_packed(ys_ref, _dot(hid, wdb[...]))

    @pl.when(jnp.logical_not(active))
    def _():
        ys_ref[...] = jnp.zeros(ys_ref.shape, ys_ref.dtype)


def _experts_call(xs, block_e, n_act, next_e, w_gate, w_up, w_down, l):
    D, DE = w_gate.shape[-2], w_gate.shape[-1]
    n_s = D // (2 * LANES)
    n_rows = xs.shape[0] // n_s
    BM = MOE_BLOCK
    nb = n_rows // BM
    any_spec = pl.BlockSpec(memory_space=pl.ANY)
    grid_spec = pltpu.PrefetchScalarGridSpec(
        num_scalar_prefetch=3,
        grid=(nb,),
        in_specs=[pl.BlockSpec((BM * n_s, LANES),
                               lambda i, be, na, nx: (jnp.minimum(i, na[0] - 1), 0)),
                  any_spec, any_spec, any_spec],
        out_specs=pl.BlockSpec((BM * n_s, LANES), lambda i, be, na, nx: (i, 0)),
        scratch_shapes=[pltpu.VMEM((D, DE), F32), pltpu.VMEM((D, DE), F32), pltpu.VMEM((DE, D), F32),
                        pltpu.VMEM((D, DE), BF16), pltpu.VMEM((D, DE), BF16),
                        pltpu.VMEM((DE, D), BF16),
                        pltpu.SemaphoreType.DMA((3,))],
    )
    return pl.pallas_call(
        functools.partial(_expert_kernel, layer=l),
        grid_spec=grid_spec,
        out_shape=jax.ShapeDtypeStruct((n_rows * n_s, LANES), jnp.uint32),
        compiler_params=_params(("arbitrary",), 56),
        name="expert_mlp",
    )(block_e, n_act, next_e, xs, w_gate, w_up, w_down)


def _combine_kernel(cur_ref, nxt_ref, x_ref, ys_ref, w_ref, g2_ref, fg_ref, o_ref, ybuf0, ybuf1, sem,
                    *, final_norm):
    i = pl.program_id(0)
    n = pl.num_programs(0)
    tm, d = x_ref.shape
    half = d // 2
    n_s = half // LANES
    n_assign = tm * TOP_K
    bufs = (ybuf0, ybuf1)

    def row_copy(idx_ref, r, s):
        return pltpu.make_async_copy(_row(ys_ref, idx_ref[0, 0, r], n_s), _row(bufs[s], r, n_s),
                                     sem.at[s])

    def wait_all(s):
        pltpu.make_async_copy(ys_ref.at[pl.ds(0, n_assign * n_s)], bufs[s], sem.at[s]).wait()

    @pl.when(i == 0)
    def _():
        def issue(g, c):
            for u in range(ISSUE_UNROLL):
                row_copy(cur_ref, g * ISSUE_UNROLL + u, 0).start(priority=u % 2)
            return c
        lax.fori_loop(0, n_assign // ISSUE_UNROLL, issue, 0)

    def step(s):
        wait_all(s)
        for r in range(n_assign):
            row_copy(nxt_ref, r, 1 - s).start(priority=r % 2)
        yg_ref = bufs[s]
        w = w_ref[...]
        w0 = jnp.broadcast_to(w[:, 0:1], (tm, LANES))
        w1 = jnp.broadcast_to(w[:, 1:2], (tm, LANES))
        lo0, hi0 = _load_packed(yg_ref, tm, n_s, offset=0, group=TOP_K)
        lo1, hi1 = _load_packed(yg_ref, tm, n_s, offset=1, group=TOP_K)
        ss = jnp.zeros((tm, 1), F32)
        for c in range(n_s):
            for base, p0, p1 in ((0, lo0[c], lo1[c]), (half, hi0[c], hi1[c])):
                cs = slice(base + c * LANES, base + (c + 1) * LANES)
                xv = x_ref[:, cs] + g2_ref[0, :, cs] * (w0 * p0 + w1 * p1)
                o_ref[:, cs] = xv
                if final_norm:
                    ss = ss + jnp.sum(xv * xv, axis=-1, keepdims=True)
        if final_norm:
            o_ref[...] = o_ref[...] * lax.rsqrt(ss * (1.0 / d) + EPS) * fg_ref[...]

        @pl.when(i == n - 1)
        def _():
            wait_all(1 - s)

    for s in range(2):
        pl.when(i % 2 == s)(functools.partial(step, s))


def _combine_call(x1, ys, dest, wts, g2, final_g, S, final_norm):
    T, D = x1.shape
    tm = min(512, S)
    nt = T // tm
    per_b = S // tm
    n_s = D // (2 * LANES)
    dest3 = dest.reshape(nt, 1, tm * TOP_K)
    kern = functools.partial(_combine_kernel, final_norm=final_norm)
    return pl.pallas_call(
        kern,
        grid=(nt,),
        in_specs=[pl.BlockSpec((1, 1, tm * TOP_K), lambda i: (i, 0, 0), memory_space=pltpu.SMEM),
                  pl.BlockSpec((1, 1, tm * TOP_K), lambda i: (jnp.minimum(i + 1, nt - 1), 0, 0),
                               memory_space=pltpu.SMEM),
                  pl.BlockSpec((tm, D), lambda i: (i, 0)),
                  pl.BlockSpec(memory_space=pl.ANY),
                  pl.BlockSpec((tm, LANES), lambda i: (i, 0)),
                  pl.BlockSpec((1, 1, D), lambda i: (i // per_b, 0, 0)),
                  pl.BlockSpec((1, D), lambda i: (0, 0))],
        out_specs=pl.BlockSpec((tm, D), lambda i: (i, 0)),
        out_shape=jax.ShapeDtypeStruct((T, D), F32),
        scratch_shapes=[pltpu.VMEM((tm * TOP_K * n_s, LANES), jnp.uint32),
                        pltpu.VMEM((tm * TOP_K * n_s, LANES), jnp.uint32),
                        pltpu.SemaphoreType.DMA((2,))],
        compiler_params=_params(("arbitrary",), 48),
        name="moe_combine",
    )(dest3, dest3, x1, ys, wts, g2, final_g)


def _dispatch_plan(routed, cnt, n_experts):
    T = routed.shape[0]
    A = T * TOP_K
    BM = MOE_BLOCK
    eid = routed[:, :TOP_K]
    rank = routed[:, TOP_K:2 * TOP_K]
    counts = cnt[0, :n_experts].astype(jnp.int32)
    padded = (counts + BM - 1) // BM * BM
    pad_end = jnp.cumsum(padded)
    pad_start = pad_end - padded
    onehot = eid[:, :, None] == jnp.arange(n_experts, dtype=jnp.int32)[None, None, :]
    dest = (jnp.sum(jnp.where(onehot, pad_start[None, None, :], 0), axis=-1)
            + rank).reshape(A)
    nb = (A + n_experts * (BM - 1) + BM - 1) // BM
    block_start = jnp.arange(nb, dtype=jnp.int32) * BM
    block_e = jnp.minimum(jnp.sum(block_start[:, None] >= pad_end[None, :], axis=1),
                          n_experts - 1).astype(jnp.int32)
    n_act = (pad_end[-1] // BM).astype(jnp.int32).reshape(1)
    last_block = jnp.maximum(pad_end // BM - 1, 0).astype(jnp.int32)
    ex = jnp.arange(n_experts, dtype=jnp.int32)
    later = (ex[None, :] > ex[:, None]) & (counts[None, :] > 0)
    next_of = jnp.min(jnp.where(later, ex[None, :], n_experts), axis=1)
    next_of = jnp.where(next_of < n_experts, next_of, -1)
    next_e = jnp.sum(jnp.where(block_e[:, None] == ex[None, :], next_of[None, :], 0),
                     axis=1).astype(jnp.int32)
    return dest.astype(jnp.int32), block_e, n_act, next_e, last_block, nb * BM


def kernel(x, c, ada_w, ada_b, norm1_g, w_in, gm_vn_g, gm_vn_b, gm_ws, gm_bs, gla_wa2, gla_ba,
           gla_on_g, w_pa, w_pb, w_out, norm2_g, w_rg, b_rg, w_re, b_re, w_e_gate, w_e_up,
           w_e_down, final_g):
    B, S, D = x.shape
    T = B * S
    L = ada_w.shape[0]
    W = gm_vn_g.shape[1]
    G, C = gm_ws.shape[1], gm_ws.shape[2]
    RANK, DK = gla_wa2.shape[1], gla_wa2.shape[2]
    DV = gla_on_g.shape[1]
    E = w_e_gate.shape[1]
    per_group = E // MOE_GROUPS
    SUB = D // (2 * LANES)
    assert B <= 8 and MOE_GROUPS + E <= LANES and RANK <= LANES

    c8 = jnp.zeros((8, D), F32).at[:B].set(c)
    mod = _ada_call(c8, ada_w, ada_b.reshape(L, 1, 6 * D))

    o_alr = 2 * W + 2 * DK + 2 * DV
    x2 = x.reshape(T, D)
    w_main, w_alr = _inproj_weight_call(w_in, o_alr, RANK)
    w_pa_b, w_pb_b, w_out_b = _cast_call(w_pa), _cast_call(w_pb), _cast_call(w_out)
    for l in range(L):
        sh1, sc1, g1, sh2, sc2, g2 = [mod[l, :B, k * D:(k + 1) * D].reshape(B, 1, D) for k in range(6)]
        wa2p = jnp.zeros((LANES, DK), BF16).at[:RANK].set(gla_wa2[l].astype(BF16))
        bias_full = jnp.repeat(gm_bs[l].T, W // G, axis=1)
        wr = jnp.zeros((D, LANES), BF16).at[:, :MOE_GROUPS + E].set(
            jnp.concatenate([w_rg[l], w_re[l]], axis=1).astype(BF16))
        br = jnp.zeros((1, LANES), F32).at[0, :MOE_GROUPS + E].set(
            jnp.concatenate([b_rg[l], b_re[l]]))

        proj, alr = _inproj_call(x2, norm1_g[l].reshape(1, D), sc1, sh1, w_main, w_alr, S, l)
        ya = _gmlp_call(proj, gm_vn_g[l].reshape(1, W), gm_vn_b[l].reshape(1, W), gm_ws[l], bias_full)
        yb = _gla_call(proj, alr, wa2p, gla_ba[l].reshape(1, DK), gla_on_g[l].reshape(1, DV), B, S, W)
        x1, h2, logits = _merge_call(ya, yb, proj, w_pa_b, w_pb_b, w_out_b, x2, g1,
                                     norm2_g[l].reshape(1, D), sc2, sh2, wr, S, W, l)
        routed, wts, cnt = _router_call(logits, br, per_group)
        dest, block_e, n_act, next_e, last_block, n_rows = _dispatch_plan(routed, cnt, E)
        xs = _dispatch_call(h2, dest, last_block, n_act, n_rows, SUB)
        ys = _experts_call(xs, block_e, n_act, next_e, w_e_gate, w_e_up, w_e_down, l)
        x2 = _combine_call(x1, ys, dest, wts, g2, final_g.reshape(1, D), S, final_norm=(l == L - 1))
    return x2.reshape(B, S, D)
```

```python
import functools

import jax
import jax.numpy as jnp
from jax import lax
from jax.experimental import pallas as pl
from jax.experimental.pallas import tpu as pltpu

F32 = jnp.float32
BF16 = jnp.bfloat16

EPS = 1e-6
GLA_HEADS = 4
GLA_TAU = 16.0
GLA_CHUNK = 64
MOE_GROUPS = 4
TOP_K = 2

LANES = 128
MIB = 1024 * 1024
MOE_BLOCK = 256
ISSUE_UNROLL = 8

NT_DIMS = (((1,), (1,)), ((), ()))
TN_DIMS = (((0,), (0,)), ((), ()))


def _params(semantics, vmem_mib):
    return pltpu.CompilerParams(dimension_semantics=semantics,
                                vmem_limit_bytes=int(vmem_mib * MIB))


def _dot(a, b):
    return jnp.dot(a, b, preferred_element_type=F32)


def _sigmoid(x):
    return 1.0 / (1.0 + jnp.exp(-x))


def _gelu(x):
    return 0.5 * x * (1.0 + lax.erf(x * (2.0 ** -0.5)))


def _pack_pairs(lo, hi):
    lo_w = pltpu.bitcast(lo.astype(BF16).astype(F32), jnp.uint32) >> 16
    hi_w = pltpu.bitcast(hi.astype(BF16).astype(F32), jnp.uint32) & jnp.uint32(0xFFFF0000)
    return lo_w | hi_w


def _unpack_pairs(w):
    lo = pltpu.bitcast(w << 16, F32)
    hi = pltpu.bitcast(w & jnp.uint32(0xFFFF0000), F32)
    return lo, hi


def _store_packed(ref, x):
    rows, d = x.shape
    half = d // 2
    n_s = half // LANES
    for s in range(n_s):
        lo = x[:, s * LANES:(s + 1) * LANES]
        hi = x[:, half + s * LANES:half + (s + 1) * LANES]
        ref[pl.ds(s, rows, stride=n_s), :] = _pack_pairs(lo, hi)


def _load_packed(ref, rows, n_s, offset=0, group=1):
    los, his = [], []
    for s in range(n_s):
        lo, hi = _unpack_pairs(ref[pl.ds(offset * n_s + s, rows, stride=n_s * group), :])
        los.append(lo)
        his.append(hi)
    return los, his


def _ada_kernel(c_ref, w_ref, b_ref, o_ref):
    c = c_ref[...]
    cond = c * _sigmoid(c)
    o_ref[0] = _dot(cond.astype(BF16), w_ref[0].astype(BF16)) + b_ref[0]


def _ada_call(c8, ada_w, ada_b3):
    L, D, N = ada_w.shape
    tn = min(1024, N)
    return pl.pallas_call(
        _ada_kernel,
        grid=(L, N // tn),
        in_specs=[pl.BlockSpec((8, D), lambda l, j: (0, 0)),
                  pl.BlockSpec((1, D, tn), lambda l, j: (l, 0, j)),
                  pl.BlockSpec((1, 1, tn), lambda l, j: (l, 0, j))],
        out_specs=pl.BlockSpec((1, 8, tn), lambda l, j: (l, 0, j)),
        out_shape=jax.ShapeDtypeStruct((L, 8, N), F32),
        compiler_params=_params(("parallel", "parallel"), 40),
        name="ada_mod",
    )(c8, ada_w, ada_b3)


def _cast_kernel(w_ref, o_ref):
    o_ref[...] = w_ref[...].astype(o_ref.dtype)


def _cast_call(w):
    L, K, N = w.shape
    tn = min(1024, N)
    return pl.pallas_call(
        _cast_kernel,
        grid=(L, N // tn),
        in_specs=[pl.BlockSpec((1, K, tn), lambda l, j: (l, 0, j))],
        out_specs=pl.BlockSpec((1, K, tn), lambda l, j: (l, 0, j)),
        out_shape=jax.ShapeDtypeStruct(w.shape, BF16),
        compiler_params=_params(("parallel", "parallel"), 32),
        name="cast_bf16",
    )(w)


def _inproj_weight_kernel(wt_ref, o_ref):
    o_ref[0] = wt_ref[...].T.astype(BF16)


def _inproj_weight_call(w_in, head_cols, rank):
    L, D, N = w_in.shape
    NM = N - rank
    tn = min(1024, NM)
    n_head = head_cols // tn
    sub = 8
    assert head_cols % tn == 0 and NM % tn == 0 and N % sub == 0 and rank % sub == 0
    wt = jnp.swapaxes(w_in, 1, 2).reshape(L * N, D)

    def first_row(l, j):
        return pl.multiple_of(l * N + j * tn + jnp.where(j >= n_head, rank, 0), sub)

    return pl.pallas_call(
        _inproj_weight_kernel,
        grid=(L, NM // tn),
        in_specs=[pl.BlockSpec((pl.Element(tn), pl.Element(D)), lambda l, j: (first_row(l, j), 0))],
        out_specs=pl.BlockSpec((1, D, tn), lambda l, j: (l, 0, j)),
        out_shape=jax.ShapeDtypeStruct((L, D, NM), BF16),
        compiler_params=_params(("parallel", "parallel"), 48),
        name="in_proj_weight",
    )(wt)


def _inproj_kernel(x_ref, g_ref, sc_ref, sh_ref, w_ref, walr_ref, o_ref, alr_ref, h_scr):
    @pl.when(pl.program_id(1) == 0)
    def _():
        x = x_ref[...]
        ms = jnp.mean(x * x, axis=-1, keepdims=True)
        y = x * lax.rsqrt(ms + EPS) * g_ref[...]
        h = (y * (1.0 + sc_ref[0]) + sh_ref[0]).astype(BF16)
        h_scr[...] = h
        alr_ref[...] = _dot(h, walr_ref[0])

    o_ref[...] = _dot(h_scr[...], w_ref[0]).astype(o_ref.dtype)


def _inproj_call(x2, g, sc, sh, w_main, w_alr, S, l):
    T, D = x2.shape
    NM = w_main.shape[2]
    tm = min(1024, S)
    tn = min(1024, NM)
    per_b = S // tm
    return pl.pallas_call(
        _inproj_kernel,
        grid=(T // tm, NM // tn),
        in_specs=[pl.BlockSpec((tm, D), lambda i, j: (i, 0)),
                  pl.BlockSpec((1, D), lambda i, j: (0, 0)),
                  pl.BlockSpec((1, 1, D), lambda i, j: (i // per_b, 0, 0)),
                  pl.BlockSpec((1, 1, D), lambda i, j: (i // per_b, 0, 0)),
                  pl.BlockSpec((1, D, tn), lambda i, j: (l, 0, j)),
                  pl.BlockSpec((1, D, LANES), lambda i, j: (l, 0, 0))],
        out_specs=[pl.BlockSpec((tm, tn), lambda i, j: (i, j)),
                   pl.BlockSpec((tm, LANES), lambda i, j: (i, 0))],
        out_shape=[jax.ShapeDtypeStruct((T, NM), BF16),
                   jax.ShapeDtypeStruct((T, LANES), F32)],
        scratch_shapes=[pltpu.VMEM((tm, D), BF16)],
        compiler_params=_params(("parallel", "arbitrary"), 56),
        name="in_proj",
    )(x2, g, sc, sh, w_main, w_alr)


def _gmlp_kernel(u_ref, v_ref, g_ref, b_ref, ws_ref, bias_ref, o_ref, vn_scr, *, groups, chunk):
    rows, width = v_ref.shape
    gc = width // groups
    r = lax.broadcasted_iota(jnp.int32, (chunk, chunk), 0)
    c = lax.broadcasted_iota(jnp.int32, (chunk, chunk), 1)
    causal = r >= c
    for ci in range(rows // chunk):
        rs = slice(ci * chunk, (ci + 1) * chunk)
        gv = _gelu(v_ref[rs, :].astype(F32))
        mu = jnp.mean(gv, axis=-1, keepdims=True)
        xc = gv - mu
        var = jnp.mean(xc * xc, axis=-1, keepdims=True)
        vn = xc * lax.rsqrt(var + EPS) * g_ref[...] + b_ref[...]
        vn_scr[...] = vn.astype(BF16)
        for gi in range(groups):
            cs = slice(gi * gc, (gi + 1) * gc)
            wm = jnp.where(causal, ws_ref[gi], 0.0).astype(BF16)
            mixed = _dot(wm, vn_scr[:, cs]) + bias_ref[:, cs]
            gu = _gelu(u_ref[rs, cs].astype(F32))
            o_ref[rs, cs] = (gu * mixed).astype(o_ref.dtype)


def _gmlp_call(proj, vn_g, vn_b, ws, bias_full):
    T = proj.shape[0]
    G, C, _ = ws.shape
    W = vn_g.shape[1]
    R = min(2 * C, T)
    kern = functools.partial(_gmlp_kernel, groups=G, chunk=C)
    return pl.pallas_call(
        kern,
        grid=(T // R,),
        in_specs=[pl.BlockSpec((R, W), lambda i: (i, 0)),
                  pl.BlockSpec((R, W), lambda i: (i, 1)),
                  pl.BlockSpec((1, W), lambda i: (0, 0)),
                  pl.BlockSpec((1, W), lambda i: (0, 0)),
                  pl.BlockSpec((G, C, C), lambda i: (0, 0, 0)),
                  pl.BlockSpec((C, W), lambda i: (0, 0))],
        out_specs=pl.BlockSpec((R, W), lambda i: (i, 0)),
        out_shape=jax.ShapeDtypeStruct((T, W), BF16),
        scratch_shapes=[pltpu.VMEM((C, W), BF16)],
        compiler_params=_params(("parallel",), 32),
        name="gmlp_gate",
    )(proj, proj, vn_g, vn_b, ws, bias_full)


def _gla_kernel(q_ref, k_ref, v_ref, r_ref, alr_ref, wa2_ref, ba_ref, g_ref, o_ref,
                st_ref, la_ref, *, heads):
    rows, dk = q_ref.shape
    dv = v_ref.shape[1]
    hk = dk // heads
    hv = dv // heads
    C = GLA_CHUNK
    scale = hk ** -0.5

    @pl.when(pl.program_id(1) == 0)
    def _():
        st_ref[...] = jnp.zeros(st_ref.shape, F32)

    z = _dot(alr_ref[...].astype(BF16), wa2_ref[...]) + ba_ref[...]
    la_ref[...] = (jnp.minimum(z, 0.0) - jnp.log(1.0 + jnp.exp(-jnp.abs(z)))) * (1.0 / GLA_TAU)

    ri = lax.broadcasted_iota(jnp.int32, (C, C), 0)
    ci = lax.broadcasted_iota(jnp.int32, (C, C), 1)
    causal = ri >= ci
    tri = causal.astype(BF16)

    def chunk_step(c, carry):
        rs = pl.ds(pl.multiple_of(c * C, C), C)
        la = la_ref[rs, :]
        hi = la.astype(BF16)
        r1 = la - hi.astype(F32)
        mid = r1.astype(BF16)
        lo = (r1 - mid.astype(F32)).astype(BF16)
        cum = _dot(tri, hi) + _dot(tri, mid) + _dot(tri, lo)
        cl = cum[C - 1:C, :]
        q = q_ref[rs, :].astype(F32) * scale
        k = k_ref[rs, :].astype(F32)
        qd = (q * jnp.exp(cum)).astype(BF16)
        ki = (k * jnp.exp(-cum)).astype(BF16)
        ke = (k * jnp.exp(cl - cum)).astype(BF16)
        dec = jnp.exp(cl)
        for h in range(heads):
            ks = slice(h * hk, (h + 1) * hk)
            vs = slice(h * hv, (h + 1) * hv)
            vh = v_ref[rs, vs]
            s = lax.dot_general(qd[:, ks], ki[:, ks], NT_DIMS, preferred_element_type=F32)
            s = jnp.where(causal, s, 0.0).astype(BF16)
            st = st_ref[h]
            o = _dot(s, vh) + lax.dot_general(qd[:, ks], st.astype(BF16), NT_DIMS,
                                              preferred_element_type=F32)
            st_ref[h] = st * dec[:, ks] + lax.dot_general(vh, ke[:, ks], TN_DIMS,
                                                          preferred_element_type=F32)
            ms = jnp.mean(o * o, axis=-1, keepdims=True)
            on = o * lax.rsqrt(ms + EPS) * g_ref[:, vs]
            rr = r_ref[rs, vs].astype(F32)
            o_ref[rs, vs] = (on * (rr * _sigmoid(rr))).astype(o_ref.dtype)
        return carry

    lax.fori_loop(0, rows // C, chunk_step, 0)


def _gla_call(proj, alr, wa2p, ba, on_g, B, S, W):
    DK = wa2p.shape[1]
    DV = on_g.shape[1]
    H = GLA_HEADS
    Cb = min(512, S)
    nb = S // Cb
    q_blk = (2 * W) // DK
    k_blk = (2 * W + DK) // DK
    v_blk = (2 * W + 2 * DK) // DV
    r_blk = (2 * W + 2 * DK + DV) // DV
    kern = functools.partial(_gla_kernel, heads=H)
    return pl.pallas_call(
        kern,
        grid=(B, nb),
        in_specs=[pl.BlockSpec((Cb, DK), lambda b, i: (b * nb + i, q_blk)),
                  pl.BlockSpec((Cb, DK), lambda b, i: (b * nb + i, k_blk)),
                  pl.BlockSpec((Cb, DV), lambda b, i: (b * nb + i, v_blk)),
                  pl.BlockSpec((Cb, DV), lambda b, i: (b * nb + i, r_blk)),
                  pl.BlockSpec((Cb, LANES), lambda b, i: (b * nb + i, 0)),
                  pl.BlockSpec((LANES, DK), lambda b, i: (0, 0)),
                  pl.BlockSpec((1, DK), lambda b, i: (0, 0)),
                  pl.BlockSpec((1, DV), lambda b, i: (0, 0))],
        out_specs=pl.BlockSpec((Cb, DV), lambda b, i: (b * nb + i, 0)),
        out_shape=jax.ShapeDtypeStruct((B * S, DV), BF16),
        scratch_shapes=[pltpu.VMEM((H, DV // H, DK // H), F32),
                        pltpu.VMEM((Cb, DK), F32)],
        compiler_params=_params(("parallel", "arbitrary"), 32),
        name="gla",
    )(proj, proj, proj, proj, alr, wa2p, ba, on_g)


def _merge_kernel(ya_ref, yb_ref, ga_ref, gb_ref, wpa_ref, wpb_ref, wout_ref, x_ref, g1_ref,
                  n2_ref, sc_ref, sh_ref, wr_ref, x1_ref, h2_ref, lg_ref):
    a = _dot(ya_ref[...], wpa_ref[0])
    b = _dot(yb_ref[...], wpb_ref[0])
    y = _sigmoid(ga_ref[...].astype(F32)) * a + _sigmoid(gb_ref[...].astype(F32)) * b
    out = _dot(y.astype(BF16), wout_ref[0])
    x1 = x_ref[...] + g1_ref[0] * out
    x1_ref[...] = x1
    ms = jnp.mean(x1 * x1, axis=-1, keepdims=True)
    h2 = (x1 * lax.rsqrt(ms + EPS) * n2_ref[...]) * (1.0 + sc_ref[0]) + sh_ref[0]
    _store_packed(h2_ref, h2)
    lg_ref[...] = _dot(h2.astype(BF16), wr_ref[...])


def _merge_call(ya, yb, proj, wpa, wpb, wout, x2, g1, n2g, sc2, sh2, wr, S, W, l):
    T, D = x2.shape
    ga_blk = (proj.shape[1] - 2 * D) // D
    gb_blk = ga_blk + 1
    tm = min(256, S)
    per_b = S // tm
    n_s = D // (2 * LANES)
    const = dict(pipeline_mode=pl.Buffered(1))
    return pl.pallas_call(
        _merge_kernel,
        grid=(T // tm,),
        in_specs=[pl.BlockSpec((tm, W), lambda i: (i, 0)),
                  pl.BlockSpec((tm, W), lambda i: (i, 0)),
                  pl.BlockSpec((tm, D), lambda i: (i, ga_blk)),
                  pl.BlockSpec((tm, D), lambda i: (i, gb_blk)),
                  pl.BlockSpec((1, W, D), lambda i: (l, 0, 0), **const),
                  pl.BlockSpec((1, W, D), lambda i: (l, 0, 0), **const),
                  pl.BlockSpec((1, D, D), lambda i: (l, 0, 0), **const),
                  pl.BlockSpec((tm, D), lambda i: (i, 0)),
                  pl.BlockSpec((1, 1, D), lambda i: (i // per_b, 0, 0)),
                  pl.BlockSpec((1, D), lambda i: (0, 0)),
                  pl.BlockSpec((1, 1, D), lambda i: (i // per_b, 0, 0)),
                  pl.BlockSpec((1, 1, D), lambda i: (i // per_b, 0, 0)),
                  pl.BlockSpec((D, LANES), lambda i: (0, 0), **const)],
        out_specs=[pl.BlockSpec((tm, D), lambda i: (i, 0)),
                   pl.BlockSpec((tm * n_s, LANES), lambda i: (i, 0)),
                   pl.BlockSpec((tm, LANES), lambda i: (i, 0))],
        out_shape=[jax.ShapeDtypeStruct((T, D), F32),
                   jax.ShapeDtypeStruct((T * n_s, LANES), jnp.uint32),
                   jax.ShapeDtypeStruct((T, LANES), F32)],
        compiler_params=_params(("parallel",), 56),
        name="merge_out",
    )(ya, yb, proj, proj, wpa, wpb, wout, x2, g1, n2g, sc2, sh2, wr)


def _router_kernel(lg_ref, b_ref, e_ref, w_ref, cnt_ref, carry, *, groups, per_group):
    @pl.when(pl.program_id(0) == 0)
    def _():
        carry[...] = jnp.zeros(carry.shape, F32)

    lg = lg_ref[...] + b_ref[...]
    lane = lax.broadcasted_iota(jnp.int32, lg.shape, 1)
    lane_f = lane.astype(F32)
    neg = jnp.float32(-1e30)
    big = jnp.float32(LANES)

    def first_argmax(vals):
        m = jnp.max(vals, axis=-1, keepdims=True)
        idx = jnp.min(jnp.where(vals == m, lane_f, big), axis=-1, keepdims=True)
        return m, idx

    gmask = lane < groups
    gl = jnp.where(gmask, lg, neg)
    gmax, gidx = first_argmax(gl)
    gsum = jnp.sum(jnp.where(gmask, jnp.exp(gl - gmax), 0.0), axis=-1, keepdims=True)
    gtop = 1.0 / gsum

    lo = groups + gidx * per_group
    emask = (lane_f >= lo) & (lane_f < lo + per_group)
    el = jnp.where(emask, lg, neg)
    m1, i1 = first_argmax(el)
    el2 = jnp.where(lane_f == i1, neg, el)
    m2, i2 = first_argmax(el2)
    den = jnp.sum(jnp.where(emask, jnp.exp(el - m1), 0.0), axis=-1, keepdims=True)
    p1 = 1.0 / den
    p2 = jnp.exp(m2 - m1) / den
    ps = p1 + p2
    w1 = gtop * p1 / ps
    w2 = gtop * p2 / ps
    e1 = i1 - groups
    e2 = i2 - groups
    w_ref[...] = jnp.where(lane == 0, w1, jnp.where(lane == 1, w2, 0.0))

    oh0 = lane_f == e1
    oh1 = lane_f == e2
    both = oh0.astype(F32) + oh1.astype(F32)
    tb = lg.shape[0]
    r = lax.broadcasted_iota(jnp.int32, (tb, tb), 0)
    c = lax.broadcasted_iota(jnp.int32, (tb, tb), 1)
    strict = (r > c).astype(BF16)
    before = _dot(strict, both.astype(BF16)) + carry[0:1, :]
    r0 = jnp.sum(jnp.where(oh0, before, 0.0), axis=-1, keepdims=True)
    r1 = jnp.sum(jnp.where(oh1, before, 0.0), axis=-1, keepdims=True)
    packed = jnp.where(lane == 0, e1, jnp.where(lane == 1, e2,
                       jnp.where(lane == 2, r0, jnp.where(lane == 3, r1, 0.0))))
    e_ref[...] = packed.astype(jnp.int32)
    carry[...] = carry[...] + jnp.sum(both, axis=0, keepdims=True)
    cnt_ref[...] = carry[...]


def _router_call(logits, bias, per_group):
    T = logits.shape[0]
    tm = min(512, T)
    kern = functools.partial(_router_kernel, groups=MOE_GROUPS, per_group=per_group)
    return pl.pallas_call(
        kern,
        grid=(T // tm,),
        in_specs=[pl.BlockSpec((tm, LANES), lambda i: (i, 0)),
                  pl.BlockSpec((1, LANES), lambda i: (0, 0))],
        out_specs=[pl.BlockSpec((tm, LANES), lambda i: (i, 0)),
                   pl.BlockSpec((tm, LANES), lambda i: (i, 0)),
                   pl.BlockSpec((8, LANES), lambda i: (0, 0))],
        out_shape=[jax.ShapeDtypeStruct((T, LANES), jnp.int32),
                   jax.ShapeDtypeStruct((T, LANES), F32),
                   jax.ShapeDtypeStruct((8, LANES), F32)],
        scratch_shapes=[pltpu.VMEM((8, LANES), F32)],
        compiler_params=_params(("arbitrary",), 16),
        name="router_topk",
    )(logits, bias)


def _row(ref, t, n_s):
    return ref.at[pl.ds(pl.multiple_of(t * n_s, n_s), n_s)]


def _dispatch_kernel(zr_ref, na_ref, dest_ref, h_ref, out_ref, zbuf, sem, zsem, *, n_s, block):
    n_assign = dest_ref.shape[2]
    tm = n_assign // TOP_K
    nb = out_ref.shape[0] // (block * n_s)

    def zero_block(b):
        start = pl.multiple_of(b * (block * n_s), block * n_s)
        return pltpu.make_async_copy(zbuf, out_ref.at[pl.ds(start, block * n_s)], zsem.at[0])

    @pl.when(pl.program_id(0) == 0)
    def _():
        zbuf[...] = jnp.zeros(zbuf.shape, zbuf.dtype)
        n_exp = zr_ref.shape[0]
        lax.fori_loop(0, n_exp, lambda e, c: (zero_block(zr_ref[e]).start(), c)[1], 0)
        lax.fori_loop(na_ref[0], nb, lambda b, c: (zero_block(b).start(), c)[1], 0)
        lax.fori_loop(0, n_exp, lambda e, c: (zero_block(0).wait(), c)[1], 0)
        lax.fori_loop(na_ref[0], nb, lambda b, c: (zero_block(0).wait(), c)[1], 0)

    tok_unroll = ISSUE_UNROLL // TOP_K

    def issue(g, c):
        for u in range(tok_unroll):
            t = g * tok_unroll + u
            for k in range(TOP_K):
                pltpu.make_async_copy(_row(h_ref, t, n_s),
                                      _row(out_ref, dest_ref[0, 0, t * TOP_K + k], n_s),
                                      sem.at[0]).start(priority=k % 2)
        return c
    lax.fori_loop(0, tm // tok_unroll, issue, 0)
    for _ in range(TOP_K):
        pltpu.make_async_copy(h_ref, out_ref.at[pl.ds(0, tm * n_s)], sem.at[0]).wait()


def _dispatch_call(h2p, dest, zero_rows, n_act, n_rows, n_s):
    T = h2p.shape[0] // n_s
    tm = min(1024, T)
    nt = T // tm
    kern = functools.partial(_dispatch_kernel, n_s=n_s, block=MOE_BLOCK)
    grid_spec = pltpu.PrefetchScalarGridSpec(
        num_scalar_prefetch=2,
        grid=(nt,),
        in_specs=[pl.BlockSpec((1, 1, tm * TOP_K), lambda i, zr, na: (i, 0, 0),
                               memory_space=pltpu.SMEM),
                  pl.BlockSpec((tm * n_s, LANES), lambda i, zr, na: (i, 0))],
        out_specs=pl.BlockSpec(memory_space=pl.ANY),
        scratch_shapes=[pltpu.VMEM((MOE_BLOCK * n_s, LANES), h2p.dtype),
                        pltpu.SemaphoreType.DMA((1,)),
                        pltpu.SemaphoreType.DMA((1,))],
    )
    return pl.pallas_call(
        kern,
        grid_spec=grid_spec,
        out_shape=jax.ShapeDtypeStruct((n_rows * n_s, LANES), h2p.dtype),
        compiler_params=_params(("arbitrary",), 16),
        name="row_dispatch",
    )(zero_rows, n_act, dest.reshape(nt, 1, tm * TOP_K), h2p)


def _stage_expert_weights(layer, be_ref, na_ref, nx_ref, i, hbm_refs, stage, bf_refs, sem):
    e = be_ref[i]
    first = (i == 0) | (e != be_ref[jnp.maximum(i - 1, 0)])

    def copies(expert):
        return [pltpu.make_async_copy(w.at[layer, expert], stage[k], sem.at[k])
                for k, w in enumerate(hbm_refs)]

    @pl.when(i == 0)
    def _():
        for cp in copies(e):
            cp.start()

    @pl.when((i < na_ref[0]) & first)
    def _():
        for k, cp in enumerate(copies(e)):
            cp.wait()
            bf_refs[k][...] = stage[k][...].astype(BF16)
        nxt = nx_ref[i]

        @pl.when(nxt >= 0)
        def _():
            for cp in copies(nxt):
                cp.start()


def _expert_kernel(be_ref, na_ref, nx_ref, xs_ref, wg_hbm, wu_hbm, wd_hbm, ys_ref,
                   sg, su, sd, wgb, wub, wdb, sem, *, layer):
    i = pl.program_id(0)
    active = i < na_ref[0]
    _stage_expert_weights(layer, be_ref, na_ref, nx_ref, i, (wg_hbm, wu_hbm, wd_hbm),
                          (sg, su, sd), (wgb, wub, wdb), sem)

    @pl.when(active)
    def _():
        n_s = sd.shape[1] // (2 * LANES)
        rows = xs_ref.shape[0] // n_s
        los, his = _load_packed(xs_ref, rows, n_s)
        x = jnp.concatenate([p.astype(BF16) for p in los + his], axis=1)
        a = _dot(x, wgb[...])
        b = _dot(x, wub[...])
        hid = (a * _sigmoid(a) * b).astype(BF16)
        _store_packed(ys_ref, _dot(hid, wdb[...]))

    @pl.when(jnp.logical_not(active))
    def _():
        ys_ref[...] = jnp.zeros(ys_ref.shape, ys_ref.dtype)


def _experts_call(xs, block_e, n_act, next_e, w_gate, w_up, w_down, l):
    D, DE = w_gate.shape[-2], w_gate.shape[-1]
    n_s = D // (2 * LANES)
    n_rows = xs.shape[0] // n_s
    BM = MOE_BLOCK
    nb = n_rows // BM
    any_spec = pl.BlockSpec(memory_space=pl.ANY)
    grid_spec = pltpu.PrefetchScalarGridSpec(
        num_scalar_prefetch=3,
        grid=(nb,),
        in_specs=[pl.BlockSpec((BM * n_s, LANES),
                               lambda i, be, na, nx: (jnp.minimum(i, na[0] - 1), 0)),
                  any_spec, any_spec, any_spec],
        out_specs=pl.BlockSpec((BM * n_s, LANES), lambda i, be, na, nx: (i, 0)),
        scratch_shapes=[pltpu.VMEM((D, DE), F32), pltpu.VMEM((D, DE), F32), pltpu.VMEM((DE, D), F32),
                        pltpu.VMEM((D, DE), BF16), pltpu.VMEM((D, DE), BF16),
                        pltpu.VMEM((DE, D), BF16),
                        pltpu.SemaphoreType.DMA((3,))],
    )
    return pl.pallas_call(
        functools.partial(_expert_kernel, layer=l),
        grid_spec=grid_spec,
        out_shape=jax.ShapeDtypeStruct((n_rows * n_s, LANES), jnp.uint32),
        compiler_params=_params(("arbitrary",), 56),
        name="expert_mlp",
    )(block_e, n_act, next_e, xs, w_gate, w_up, w_down)


def _combine_kernel(cur_ref, nxt_ref, x_ref, ys_ref, w_ref, g2_ref, fg_ref, o_ref, ybuf0, ybuf1, sem,
                    *, final_norm):
    i = pl.program_id(0)
    n = pl.num_programs(0)
    tm, d = x_ref.shape
    half = d // 2
    n_s = half // LANES
    n_assign = tm * TOP_K
    bufs = (ybuf0, ybuf1)

    def row_copy(idx_ref, r, s):
        return pltpu.make_async_copy(_row(ys_ref, idx_ref[0, 0, r], n_s), _row(bufs[s], r, n_s),
                                     sem.at[s])

    def wait_all(s):
        pltpu.make_async_copy(ys_ref.at[pl.ds(0, n_assign * n_s)], bufs[s], sem.at[s]).wait()

    @pl.when(i == 0)
    def _():
        def issue(g, c):
            for u in range(ISSUE_UNROLL):
                row_copy(cur_ref, g * ISSUE_UNROLL + u, 0).start(priority=u % 2)
            return c
        lax.fori_loop(0, n_assign // ISSUE_UNROLL, issue, 0)

    def step(s):
        wait_all(s)
        for r in range(n_assign):
            row_copy(nxt_ref, r, 1 - s).start(priority=r % 2)
        yg_ref = bufs[s]
        w = w_ref[...]
        w0 = jnp.broadcast_to(w[:, 0:1], (tm, LANES))
        w1 = jnp.broadcast_to(w[:, 1:2], (tm, LANES))
        lo0, hi0 = _load_packed(yg_ref, tm, n_s, offset=0, group=TOP_K)
        lo1, hi1 = _load_packed(yg_ref, tm, n_s, offset=1, group=TOP_K)
        ss = jnp.zeros((tm, 1), F32)
        for c in range(n_s):
            for base, p0, p1 in ((0, lo0[c], lo1[c]), (half, hi0[c], hi1[c])):
                cs = slice(base + c * LANES, base + (c + 1) * LANES)
                xv = x_ref[:, cs] + g2_ref[0, :, cs] * (w0 * p0 + w1 * p1)
                o_ref[:, cs] = xv
                if final_norm:
                    ss = ss + jnp.sum(xv * xv, axis=-1, keepdims=True)
        if final_norm:
            o_ref[...] = o_ref[...] * lax.rsqrt(ss * (1.0 / d) + EPS) * fg_ref[...]

        @pl.when(i == n - 1)
        def _():
            wait_all(1 - s)

    for s in range(2):
        pl.when(i % 2 == s)(functools.partial(step, s))


def _combine_call(x1, ys, dest, wts, g2, final_g, S, final_norm):
    T, D = x1.shape
    tm = min(512, S)
    nt = T // tm
    per_b = S // tm
    n_s = D // (2 * LANES)
    dest3 = dest.reshape(nt, 1, tm * TOP_K)
    kern = functools.partial(_combine_kernel, final_norm=final_norm)
    return pl.pallas_call(
        kern,
        grid=(nt,),
        in_specs=[pl.BlockSpec((1, 1, tm * TOP_K), lambda i: (i, 0, 0), memory_space=pltpu.SMEM),
                  pl.BlockSpec((1, 1, tm * TOP_K), lambda i: (jnp.minimum(i + 1, nt - 1), 0, 0),
                               memory_space=pltpu.SMEM),
                  pl.BlockSpec((tm, D), lambda i: (i, 0)),
                  pl.BlockSpec(memory_space=pl.ANY),
                  pl.BlockSpec((tm, LANES), lambda i: (i, 0)),
                  pl.BlockSpec((1, 1, D), lambda i: (i // per_b, 0, 0)),
                  pl.BlockSpec((1, D), lambda i: (0, 0))],
        out_specs=pl.BlockSpec((tm, D), lambda i: (i, 0)),
        out_shape=jax.ShapeDtypeStruct((T, D), F32),
        scratch_shapes=[pltpu.VMEM((tm * TOP_K * n_s, LANES), jnp.uint32),
                        pltpu.VMEM((tm * TOP_K * n_s, LANES), jnp.uint32),
                        pltpu.SemaphoreType.DMA((2,))],
        compiler_params=_params(("arbitrary",), 48),
        name="moe_combine",
    )(dest3, dest3, x1, ys, wts, g2, final_g)


def _dispatch_plan(routed, cnt, n_experts):
    T = routed.shape[0]
    A = T * TOP_K
    BM = MOE_BLOCK
    eid = routed[:, :TOP_K]
    rank = routed[:, TOP_K:2 * TOP_K]
    counts = cnt[0, :n_experts].astype(jnp.int32)
    padded = (counts + BM - 1) // BM * BM
    pad_end = jnp.cumsum(padded)
    pad_start = pad_end - padded
    onehot = eid[:, :, None] == jnp.arange(n_experts, dtype=jnp.int32)[None, None, :]
    dest = (jnp.sum(jnp.where(onehot, pad_start[None, None, :], 0), axis=-1)
            + rank).reshape(A)
    nb = (A + n_experts * (BM - 1) + BM - 1) // BM
    block_start = jnp.arange(nb, dtype=jnp.int32) * BM
    block_e = jnp.minimum(jnp.sum(block_start[:, None] >= pad_end[None, :], axis=1),
                          n_experts - 1).astype(jnp.int32)
    n_act = (pad_end[-1] // BM).astype(jnp.int32).reshape(1)
    last_block = jnp.maximum(pad_end // BM - 1, 0).astype(jnp.int32)
    ex = jnp.arange(n_experts, dtype=jnp.int32)
    later = (ex[None, :] > ex[:, None]) & (counts[None, :] > 0)
    next_of = jnp.min(jnp.where(later, ex[None, :], n_experts), axis=1)
    next_of = jnp.where(next_of < n_experts, next_of, -1)
    next_e = jnp.sum(jnp.where(block_e[:, None] == ex[None, :], next_of[None, :], 0),
                     axis=1).astype(jnp.int32)
    return dest.astype(jnp.int32), block_e, n_act, next_e, last_block, nb * BM


def kernel(x, c, ada_w, ada_b, norm1_g, w_in, gm_vn_g, gm_vn_b, gm_ws, gm_bs, gla_wa2, gla_ba,
           gla_on_g, w_pa, w_pb, w_out, norm2_g, w_rg, b_rg, w_re, b_re, w_e_gate, w_e_up,
           w_e_down, final_g):
    B, S, D = x.shape
    T = B * S
    L = ada_w.shape[0]
    W = gm_vn_g.shape[1]
    G, C = gm_ws.shape[1], gm_ws.shape[2]
    RANK, DK = gla_wa2.shape[1], gla_wa2.shape[2]
    DV = gla_on_g.shape[1]
    E = w_e_gate.shape[1]
    per_group = E // MOE_GROUPS
    SUB = D // (2 * LANES)
    assert B <= 8 and MOE_GROUPS + E <= LANES and RANK <= LANES

    c8 = jnp.zeros((8, D), F32).at[:B].set(c)
    mod = _ada_call(c8, ada_w, ada_b.reshape(L, 1, 6 * D))

    o_alr = 2 * W + 2 * DK + 2 * DV
    x2 = x.reshape(T, D)
    w_main = _inproj_weight_call(w_in, o_alr, RANK)
    w_alr = jnp.zeros((L, D, LANES), BF16).at[:, :, :RANK].set(
        w_in[:, :, o_alr:o_alr + RANK].astype(BF16))
    w_pa_b, w_pb_b, w_out_b = _cast_call(w_pa), _cast_call(w_pb), _cast_call(w_out)
    for l in range(L):
        sh1, sc1, g1, sh2, sc2, g2 = [mod[l, :B, k * D:(k + 1) * D].reshape(B, 1, D) for k in range(6)]
        wa2p = jnp.zeros((LANES, DK), BF16).at[:RANK].set(gla_wa2[l].astype(BF16))
        bias_full = jnp.repeat(gm_bs[l].T, W // G, axis=1)
        wr = jnp.zeros((D, LANES), BF16).at[:, :MOE_GROUPS + E].set(
            jnp.concatenate([w_rg[l], w_re[l]], axis=1).astype(BF16))
        br = jnp.zeros((1, LANES), F32).at[0, :MOE_GROUPS + E].set(
            jnp.concatenate([b_rg[l], b_re[l]]))

        proj, alr = _inproj_call(x2, norm1_g[l].reshape(1, D), sc1, sh1, w_main, w_alr, S, l)
        ya = _gmlp_call(proj, gm_vn_g[l].reshape(1, W), gm_vn_b[l].reshape(1, W), gm_ws[l], bias_full)
        yb = _gla_call(proj, alr, wa2p, gla_ba[l].reshape(1, DK), gla_on_g[l].reshape(1, DV), B, S, W)
        x1, h2, logits = _merge_call(ya, yb, proj, w_pa_b, w_pb_b, w_out_b, x2, g1,
                                     norm2_g[l].reshape(1, D), sc2, sh2, wr, S, W, l)
        routed, wts, cnt = _router_call(logits, br, per_group)
        dest, block_e, n_act, next_e, last_block, n_rows = _dispatch_plan(routed, cnt, E)
        xs = _dispatch_call(h2, dest, last_block, n_act, n_rows, SUB)
        ys = _experts_call(xs, block_e, n_act, next_e, w_e_gate, w_e_up, w_e_down, l)
        x2 = _combine_call(x1, ys, dest, wts, g2, final_g.reshape(1, D), S, final_norm=(l == L - 1))
    return x2.reshape(B, S, D)
```

```python
import functools

import jax
import jax.numpy as jnp
from jax import lax
from jax.experimental import pallas as pl
from jax.experimental.pallas import tpu as pltpu

F32 = jnp.float32
BF16 = jnp.bfloat16

EPS = 1e-6
GLA_HEADS = 4
GLA_TAU = 16.0
GLA_CHUNK = 64
MOE_GROUPS = 4
TOP_K = 2

LANES = 128
MIB = 1024 * 1024
MOE_BLOCK = 256
ISSUE_UNROLL = 8

NT_DIMS = (((1,), (1,)), ((), ()))
TN_DIMS = (((0,), (0,)), ((), ()))


def _params(semantics, vmem_mib):
    return pltpu.CompilerParams(dimension_semantics=semantics,
                                vmem_limit_bytes=int(vmem_mib * MIB))


def _dot(a, b):
    return jnp.dot(a, b, preferred_element_type=F32)


def _sigmoid(x):
    return 1.0 / (1.0 + jnp.exp(-x))


def _gelu(x):
    return 0.5 * x * (1.0 + lax.erf(x * (2.0 ** -0.5)))


def _pack_pairs(lo, hi):
    lo_w = pltpu.bitcast(lo.astype(BF16).astype(F32), jnp.uint32) >> 16
    hi_w = pltpu.bitcast(hi.astype(BF16).astype(F32), jnp.uint32) & jnp.uint32(0xFFFF0000)
    return lo_w | hi_w


def _unpack_pairs(w):
    lo = pltpu.bitcast(w << 16, F32)
    hi = pltpu.bitcast(w & jnp.uint32(0xFFFF0000), F32)
    return lo, hi


def _store_packed(ref, x):
    rows, d = x.shape
    half = d // 2
    n_s = half // LANES
    for s in range(n_s):
        lo = x[:, s * LANES:(s + 1) * LANES]
        hi = x[:, half + s * LANES:half + (s + 1) * LANES]
        ref[pl.ds(s, rows, stride=n_s), :] = _pack_pairs(lo, hi)


def _load_packed(ref, rows, n_s, offset=0, group=1):
    los, his = [], []
    for s in range(n_s):
        lo, hi = _unpack_pairs(ref[pl.ds(offset * n_s + s, rows, stride=n_s * group), :])
        los.append(lo)
        his.append(hi)
    return los, his


def _ada_kernel(c_ref, w_ref, b_ref, o_ref):
    c = c_ref[...]
    cond = c * _sigmoid(c)
    o_ref[0] = _dot(cond.astype(BF16), w_ref[0].astype(BF16)) + b_ref[0]


def _ada_call(c8, ada_w, ada_b3):
    L, D, N = ada_w.shape
    tn = min(1024, N)
    return pl.pallas_call(
        _ada_kernel,
        grid=(L, N // tn),
        in_specs=[pl.BlockSpec((8, D), lambda l, j: (0, 0)),
                  pl.BlockSpec((1, D, tn), lambda l, j: (l, 0, j)),
                  pl.BlockSpec((1, 1, tn), lambda l, j: (l, 0, j))],
        out_specs=pl.BlockSpec((1, 8, tn), lambda l, j: (l, 0, j)),
        out_shape=jax.ShapeDtypeStruct((L, 8, N), F32),
        compiler_params=_params(("parallel", "parallel"), 40),
        name="ada_mod",
    )(c8, ada_w, ada_b3)


def _cast_kernel(w_ref, o_ref):
    o_ref[...] = w_ref[...].astype(o_ref.dtype)


def _cast_call(w):
    L, K, N = w.shape
    tn = min(1024, N)
    return pl.pallas_call(
        _cast_kernel,
        grid=(L, N // tn),
        in_specs=[pl.BlockSpec((1, K, tn), lambda l, j: (l, 0, j))],
        out_specs=pl.BlockSpec((1, K, tn), lambda l, j: (l, 0, j)),
        out_shape=jax.ShapeDtypeStruct(w.shape, BF16),
        compiler_params=_params(("parallel", "parallel"), 32),
        name="cast_bf16",
    )(w)


def _inproj_weight_kernel(wt_ref, o_ref):
    o_ref[0] = wt_ref[...].T.astype(BF16)


def _inproj_weight_call(w_in, head_cols, rank):
    L, D, N = w_in.shape
    NM = N - rank
    tn = min(1024, NM)
    n_head = head_cols // tn
    sub = 8
    assert head_cols % tn == 0 and NM % tn == 0 and N % sub == 0 and rank % sub == 0
    wt = jnp.swapaxes(w_in, 1, 2).reshape(L * N, D)

    def first_row(l, j):
        return pl.multiple_of(l * N + j * tn + jnp.where(j >= n_head, rank, 0), sub)

    return pl.pallas_call(
        _inproj_weight_kernel,
        grid=(L, NM // tn),
        in_specs=[pl.BlockSpec((pl.Element(tn), pl.Element(D)), lambda l, j: (first_row(l, j), 0))],
        out_specs=pl.BlockSpec((1, D, tn), lambda l, j: (l, 0, j)),
        out_shape=jax.ShapeDtypeStruct((L, D, NM), BF16),
        compiler_params=_params(("parallel", "parallel"), 48),
        name="in_proj_weight",
    )(wt)


def _inproj_kernel(x_ref, g_ref, sc_ref, sh_ref, w_ref, walr_ref, o_ref, alr_ref, h_scr):
    @pl.when(pl.program_id(1) == 0)
    def _():
        x = x_ref[...]
        ms = jnp.mean(x * x, axis=-1, keepdims=True)
        y = x * lax.rsqrt(ms + EPS) * g_ref[...]
        h = (y * (1.0 + sc_ref[0]) + sh_ref[0]).astype(BF16)
        h_scr[...] = h
        alr_ref[...] = _dot(h, walr_ref[0])

    o_ref[...] = _dot(h_scr[...], w_ref[0]).astype(o_ref.dtype)


def _inproj_call(x2, g, sc, sh, w_main, w_alr, S, l):
    T, D = x2.shape
    NM = w_main.shape[2]
    tm = min(1024, S)
    tn = min(2048, NM)
    per_b = S // tm
    return pl.pallas_call(
        _inproj_kernel,
        grid=(T // tm, NM // tn),
        in_specs=[pl.BlockSpec((tm, D), lambda i, j: (i, 0)),
                  pl.BlockSpec((1, D), lambda i, j: (0, 0)),
                  pl.BlockSpec((1, 1, D), lambda i, j: (i // per_b, 0, 0)),
                  pl.BlockSpec((1, 1, D), lambda i, j: (i // per_b, 0, 0)),
                  pl.BlockSpec((1, D, tn), lambda i, j: (l, 0, j)),
                  pl.BlockSpec((1, D, LANES), lambda i, j: (l, 0, 0))],
        out_specs=[pl.BlockSpec((tm, tn), lambda i, j: (i, j)),
                   pl.BlockSpec((tm, LANES), lambda i, j: (i, 0))],
        out_shape=[jax.ShapeDtypeStruct((T, NM), BF16),
                   jax.ShapeDtypeStruct((T, LANES), F32)],
        scratch_shapes=[pltpu.VMEM((tm, D), BF16)],
        compiler_params=_params(("parallel", "arbitrary"), 56),
        name="in_proj",
    )(x2, g, sc, sh, w_main, w_alr)


def _gmlp_kernel(u_ref, v_ref, g_ref, b_ref, ws_ref, bias_ref, o_ref, vn_scr, *, groups, chunk):
    rows, width = v_ref.shape
    gc = width // groups
    r = lax.broadcasted_iota(jnp.int32, (chunk, chunk), 0)
    c = lax.broadcasted_iota(jnp.int32, (chunk, chunk), 1)
    causal = r >= c
    for ci in range(rows // chunk):
        rs = slice(ci * chunk, (ci + 1) * chunk)
        gv = _gelu(v_ref[rs, :].astype(F32))
        mu = jnp.mean(gv, axis=-1, keepdims=True)
        xc = gv - mu
        var = jnp.mean(xc * xc, axis=-1, keepdims=True)
        vn = xc * lax.rsqrt(var + EPS) * g_ref[...] + b_ref[...]
        vn_scr[...] = vn.astype(BF16)
        for gi in range(groups):
            cs = slice(gi * gc, (gi + 1) * gc)
            wm = jnp.where(causal, ws_ref[gi], 0.0).astype(BF16)
            mixed = _dot(wm, vn_scr[:, cs]) + bias_ref[:, cs]
            gu = _gelu(u_ref[rs, cs].astype(F32))
            o_ref[rs, cs] = (gu * mixed).astype(o_ref.dtype)


def _gmlp_call(proj, vn_g, vn_b, ws, bias_full):
    T = proj.shape[0]
    G, C, _ = ws.shape
    W = vn_g.shape[1]
    R = min(4 * C, T)
    kern = functools.partial(_gmlp_kernel, groups=G, chunk=C)
    return pl.pallas_call(
        kern,
        grid=(T // R,),
        in_specs=[pl.BlockSpec((R, W), lambda i: (i, 0)),
                  pl.BlockSpec((R, W), lambda i: (i, 1)),
                  pl.BlockSpec((1, W), lambda i: (0, 0)),
                  pl.BlockSpec((1, W), lambda i: (0, 0)),
                  pl.BlockSpec((G, C, C), lambda i: (0, 0, 0)),
                  pl.BlockSpec((C, W), lambda i: (0, 0))],
        out_specs=pl.BlockSpec((R, W), lambda i: (i, 0)),
        out_shape=jax.ShapeDtypeStruct((T, W), BF16),
        scratch_shapes=[pltpu.VMEM((C, W), BF16)],
        compiler_params=_params(("parallel",), 32),
        name="gmlp_gate",
    )(proj, proj, vn_g, vn_b, ws, bias_full)


def _gla_kernel(q_ref, k_ref, v_ref, r_ref, alr_ref, wa2_ref, ba_ref, g_ref, o_ref,
                st_ref, la_ref, *, heads):
    rows, dk = q_ref.shape
    dv = v_ref.shape[1]
    hk = dk // heads
    hv = dv // heads
    C = GLA_CHUNK
    scale = hk ** -0.5

    @pl.when(pl.program_id(1) == 0)
    def _():
        st_ref[...] = jnp.zeros(st_ref.shape, F32)

    z = _dot(alr_ref[...].astype(BF16), wa2_ref[...]) + ba_ref[...]
    la_ref[...] = (jnp.minimum(z, 0.0) - jnp.log(1.0 + jnp.exp(-jnp.abs(z)))) * (1.0 / GLA_TAU)

    ri = lax.broadcasted_iota(jnp.int32, (C, C), 0)
    ci = lax.broadcasted_iota(jnp.int32, (C, C), 1)
    causal = ri >= ci
    tri = causal.astype(BF16)

    def chunk_step(c, carry):
        rs = pl.ds(pl.multiple_of(c * C, C), C)
        la = la_ref[rs, :]
        hi = la.astype(BF16)
        r1 = la - hi.astype(F32)
        mid = r1.astype(BF16)
        lo = (r1 - mid.astype(F32)).astype(BF16)
        cum = _dot(tri, hi) + _dot(tri, mid) + _dot(tri, lo)
        cl = cum[C - 1:C, :]
        q = q_ref[rs, :].astype(F32) * scale
        k = k_ref[rs, :].astype(F32)
        qd = (q * jnp.exp(cum)).astype(BF16)
        ki = (k * jnp.exp(-cum)).astype(BF16)
        ke = (k * jnp.exp(cl - cum)).astype(BF16)
        dec = jnp.exp(cl)
        for h in range(heads):
            ks = slice(h * hk, (h + 1) * hk)
            vs = slice(h * hv, (h + 1) * hv)
            vh = v_ref[rs, vs]
            s = lax.dot_general(qd[:, ks], ki[:, ks], NT_DIMS, preferred_element_type=F32)
            s = jnp.where(causal, s, 0.0).astype(BF16)
            st = st_ref[h]
            o = _dot(s, vh) + lax.dot_general(qd[:, ks], st.astype(BF16), NT_DIMS,
                                              preferred_element_type=F32)
            st_ref[h] = st * dec[:, ks] + lax.dot_general(vh, ke[:, ks], TN_DIMS,
                                                          preferred_element_type=F32)
            ms = jnp.mean(o * o, axis=-1, keepdims=True)
            on = o * lax.rsqrt(ms + EPS) * g_ref[:, vs]
            rr = r_ref[rs, vs].astype(F32)
            o_ref[rs, vs] = (on * (rr * _sigmoid(rr))).astype(o_ref.dtype)
        return carry

    lax.fori_loop(0, rows // C, chunk_step, 0)


def _gla_call(proj, alr, wa2p, ba, on_g, B, S, W):
    DK = wa2p.shape[1]
    DV = on_g.shape[1]
    H = GLA_HEADS
    Cb = min(512, S)
    nb = S // Cb
    q_blk = (2 * W) // DK
    k_blk = (2 * W + DK) // DK
    v_blk = (2 * W + 2 * DK) // DV
    r_blk = (2 * W + 2 * DK + DV) // DV
    kern = functools.partial(_gla_kernel, heads=H)
    return pl.pallas_call(
        kern,
        grid=(B, nb),
        in_specs=[pl.BlockSpec((Cb, DK), lambda b, i: (b * nb + i, q_blk)),
                  pl.BlockSpec((Cb, DK), lambda b, i: (b * nb + i, k_blk)),
                  pl.BlockSpec((Cb, DV), lambda b, i: (b * nb + i, v_blk)),
                  pl.BlockSpec((Cb, DV), lambda b, i: (b * nb + i, r_blk)),
                  pl.BlockSpec((Cb, LANES), lambda b, i: (b * nb + i, 0)),
                  pl.BlockSpec((LANES, DK), lambda b, i: (0, 0)),
                  pl.BlockSpec((1, DK), lambda b, i: (0, 0)),
                  pl.BlockSpec((1, DV), lambda b, i: (0, 0))],
        out_specs=pl.BlockSpec((Cb, DV), lambda b, i: (b * nb + i, 0)),
        out_shape=jax.ShapeDtypeStruct((B * S, DV), BF16),
        scratch_shapes=[pltpu.VMEM((H, DV // H, DK // H), F32),
                        pltpu.VMEM((Cb, DK), F32)],
        compiler_params=_params(("parallel", "arbitrary"), 32),
        name="gla",
    )(proj, proj, proj, proj, alr, wa2p, ba, on_g)


def _merge_kernel(ya_ref, yb_ref, ga_ref, gb_ref, wpa_ref, wpb_ref, wout_ref, x_ref, g1_ref,
                  n2_ref, sc_ref, sh_ref, wr_ref, x1_ref, h2_ref, lg_ref):
    a = _dot(ya_ref[...], wpa_ref[0])
    b = _dot(yb_ref[...], wpb_ref[0])
    y = _sigmoid(ga_ref[...].astype(F32)) * a + _sigmoid(gb_ref[...].astype(F32)) * b
    out = _dot(y.astype(BF16), wout_ref[0])
    x1 = x_ref[...] + g1_ref[0] * out
    x1_ref[...] = x1
    ms = jnp.mean(x1 * x1, axis=-1, keepdims=True)
    h2 = (x1 * lax.rsqrt(ms + EPS) * n2_ref[...]) * (1.0 + sc_ref[0]) + sh_ref[0]
    _store_packed(h2_ref, h2)
    lg_ref[...] = _dot(h2.astype(BF16), wr_ref[...])


def _merge_call(ya, yb, proj, wpa, wpb, wout, x2, g1, n2g, sc2, sh2, wr, S, W, l):
    T, D = x2.shape
    ga_blk = (proj.shape[1] - 2 * D) // D
    gb_blk = ga_blk + 1
    tm = min(256, S)
    per_b = S // tm
    n_s = D // (2 * LANES)
    const = dict(pipeline_mode=pl.Buffered(1))
    return pl.pallas_call(
        _merge_kernel,
        grid=(T // tm,),
        in_specs=[pl.BlockSpec((tm, W), lambda i: (i, 0)),
                  pl.BlockSpec((tm, W), lambda i: (i, 0)),
                  pl.BlockSpec((tm, D), lambda i: (i, ga_blk)),
                  pl.BlockSpec((tm, D), lambda i: (i, gb_blk)),
                  pl.BlockSpec((1, W, D), lambda i: (l, 0, 0), **const),
                  pl.BlockSpec((1, W, D), lambda i: (l, 0, 0), **const),
                  pl.BlockSpec((1, D, D), lambda i: (l, 0, 0), **const),
                  pl.BlockSpec((tm, D), lambda i: (i, 0)),
                  pl.BlockSpec((1, 1, D), lambda i: (i // per_b, 0, 0)),
                  pl.BlockSpec((1, D), lambda i: (0, 0)),
                  pl.BlockSpec((1, 1, D), lambda i: (i // per_b, 0, 0)),
                  pl.BlockSpec((1, 1, D), lambda i: (i // per_b, 0, 0)),
                  pl.BlockSpec((D, LANES), lambda i: (0, 0), **const)],
        out_specs=[pl.BlockSpec((tm, D), lambda i: (i, 0)),
                   pl.BlockSpec((tm * n_s, LANES), lambda i: (i, 0)),
                   pl.BlockSpec((tm, LANES), lambda i: (i, 0))],
        out_shape=[jax.ShapeDtypeStruct((T, D), F32),
                   jax.ShapeDtypeStruct((T * n_s, LANES), jnp.uint32),
                   jax.ShapeDtypeStruct((T, LANES), F32)],
        compiler_params=_params(("parallel",), 56),
        name="merge_out",
    )(ya, yb, proj, proj, wpa, wpb, wout, x2, g1, n2g, sc2, sh2, wr)


def _router_kernel(lg_ref, b_ref, e_ref, w_ref, cnt_ref, carry, *, groups, per_group):
    @pl.when(pl.program_id(0) == 0)
    def _():
        carry[...] = jnp.zeros(carry.shape, F32)

    lg = lg_ref[...] + b_ref[...]
    lane = lax.broadcasted_iota(jnp.int32, lg.shape, 1)
    lane_f = lane.astype(F32)
    neg = jnp.float32(-1e30)
    big = jnp.float32(LANES)

    def first_argmax(vals):
        m = jnp.max(vals, axis=-1, keepdims=True)
        idx = jnp.min(jnp.where(vals == m, lane_f, big), axis=-1, keepdims=True)
        return m, idx

    gmask = lane < groups
    gl = jnp.where(gmask, lg, neg)
    gmax, gidx = first_argmax(gl)
    gsum = jnp.sum(jnp.where(gmask, jnp.exp(gl - gmax), 0.0), axis=-1, keepdims=True)
    gtop = 1.0 / gsum

    lo = groups + gidx * per_group
    emask = (lane_f >= lo) & (lane_f < lo + per_group)
    el = jnp.where(emask, lg, neg)
    m1, i1 = first_argmax(el)
    el2 = jnp.where(lane_f == i1, neg, el)
    m2, i2 = first_argmax(el2)
    den = jnp.sum(jnp.where(emask, jnp.exp(el - m1), 0.0), axis=-1, keepdims=True)
    p1 = 1.0 / den
    p2 = jnp.exp(m2 - m1) / den
    ps = p1 + p2
    w1 = gtop * p1 / ps
    w2 = gtop * p2 / ps
    e1 = i1 - groups
    e2 = i2 - groups
    w_ref[...] = jnp.where(lane == 0, w1, jnp.where(lane == 1, w2, 0.0))

    oh0 = lane_f == e1
    oh1 = lane_f == e2
    both = oh0.astype(F32) + oh1.astype(F32)
    tb = lg.shape[0]
    r = lax.broadcasted_iota(jnp.int32, (tb, tb), 0)
    c = lax.broadcasted_iota(jnp.int32, (tb, tb), 1)
    strict = (r > c).astype(BF16)
    before = _dot(strict, both.astype(BF16)) + carry[0:1, :]
    r0 = jnp.sum(jnp.where(oh0, before, 0.0), axis=-1, keepdims=True)
    r1 = jnp.sum(jnp.where(oh1, before, 0.0), axis=-1, keepdims=True)
    packed = jnp.where(lane == 0, e1, jnp.where(lane == 1, e2,
                       jnp.where(lane == 2, r0, jnp.where(lane == 3, r1, 0.0))))
    e_ref[...] = packed.astype(jnp.int32)
    carry[...] = carry[...] + jnp.sum(both, axis=0, keepdims=True)
    cnt_ref[...] = carry[...]


def _router_call(logits, bias, per_group):
    T = logits.shape[0]
    tm = min(512, T)
    kern = functools.partial(_router_kernel, groups=MOE_GROUPS, per_group=per_group)
    return pl.pallas_call(
        kern,
        grid=(T // tm,),
        in_specs=[pl.BlockSpec((tm, LANES), lambda i: (i, 0)),
                  pl.BlockSpec((1, LANES), lambda i: (0, 0))],
        out_specs=[pl.BlockSpec((tm, LANES), lambda i: (i, 0)),
                   pl.BlockSpec((tm, LANES), lambda i: (i, 0)),
                   pl.BlockSpec((8, LANES), lambda i: (0, 0))],
        out_shape=[jax.ShapeDtypeStruct((T, LANES), jnp.int32),
                   jax.ShapeDtypeStruct((T, LANES), F32),
                   jax.ShapeDtypeStruct((8, LANES), F32)],
        scratch_shapes=[pltpu.VMEM((8, LANES), F32)],
        compiler_params=_params(("arbitrary",), 16),
        name="router_topk",
    )(logits, bias)


def _row(ref, t, n_s):
    return ref.at[pl.ds(pl.multiple_of(t * n_s, n_s), n_s)]


def _dispatch_kernel(zr_ref, na_ref, dest_ref, h_ref, out_ref, zbuf, sem, zsem, *, n_s, block):
    n_assign = dest_ref.shape[2]
    tm = n_assign // TOP_K
    nb = out_ref.shape[0] // (block * n_s)

    def zero_block(b):
        start = pl.multiple_of(b * (block * n_s), block * n_s)
        return pltpu.make_async_copy(zbuf, out_ref.at[pl.ds(start, block * n_s)], zsem.at[0])

    @pl.when(pl.program_id(0) == 0)
    def _():
        zbuf[...] = jnp.zeros(zbuf.shape, zbuf.dtype)
        n_exp = zr_ref.shape[0]
        lax.fori_loop(0, n_exp, lambda e, c: (zero_block(zr_ref[e]).start(), c)[1], 0)
        lax.fori_loop(na_ref[0], nb, lambda b, c: (zero_block(b).start(), c)[1], 0)
        lax.fori_loop(0, n_exp, lambda e, c: (zero_block(0).wait(), c)[1], 0)
        lax.fori_loop(na_ref[0], nb, lambda b, c: (zero_block(0).wait(), c)[1], 0)

    tok_unroll = ISSUE_UNROLL // TOP_K

    def issue(g, c):
        for u in range(tok_unroll):
            t = g * tok_unroll + u
            for k in range(TOP_K):
                pltpu.make_async_copy(_row(h_ref, t, n_s),
                                      _row(out_ref, dest_ref[0, 0, t * TOP_K + k], n_s),
                                      sem.at[0]).start(priority=k % 2)
        return c
    lax.fori_loop(0, tm // tok_unroll, issue, 0)
    for _ in range(TOP_K):
        pltpu.make_async_copy(h_ref, out_ref.at[pl.ds(0, tm * n_s)], sem.at[0]).wait()


def _dispatch_call(h2p, dest, zero_rows, n_act, n_rows, n_s):
    T = h2p.shape[0] // n_s
    tm = min(1024, T)
    nt = T // tm
    kern = functools.partial(_dispatch_kernel, n_s=n_s, block=MOE_BLOCK)
    grid_spec = pltpu.PrefetchScalarGridSpec(
        num_scalar_prefetch=2,
        grid=(nt,),
        in_specs=[pl.BlockSpec((1, 1, tm * TOP_K), lambda i, zr, na: (i, 0, 0),
                               memory_space=pltpu.SMEM),
                  pl.BlockSpec((tm * n_s, LANES), lambda i, zr, na: (i, 0))],
        out_specs=pl.BlockSpec(memory_space=pl.ANY),
        scratch_shapes=[pltpu.VMEM((MOE_BLOCK * n_s, LANES), h2p.dtype),
                        pltpu.SemaphoreType.DMA((1,)),
                        pltpu.SemaphoreType.DMA((1,))],
    )
    return pl.pallas_call(
        kern,
        grid_spec=grid_spec,
        out_shape=jax.ShapeDtypeStruct((n_rows * n_s, LANES), h2p.dtype),
        compiler_params=_params(("arbitrary",), 16),
        name="row_dispatch",
    )(zero_rows, n_act, dest.reshape(nt, 1, tm * TOP_K), h2p)


def _stage_expert_weights(layer, be_ref, na_ref, nx_ref, i, hbm_refs, stage, bf_refs, sem):
    e = be_ref[i]
    first = (i == 0) | (e != be_ref[jnp.maximum(i - 1, 0)])

    def copies(expert):
        return [pltpu.make_async_copy(w.at[layer, expert], stage[k], sem.at[k])
                for k, w in enumerate(hbm_refs)]

    @pl.when(i == 0)
    def _():
        for cp in copies(e):
            cp.start()

    @pl.when((i < na_ref[0]) & first)
    def _():
        for k, cp in enumerate(copies(e)):
            cp.wait()
            bf_refs[k][...] = stage[k][...].astype(BF16)
        nxt = nx_ref[i]

        @pl.when(nxt >= 0)
        def _():
            for cp in copies(nxt):
                cp.start()


def _expert_kernel(be_ref, na_ref, nx_ref, xs_ref, wg_hbm, wu_hbm, wd_hbm, ys_ref,
                   sg, su, sd, wgb, wub, wdb, sem, *, layer):
    i = pl.program_id(0)
    active = i < na_ref[0]
    _stage_expert_weights(layer, be_ref, na_ref, nx_ref, i, (wg_hbm, wu_hbm, wd_hbm),
                          (sg, su, sd), (wgb, wub, wdb), sem)

    @pl.when(active)
    def _():
        n_s = sd.shape[1] // (2 * LANES)
        rows = xs_ref.shape[0] // n_s
        los, his = _load_packed(xs_ref, rows, n_s)
        x = jnp.concatenate([p.astype(BF16) for p in los + his], axis=1)
        a = _dot(x, wgb[...])
        b = _dot(x, wub[...])
        hid = (a * _sigmoid(a) * b).astype(BF16)
        _store_packed(ys_ref, _dot(hid, wdb[...]))

    @pl.when(jnp.logical_not(active))
    def _():
        ys_ref[...] = jnp.zeros(ys_ref.shape, ys_ref.dtype)


def _experts_call(xs, block_e, n_act, next_e, w_gate, w_up, w_down, l):
    D, DE = w_gate.shape[-2], w_gate.shape[-1]
    n_s = D // (2 * LANES)
    n_rows = xs.shape[0] // n_s
    BM = MOE_BLOCK
    nb = n_rows // BM
    any_spec = pl.BlockSpec(memory_space=pl.ANY)
    grid_spec = pltpu.PrefetchScalarGridSpec(
        num_scalar_prefetch=3,
        grid=(nb,),
        in_specs=[pl.BlockSpec((BM * n_s, LANES),
                               lambda i, be, na, nx: (jnp.minimum(i, na[0] - 1), 0)),
                  any_spec, any_spec, any_spec],
        out_specs=pl.BlockSpec((BM * n_s, LANES), lambda i, be, na, nx: (i, 0)),
        scratch_shapes=[pltpu.VMEM((D, DE), F32), pltpu.VMEM((D, DE), F32), pltpu.VMEM((DE, D), F32),
                        pltpu.VMEM((D, DE), BF16), pltpu.VMEM((D, DE), BF16),
                        pltpu.VMEM((DE, D), BF16),
                        pltpu.SemaphoreType.DMA((3,))],
    )
    return pl.pallas_call(
        functools.partial(_expert_kernel, layer=l),
        grid_spec=grid_spec,
        out_shape=jax.ShapeDtypeStruct((n_rows * n_s, LANES), jnp.uint32),
        compiler_params=_params(("arbitrary",), 56),
        name="expert_mlp",
    )(block_e, n_act, next_e, xs, w_gate, w_up, w_down)


def _combine_kernel(cur_ref, nxt_ref, x_ref, ys_ref, w_ref, g2_ref, fg_ref, o_ref, ybuf0, ybuf1, sem,
                    *, final_norm):
    i = pl.program_id(0)
    n = pl.num_programs(0)
    tm, d = x_ref.shape
    half = d // 2
    n_s = half // LANES
    n_assign = tm * TOP_K
    bufs = (ybuf0, ybuf1)

    def row_copy(idx_ref, r, s):
        return pltpu.make_async_copy(_row(ys_ref, idx_ref[0, 0, r], n_s), _row(bufs[s], r, n_s),
                                     sem.at[s])

    def wait_all(s):
        pltpu.make_async_copy(ys_ref.at[pl.ds(0, n_assign * n_s)], bufs[s], sem.at[s]).wait()

    @pl.when(i == 0)
    def _():
        def issue(g, c):
            for u in range(ISSUE_UNROLL):
                row_copy(cur_ref, g * ISSUE_UNROLL + u, 0).start(priority=u % 2)
            return c
        lax.fori_loop(0, n_assign // ISSUE_UNROLL, issue, 0)

    def step(s):
        wait_all(s)
        for r in range(n_assign):
            row_copy(nxt_ref, r, 1 - s).start(priority=r % 2)
        yg_ref = bufs[s]
        w = w_ref[...]
        w0 = jnp.broadcast_to(w[:, 0:1], (tm, LANES))
        w1 = jnp.broadcast_to(w[:, 1:2], (tm, LANES))
        lo0, hi0 = _load_packed(yg_ref, tm, n_s, offset=0, group=TOP_K)
        lo1, hi1 = _load_packed(yg_ref, tm, n_s, offset=1, group=TOP_K)
        ss = jnp.zeros((tm, 1), F32)
        for c in range(n_s):
            for base, p0, p1 in ((0, lo0[c], lo1[c]), (half, hi0[c], hi1[c])):
                cs = slice(base + c * LANES, base + (c + 1) * LANES)
                xv = x_ref[:, cs] + g2_ref[0, :, cs] * (w0 * p0 + w1 * p1)
                o_ref[:, cs] = xv
                if final_norm:
                    ss = ss + jnp.sum(xv * xv, axis=-1, keepdims=True)
        if final_norm:
            o_ref[...] = o_ref[...] * lax.rsqrt(ss * (1.0 / d) + EPS) * fg_ref[...]

        @pl.when(i == n - 1)
        def _():
            wait_all(1 - s)

    for s in range(2):
        pl.when(i % 2 == s)(functools.partial(step, s))


def _combine_call(x1, ys, dest, wts, g2, final_g, S, final_norm):
    T, D = x1.shape
    tm = min(512, S)
    nt = T // tm
    per_b = S // tm
    n_s = D // (2 * LANES)
    dest3 = dest.reshape(nt, 1, tm * TOP_K)
    kern = functools.partial(_combine_kernel, final_norm=final_norm)
    return pl.pallas_call(
        kern,
        grid=(nt,),
        in_specs=[pl.BlockSpec((1, 1, tm * TOP_K), lambda i: (i, 0, 0), memory_space=pltpu.SMEM),
                  pl.BlockSpec((1, 1, tm * TOP_K), lambda i: (jnp.minimum(i + 1, nt - 1), 0, 0),
                               memory_space=pltpu.SMEM),
                  pl.BlockSpec((tm, D), lambda i: (i, 0)),
                  pl.BlockSpec(memory_space=pl.ANY),
                  pl.BlockSpec((tm, LANES), lambda i: (i, 0)),
                  pl.BlockSpec((1, 1, D), lambda i: (i // per_b, 0, 0)),
                  pl.BlockSpec((1, D), lambda i: (0, 0))],
        out_specs=pl.BlockSpec((tm, D), lambda i: (i, 0)),
        out_shape=jax.ShapeDtypeStruct((T, D), F32),
        scratch_shapes=[pltpu.VMEM((tm * TOP_K * n_s, LANES), jnp.uint32),
                        pltpu.VMEM((tm * TOP_K * n_s, LANES), jnp.uint32),
                        pltpu.SemaphoreType.DMA((2,))],
        compiler_params=_params(("arbitrary",), 48),
        name="moe_combine",
    )(dest3, dest3, x1, ys, wts, g2, final_g)


def _dispatch_plan(routed, cnt, n_experts):
    T = routed.shape[0]
    A = T * TOP_K
    BM = MOE_BLOCK
    eid = routed[:, :TOP_K]
    rank = routed[:, TOP_K:2 * TOP_K]
    counts = cnt[0, :n_experts].astype(jnp.int32)
    padded = (counts + BM - 1) // BM * BM
    pad_end = jnp.cumsum(padded)
    pad_start = pad_end - padded
    onehot = eid[:, :, None] == jnp.arange(n_experts, dtype=jnp.int32)[None, None, :]
    dest = (jnp.sum(jnp.where(onehot, pad_start[None, None, :], 0), axis=-1)
            + rank).reshape(A)
    nb = (A + n_experts * (BM - 1) + BM - 1) // BM
    block_start = jnp.arange(nb, dtype=jnp.int32) * BM
    block_e = jnp.minimum(jnp.sum(block_start[:, None] >= pad_end[None, :], axis=1),
                          n_experts - 1).astype(jnp.int32)
    n_act = (pad_end[-1] // BM).astype(jnp.int32).reshape(1)
    last_block = jnp.maximum(pad_end // BM - 1, 0).astype(jnp.int32)
    ex = jnp.arange(n_experts, dtype=jnp.int32)
    later = (ex[None, :] > ex[:, None]) & (counts[None, :] > 0)
    next_of = jnp.min(jnp.where(later, ex[None, :], n_experts), axis=1)
    next_of = jnp.where(next_of < n_experts, next_of, -1)
    next_e = jnp.sum(jnp.where(block_e[:, None] == ex[None, :], next_of[None, :], 0),
                     axis=1).astype(jnp.int32)
    return dest.astype(jnp.int32), block_e, n_act, next_e, last_block, nb * BM


def kernel(x, c, ada_w, ada_b, norm1_g, w_in, gm_vn_g, gm_vn_b, gm_ws, gm_bs, gla_wa2, gla_ba,
           gla_on_g, w_pa, w_pb, w_out, norm2_g, w_rg, b_rg, w_re, b_re, w_e_gate, w_e_up,
           w_e_down, final_g):
    B, S, D = x.shape
    T = B * S
    L = ada_w.shape[0]
    W = gm_vn_g.shape[1]
    G, C = gm_ws.shape[1], gm_ws.shape[2]
    RANK, DK = gla_wa2.shape[1], gla_wa2.shape[2]
    DV = gla_on_g.shape[1]
    E = w_e_gate.shape[1]
    per_group = E // MOE_GROUPS
    SUB = D // (2 * LANES)
    assert B <= 8 and MOE_GROUPS + E <= LANES and RANK <= LANES

    c8 = jnp.zeros((8, D), F32).at[:B].set(c)
    mod = _ada_call(c8, ada_w, ada_b.reshape(L, 1, 6 * D))

    o_alr = 2 * W + 2 * DK + 2 * DV
    x2 = x.reshape(T, D)
    w_main = _inproj_weight_call(w_in, o_alr, RANK)
    w_alr = jnp.zeros((L, D, LANES), BF16).at[:, :, :RANK].set(
        w_in[:, :, o_alr:o_alr + RANK].astype(BF16))
    w_pa_b, w_pb_b, w_out_b = _cast_call(w_pa), _cast_call(w_pb), _cast_call(w_out)
    for l in range(L):
        sh1, sc1, g1, sh2, sc2, g2 = [mod[l, :B, k * D:(k + 1) * D].reshape(B, 1, D) for k in range(6)]
        wa2p = jnp.zeros((LANES, DK), BF16).at[:RANK].set(gla_wa2[l].astype(BF16))
        bias_full = jnp.repeat(gm_bs[l].T, W // G, axis=1)
        wr = jnp.zeros((D, LANES), BF16).at[:, :MOE_GROUPS + E].set(
            jnp.concatenate([w_rg[l], w_re[l]], axis=1).astype(BF16))
        br = jnp.zeros((1, LANES), F32).at[0, :MOE_GROUPS + E].set(
            jnp.concatenate([b_rg[l], b_re[l]]))

        proj, alr = _inproj_call(x2, norm1_g[l].reshape(1, D), sc1, sh1, w_main, w_alr, S, l)
        ya = _gmlp_call(proj, gm_vn_g[l].reshape(1, W), gm_vn_b[l].reshape(1, W), gm_ws[l], bias_full)
        yb = _gla_call(proj, alr, wa2p, gla_ba[l].reshape(1, DK), gla_on_g[l].reshape(1, DV), B, S, W)
        x1, h2, logits = _merge_call(ya, yb, proj, w_pa_b, w_pb_b, w_out_b, x2, g1,
                                     norm2_g[l].reshape(1, D), sc2, sh2, wr, S, W, l)
        routed, wts, cnt = _router_call(logits, br, per_group)
        dest, block_e, n_act, next_e, last_block, n_rows = _dispatch_plan(routed, cnt, E)
        xs = _dispatch_call(h2, dest, last_block, n_act, n_rows, SUB)
        ys = _experts_call(xs, block_e, n_act, next_e, w_e_gate, w_e_up, w_e_down, l)
        x2 = _combine_call(x1, ys, dest, wts, g2, final_g.reshape(1, D), S, final_norm=(l == L - 1))
    return x2.reshape(B, S, D)
```

```python
import functools

import jax
import jax.numpy as jnp
from jax import lax
from jax.experimental import pallas as pl
from jax.experimental.pallas import tpu as pltpu

F32 = jnp.float32
BF16 = jnp.bfloat16

EPS = 1e-6
GLA_HEADS = 4
GLA_TAU = 16.0
GLA_CHUNK = 64
MOE_GROUPS = 4
TOP_K = 2

LANES = 128
MIB = 1024 * 1024
MOE_BLOCK = 256
ISSUE_UNROLL = 8

NT_DIMS = (((1,), (1,)), ((), ()))
TN_DIMS = (((0,), (0,)), ((), ()))


def _params(semantics, vmem_mib):
    return pltpu.CompilerParams(dimension_semantics=semantics,
                                vmem_limit_bytes=int(vmem_mib * MIB))


def _dot(a, b):
    return jnp.dot(a, b, preferred_element_type=F32)


def _sigmoid(x):
    return 1.0 / (1.0 + jnp.exp(-x))


def _gelu(x):
    return 0.5 * x * (1.0 + lax.erf(x * (2.0 ** -0.5)))


def _pack_pairs(lo, hi):
    lo_w = pltpu.bitcast(lo.astype(BF16).astype(F32), jnp.uint32) >> 16
    hi_w = pltpu.bitcast(hi.astype(BF16).astype(F32), jnp.uint32) & jnp.uint32(0xFFFF0000)
    return lo_w | hi_w


def _unpack_pairs(w):
    lo = pltpu.bitcast(w << 16, F32)
    hi = pltpu.bitcast(w & jnp.uint32(0xFFFF0000), F32)
    return lo, hi


def _store_packed(ref, x):
    rows, d = x.shape
    half = d // 2
    n_s = half // LANES
    for s in range(n_s):
        lo = x[:, s * LANES:(s + 1) * LANES]
        hi = x[:, half + s * LANES:half + (s + 1) * LANES]
        ref[pl.ds(s, rows, stride=n_s), :] = _pack_pairs(lo, hi)


def _load_packed(ref, rows, n_s, offset=0, group=1):
    los, his = [], []
    for s in range(n_s):
        lo, hi = _unpack_pairs(ref[pl.ds(offset * n_s + s, rows, stride=n_s * group), :])
        los.append(lo)
        his.append(hi)
    return los, his


def _ada_kernel(c_ref, w_ref, b_ref, o_ref):
    c = c_ref[...]
    cond = c * _sigmoid(c)
    o_ref[0] = _dot(cond.astype(BF16), w_ref[0].astype(BF16)) + b_ref[0]


def _ada_call(c8, ada_w, ada_b3):
    L, D, N = ada_w.shape
    tn = min(2048, N)
    return pl.pallas_call(
        _ada_kernel,
        grid=(L, N // tn),
        in_specs=[pl.BlockSpec((8, D), lambda l, j: (0, 0)),
                  pl.BlockSpec((1, D, tn), lambda l, j: (l, 0, j)),
                  pl.BlockSpec((1, 1, tn), lambda l, j: (l, 0, j))],
        out_specs=pl.BlockSpec((1, 8, tn), lambda l, j: (l, 0, j)),
        out_shape=jax.ShapeDtypeStruct((L, 8, N), F32),
        compiler_params=_params(("parallel", "parallel"), 52),
        name="ada_mod",
    )(c8, ada_w, ada_b3)


def _cast_kernel(w_ref, o_ref):
    o_ref[...] = w_ref[...].astype(o_ref.dtype)


def _cast_call(w):
    L, K, N = w.shape
    tn = min(1024, N)
    return pl.pallas_call(
        _cast_kernel,
        grid=(L, N // tn),
        in_specs=[pl.BlockSpec((1, K, tn), lambda l, j: (l, 0, j))],
        out_specs=pl.BlockSpec((1, K, tn), lambda l, j: (l, 0, j)),
        out_shape=jax.ShapeDtypeStruct(w.shape, BF16),
        compiler_params=_params(("parallel", "parallel"), 32),
        name="cast_bf16",
    )(w)


def _inproj_weight_kernel(wt_ref, o_ref):
    o_ref[0] = wt_ref[...].T.astype(BF16)


def _inproj_weight_call(w_in, head_cols, rank):
    L, D, N = w_in.shape
    NM = N - rank
    tn = min(1024, NM)
    n_head = head_cols // tn
    sub = 8
    assert head_cols % tn == 0 and NM % tn == 0 and N % sub == 0 and rank % sub == 0
    wt = jnp.swapaxes(w_in, 1, 2).reshape(L * N, D)

    def first_row(l, j):
        return pl.multiple_of(l * N + j * tn + jnp.where(j >= n_head, rank, 0), sub)

    return pl.pallas_call(
        _inproj_weight_kernel,
        grid=(L, NM // tn),
        in_specs=[pl.BlockSpec((pl.Element(tn), pl.Element(D)), lambda l, j: (first_row(l, j), 0))],
        out_specs=pl.BlockSpec((1, D, tn), lambda l, j: (l, 0, j)),
        out_shape=jax.ShapeDtypeStruct((L, D, NM), BF16),
        compiler_params=_params(("parallel", "parallel"), 48),
        name="in_proj_weight",
    )(wt)


def _inproj_kernel(x_ref, g_ref, sc_ref, sh_ref, w_ref, walr_ref, o_ref, alr_ref, h_scr):
    @pl.when(pl.program_id(1) == 0)
    def _():
        x = x_ref[...]
        ms = jnp.mean(x * x, axis=-1, keepdims=True)
        y = x * lax.rsqrt(ms + EPS) * g_ref[...]
        h = (y * (1.0 + sc_ref[0]) + sh_ref[0]).astype(BF16)
        h_scr[...] = h
        alr_ref[...] = _dot(h, walr_ref[0])

    o_ref[...] = _dot(h_scr[...], w_ref[0]).astype(o_ref.dtype)


def _inproj_call(x2, g, sc, sh, w_main, w_alr, S, l):
    T, D = x2.shape
    NM = w_main.shape[2]
    tm = min(1024, S)
    tn = min(2048, NM)
    per_b = S // tm
    return pl.pallas_call(
        _inproj_kernel,
        grid=(T // tm, NM // tn),
        in_specs=[pl.BlockSpec((tm, D), lambda i, j: (i, 0)),
                  pl.BlockSpec((1, D), lambda i, j: (0, 0)),
                  pl.BlockSpec((1, 1, D), lambda i, j: (i // per_b, 0, 0)),
                  pl.BlockSpec((1, 1, D), lambda i, j: (i // per_b, 0, 0)),
                  pl.BlockSpec((1, D, tn), lambda i, j: (l, 0, j)),
                  pl.BlockSpec((1, D, LANES), lambda i, j: (l, 0, 0))],
        out_specs=[pl.BlockSpec((tm, tn), lambda i, j: (i, j)),
                   pl.BlockSpec((tm, LANES), lambda i, j: (i, 0))],
        out_shape=[jax.ShapeDtypeStruct((T, NM), BF16),
                   jax.ShapeDtypeStruct((T, LANES), F32)],
        scratch_shapes=[pltpu.VMEM((tm, D), BF16)],
        compiler_params=_params(("parallel", "arbitrary"), 56),
        name="in_proj",
    )(x2, g, sc, sh, w_main, w_alr)


def _gmlp_kernel(u_ref, v_ref, g_ref, b_ref, ws_ref, bias_ref, o_ref, vn_scr, *, groups, chunk):
    rows, width = v_ref.shape
    gc = width // groups
    r = lax.broadcasted_iota(jnp.int32, (chunk, chunk), 0)
    c = lax.broadcasted_iota(jnp.int32, (chunk, chunk), 1)
    causal = r >= c
    for ci in range(rows // chunk):
        rs = slice(ci * chunk, (ci + 1) * chunk)
        gv = _gelu(v_ref[rs, :].astype(F32))
        mu = jnp.mean(gv, axis=-1, keepdims=True)
        xc = gv - mu
        var = jnp.mean(xc * xc, axis=-1, keepdims=True)
        vn = xc * lax.rsqrt(var + EPS) * g_ref[...] + b_ref[...]
        vn_scr[...] = vn.astype(BF16)
        for gi in range(groups):
            cs = slice(gi * gc, (gi + 1) * gc)
            wm = jnp.where(causal, ws_ref[gi], 0.0).astype(BF16)
            mixed = _dot(wm, vn_scr[:, cs]) + bias_ref[:, cs]
            gu = _gelu(u_ref[rs, cs].astype(F32))
            o_ref[rs, cs] = (gu * mixed).astype(o_ref.dtype)


def _gmlp_call(proj, vn_g, vn_b, ws, bias_full):
    T = proj.shape[0]
    G, C, _ = ws.shape
    W = vn_g.shape[1]
    R = min(4 * C, T)
    kern = functools.partial(_gmlp_kernel, groups=G, chunk=C)
    return pl.pallas_call(
        kern,
        grid=(T // R,),
        in_specs=[pl.BlockSpec((R, W), lambda i: (i, 0)),
                  pl.BlockSpec((R, W), lambda i: (i, 1)),
                  pl.BlockSpec((1, W), lambda i: (0, 0)),
                  pl.BlockSpec((1, W), lambda i: (0, 0)),
                  pl.BlockSpec((G, C, C), lambda i: (0, 0, 0)),
                  pl.BlockSpec((C, W), lambda i: (0, 0))],
        out_specs=pl.BlockSpec((R, W), lambda i: (i, 0)),
        out_shape=jax.ShapeDtypeStruct((T, W), BF16),
        scratch_shapes=[pltpu.VMEM((C, W), BF16)],
        compiler_params=_params(("parallel",), 32),
        name="gmlp_gate",
    )(proj, proj, vn_g, vn_b, ws, bias_full)


def _gla_kernel(q_ref, k_ref, v_ref, r_ref, alr_ref, wa2_ref, ba_ref, g_ref, o_ref,
                st_ref, la_ref, *, heads):
    rows, dk = q_ref.shape
    dv = v_ref.shape[1]
    hk = dk // heads
    hv = dv // heads
    C = GLA_CHUNK
    scale = hk ** -0.5

    @pl.when(pl.program_id(1) == 0)
    def _():
        st_ref[...] = jnp.zeros(st_ref.shape, F32)

    z = _dot(alr_ref[...].astype(BF16), wa2_ref[...]) + ba_ref[...]
    la_ref[...] = (jnp.minimum(z, 0.0) - jnp.log(1.0 + jnp.exp(-jnp.abs(z)))) * (1.0 / GLA_TAU)

    ri = lax.broadcasted_iota(jnp.int32, (C, C), 0)
    ci = lax.broadcasted_iota(jnp.int32, (C, C), 1)
    causal = ri >= ci
    tri = causal.astype(BF16)

    def chunk_step(c, carry):
        rs = pl.ds(pl.multiple_of(c * C, C), C)
        la = la_ref[rs, :]
        hi = la.astype(BF16)
        r1 = la - hi.astype(F32)
        mid = r1.astype(BF16)
        lo = (r1 - mid.astype(F32)).astype(BF16)
        cum = _dot(tri, hi) + _dot(tri, mid) + _dot(tri, lo)
        cl = cum[C - 1:C, :]
        q = q_ref[rs, :].astype(F32) * scale
        k = k_ref[rs, :].astype(F32)
        qd = (q * jnp.exp(cum)).astype(BF16)
        ki = (k * jnp.exp(-cum)).astype(BF16)
        ke = (k * jnp.exp(cl - cum)).astype(BF16)
        dec = jnp.exp(cl)
        for h in range(heads):
            ks = slice(h * hk, (h + 1) * hk)
            vs = slice(h * hv, (h + 1) * hv)
            vh = v_ref[rs, vs]
            s = lax.dot_general(qd[:, ks], ki[:, ks], NT_DIMS, preferred_element_type=F32)
            s = jnp.where(causal, s, 0.0).astype(BF16)
            st = st_ref[h]
            o = _dot(s, vh) + lax.dot_general(qd[:, ks], st.astype(BF16), NT_DIMS,
                                              preferred_element_type=F32)
            st_ref[h] = st * dec[:, ks] + lax.dot_general(vh, ke[:, ks], TN_DIMS,
                                                          preferred_element_type=F32)
            ms = jnp.mean(o * o, axis=-1, keepdims=True)
            on = o * lax.rsqrt(ms + EPS) * g_ref[:, vs]
            rr = r_ref[rs, vs].astype(F32)
            o_ref[rs, vs] = (on * (rr * _sigmoid(rr))).astype(o_ref.dtype)
        return carry

    lax.fori_loop(0, rows // C, chunk_step, 0)


def _gla_call(proj, alr, wa2p, ba, on_g, B, S, W):
    DK = wa2p.shape[1]
    DV = on_g.shape[1]
    H = GLA_HEADS
    Cb = min(1024, S)
    nb = S // Cb
    q_blk = (2 * W) // DK
    k_blk = (2 * W + DK) // DK
    v_blk = (2 * W + 2 * DK) // DV
    r_blk = (2 * W + 2 * DK + DV) // DV
    kern = functools.partial(_gla_kernel, heads=H)
    return pl.pallas_call(
        kern,
        grid=(B, nb),
        in_specs=[pl.BlockSpec((Cb, DK), lambda b, i: (b * nb + i, q_blk)),
                  pl.BlockSpec((Cb, DK), lambda b, i: (b * nb + i, k_blk)),
                  pl.BlockSpec((Cb, DV), lambda b, i: (b * nb + i, v_blk)),
                  pl.BlockSpec((Cb, DV), lambda b, i: (b * nb + i, r_blk)),
                  pl.BlockSpec((Cb, LANES), lambda b, i: (b * nb + i, 0)),
                  pl.BlockSpec((LANES, DK), lambda b, i: (0, 0)),
                  pl.BlockSpec((1, DK), lambda b, i: (0, 0)),
                  pl.BlockSpec((1, DV), lambda b, i: (0, 0))],
        out_specs=pl.BlockSpec((Cb, DV), lambda b, i: (b * nb + i, 0)),
        out_shape=jax.ShapeDtypeStruct((B * S, DV), BF16),
        scratch_shapes=[pltpu.VMEM((H, DV // H, DK // H), F32),
                        pltpu.VMEM((Cb, DK), F32)],
        compiler_params=_params(("parallel", "arbitrary"), 48),
        name="gla",
    )(proj, proj, proj, proj, alr, wa2p, ba, on_g)


def _merge_kernel(ya_ref, yb_ref, ga_ref, gb_ref, wpa_ref, wpb_ref, wout_ref, x_ref, g1_ref,
                  n2_ref, sc_ref, sh_ref, wr_ref, x1_ref, h2_ref, lg_ref):
    a = _dot(ya_ref[...], wpa_ref[0])
    b = _dot(yb_ref[...], wpb_ref[0])
    y = _sigmoid(ga_ref[...].astype(F32)) * a + _sigmoid(gb_ref[...].astype(F32)) * b
    out = _dot(y.astype(BF16), wout_ref[0])
    x1 = x_ref[...] + g1_ref[0] * out
    x1_ref[...] = x1
    ms = jnp.mean(x1 * x1, axis=-1, keepdims=True)
    h2 = (x1 * lax.rsqrt(ms + EPS) * n2_ref[...]) * (1.0 + sc_ref[0]) + sh_ref[0]
    _store_packed(h2_ref, h2)
    lg_ref[...] = _dot(h2.astype(BF16), wr_ref[...])


def _merge_call(ya, yb, proj, wpa, wpb, wout, x2, g1, n2g, sc2, sh2, wr, S, W, l):
    T, D = x2.shape
    ga_blk = (proj.shape[1] - 2 * D) // D
    gb_blk = ga_blk + 1
    tm = min(256, S)
    per_b = S // tm
    n_s = D // (2 * LANES)
    const = dict(pipeline_mode=pl.Buffered(1))
    return pl.pallas_call(
        _merge_kernel,
        grid=(T // tm,),
        in_specs=[pl.BlockSpec((tm, W), lambda i: (i, 0)),
                  pl.BlockSpec((tm, W), lambda i: (i, 0)),
                  pl.BlockSpec((tm, D), lambda i: (i, ga_blk)),
                  pl.BlockSpec((tm, D), lambda i: (i, gb_blk)),
                  pl.BlockSpec((1, W, D), lambda i: (l, 0, 0), **const),
                  pl.BlockSpec((1, W, D), lambda i: (l, 0, 0), **const),
                  pl.BlockSpec((1, D, D), lambda i: (l, 0, 0), **const),
                  pl.BlockSpec((tm, D), lambda i: (i, 0)),
                  pl.BlockSpec((1, 1, D), lambda i: (i // per_b, 0, 0)),
                  pl.BlockSpec((1, D), lambda i: (0, 0)),
                  pl.BlockSpec((1, 1, D), lambda i: (i // per_b, 0, 0)),
                  pl.BlockSpec((1, 1, D), lambda i: (i // per_b, 0, 0)),
                  pl.BlockSpec((D, LANES), lambda i: (0, 0), **const)],
        out_specs=[pl.BlockSpec((tm, D), lambda i: (i, 0)),
                   pl.BlockSpec((tm * n_s, LANES), lambda i: (i, 0)),
                   pl.BlockSpec((tm, LANES), lambda i: (i, 0))],
        out_shape=[jax.ShapeDtypeStruct((T, D), F32),
                   jax.ShapeDtypeStruct((T * n_s, LANES), jnp.uint32),
                   jax.ShapeDtypeStruct((T, LANES), F32)],
        compiler_params=_params(("parallel",), 56),
        name="merge_out",
    )(ya, yb, proj, proj, wpa, wpb, wout, x2, g1, n2g, sc2, sh2, wr)


def _router_kernel(lg_ref, b_ref, e_ref, w_ref, cnt_ref, carry, *, groups, per_group):
    @pl.when(pl.program_id(0) == 0)
    def _():
        carry[...] = jnp.zeros(carry.shape, F32)

    lg = lg_ref[...] + b_ref[...]
    lane = lax.broadcasted_iota(jnp.int32, lg.shape, 1)
    lane_f = lane.astype(F32)
    neg = jnp.float32(-1e30)
    big = jnp.float32(LANES)

    def first_argmax(vals):
        m = jnp.max(vals, axis=-1, keepdims=True)
        idx = jnp.min(jnp.where(vals == m, lane_f, big), axis=-1, keepdims=True)
        return m, idx

    gmask = lane < groups
    gl = jnp.where(gmask, lg, neg)
    gmax, gidx = first_argmax(gl)
    gsum = jnp.sum(jnp.where(gmask, jnp.exp(gl - gmax), 0.0), axis=-1, keepdims=True)
    gtop = 1.0 / gsum

    lo = groups + gidx * per_group
    emask = (lane_f >= lo) & (lane_f < lo + per_group)
    el = jnp.where(emask, lg, neg)
    m1, i1 = first_argmax(el)
    el2 = jnp.where(lane_f == i1, neg, el)
    m2, i2 = first_argmax(el2)
    den = jnp.sum(jnp.where(emask, jnp.exp(el - m1), 0.0), axis=-1, keepdims=True)
    p1 = 1.0 / den
    p2 = jnp.exp(m2 - m1) / den
    ps = p1 + p2
    w1 = gtop * p1 / ps
    w2 = gtop * p2 / ps
    e1 = i1 - groups
    e2 = i2 - groups
    w_ref[...] = jnp.where(lane == 0, w1, jnp.where(lane == 1, w2, 0.0))

    oh0 = lane_f == e1
    oh1 = lane_f == e2
    both = oh0.astype(F32) + oh1.astype(F32)
    tb = lg.shape[0]
    r = lax.broadcasted_iota(jnp.int32, (tb, tb), 0)
    c = lax.broadcasted_iota(jnp.int32, (tb, tb), 1)
    strict = (r > c).astype(BF16)
    before = _dot(strict, both.astype(BF16)) + carry[0:1, :]
    r0 = jnp.sum(jnp.where(oh0, before, 0.0), axis=-1, keepdims=True)
    r1 = jnp.sum(jnp.where(oh1, before, 0.0), axis=-1, keepdims=True)
    packed = jnp.where(lane == 0, e1, jnp.where(lane == 1, e2,
                       jnp.where(lane == 2, r0, jnp.where(lane == 3, r1, 0.0))))
    e_ref[...] = packed.astype(jnp.int32)
    carry[...] = carry[...] + jnp.sum(both, axis=0, keepdims=True)
    cnt_ref[...] = carry[...]


def _router_call(logits, bias, per_group):
    T = logits.shape[0]
    tm = min(512, T)
    kern = functools.partial(_router_kernel, groups=MOE_GROUPS, per_group=per_group)
    return pl.pallas_call(
        kern,
        grid=(T // tm,),
        in_specs=[pl.BlockSpec((tm, LANES), lambda i: (i, 0)),
                  pl.BlockSpec((1, LANES), lambda i: (0, 0))],
        out_specs=[pl.BlockSpec((tm, LANES), lambda i: (i, 0)),
                   pl.BlockSpec((tm, LANES), lambda i: (i, 0)),
                   pl.BlockSpec((8, LANES), lambda i: (0, 0))],
        out_shape=[jax.ShapeDtypeStruct((T, LANES), jnp.int32),
                   jax.ShapeDtypeStruct((T, LANES), F32),
                   jax.ShapeDtypeStruct((8, LANES), F32)],
        scratch_shapes=[pltpu.VMEM((8, LANES), F32)],
        compiler_params=_params(("arbitrary",), 16),
        name="router_topk",
    )(logits, bias)


def _row(ref, t, n_s):
    return ref.at[pl.ds(pl.multiple_of(t * n_s, n_s), n_s)]


def _dispatch_kernel(zr_ref, na_ref, dest_ref, h_ref, out_ref, zbuf, sem, zsem, *, n_s, block):
    n_assign = dest_ref.shape[2]
    tm = n_assign // TOP_K
    nb = out_ref.shape[0] // (block * n_s)

    def zero_block(b):
        start = pl.multiple_of(b * (block * n_s), block * n_s)
        return pltpu.make_async_copy(zbuf, out_ref.at[pl.ds(start, block * n_s)], zsem.at[0])

    @pl.when(pl.program_id(0) == 0)
    def _():
        zbuf[...] = jnp.zeros(zbuf.shape, zbuf.dtype)
        n_exp = zr_ref.shape[0]
        lax.fori_loop(0, n_exp, lambda e, c: (zero_block(zr_ref[e]).start(), c)[1], 0)
        lax.fori_loop(na_ref[0], nb, lambda b, c: (zero_block(b).start(), c)[1], 0)
        lax.fori_loop(0, n_exp, lambda e, c: (zero_block(0).wait(), c)[1], 0)
        lax.fori_loop(na_ref[0], nb, lambda b, c: (zero_block(0).wait(), c)[1], 0)

    tok_unroll = ISSUE_UNROLL // TOP_K

    def issue(g, c):
        for u in range(tok_unroll):
            t = g * tok_unroll + u
            for k in range(TOP_K):
                pltpu.make_async_copy(_row(h_ref, t, n_s),
                                      _row(out_ref, dest_ref[0, 0, t * TOP_K + k], n_s),
                                      sem.at[0]).start(priority=k % 2)
        return c
    lax.fori_loop(0, tm // tok_unroll, issue, 0)
    for _ in range(TOP_K):
        pltpu.make_async_copy(h_ref, out_ref.at[pl.ds(0, tm * n_s)], sem.at[0]).wait()


def _dispatch_call(h2p, dest, zero_rows, n_act, n_rows, n_s):
    T = h2p.shape[0] // n_s
    tm = min(1024, T)
    nt = T // tm
    kern = functools.partial(_dispatch_kernel, n_s=n_s, block=MOE_BLOCK)
    grid_spec = pltpu.PrefetchScalarGridSpec(
        num_scalar_prefetch=2,
        grid=(nt,),
        in_specs=[pl.BlockSpec((1, 1, tm * TOP_K), lambda i, zr, na: (i, 0, 0),
                               memory_space=pltpu.SMEM),
                  pl.BlockSpec((tm * n_s, LANES), lambda i, zr, na: (i, 0))],
        out_specs=pl.BlockSpec(memory_space=pl.ANY),
        scratch_shapes=[pltpu.VMEM((MOE_BLOCK * n_s, LANES), h2p.dtype),
                        pltpu.SemaphoreType.DMA((1,)),
                        pltpu.SemaphoreType.DMA((1,))],
    )
    return pl.pallas_call(
        kern,
        grid_spec=grid_spec,
        out_shape=jax.ShapeDtypeStruct((n_rows * n_s, LANES), h2p.dtype),
        compiler_params=_params(("arbitrary",), 16),
        name="row_dispatch",
    )(zero_rows, n_act, dest.reshape(nt, 1, tm * TOP_K), h2p)


def _stage_expert_weights(layer, be_ref, na_ref, nx_ref, i, hbm_refs, stage, bf_refs, sem):
    e = be_ref[i]
    first = (i == 0) | (e != be_ref[jnp.maximum(i - 1, 0)])

    def copies(expert):
        return [pltpu.make_async_copy(w.at[layer, expert], stage[k], sem.at[k])
                for k, w in enumerate(hbm_refs)]

    @pl.when(i == 0)
    def _():
        for cp in copies(e):
            cp.start()

    @pl.when((i < na_ref[0]) & first)
    def _():
        for k, cp in enumerate(copies(e)):
            cp.wait()
            bf_refs[k][...] = stage[k][...].astype(BF16)
        nxt = nx_ref[i]

        @pl.when(nxt >= 0)
        def _():
            for cp in copies(nxt):
                cp.start()


def _expert_kernel(be_ref, na_ref, nx_ref, xs_ref, wg_hbm, wu_hbm, wd_hbm, ys_ref,
                   sg, su, sd, wgb, wub, wdb, sem, *, layer):
    i = pl.program_id(0)
    active = i < na_ref[0]
    _stage_expert_weights(layer, be_ref, na_ref, nx_ref, i, (wg_hbm, wu_hbm, wd_hbm),
                          (sg, su, sd), (wgb, wub, wdb), sem)

    @pl.when(active)
    def _():
        n_s = sd.shape[1] // (2 * LANES)
        rows = xs_ref.shape[0] // n_s
        los, his = _load_packed(xs_ref, rows, n_s)
        x = jnp.concatenate([p.astype(BF16) for p in los + his], axis=1)
        a = _dot(x, wgb[...])
        b = _dot(x, wub[...])
        hid = (a * _sigmoid(a) * b).astype(BF16)
        _store_packed(ys_ref, _dot(hid, wdb[...]))

    @pl.when(jnp.logical_not(active))
    def _():
        ys_ref[...] = jnp.zeros(ys_ref.shape, ys_ref.dtype)


def _experts_call(xs, block_e, n_act, next_e, w_gate, w_up, w_down, l):
    D, DE = w_gate.shape[-2], w_gate.shape[-1]
    n_s = D // (2 * LANES)
    n_rows = xs.shape[0] // n_s
    BM = MOE_BLOCK
    nb = n_rows // BM
    any_spec = pl.BlockSpec(memory_space=pl.ANY)
    grid_spec = pltpu.PrefetchScalarGridSpec(
        num_scalar_prefetch=3,
        grid=(nb,),
        in_specs=[pl.BlockSpec((BM * n_s, LANES),
                               lambda i, be, na, nx: (jnp.minimum(i, na[0] - 1), 0)),
                  any_spec, any_spec, any_spec],
        out_specs=pl.BlockSpec((BM * n_s, LANES), lambda i, be, na, nx: (i, 0)),
        scratch_shapes=[pltpu.VMEM((D, DE), F32), pltpu.VMEM((D, DE), F32), pltpu.VMEM((DE, D), F32),
                        pltpu.VMEM((D, DE), BF16), pltpu.VMEM((D, DE), BF16),
                        pltpu.VMEM((DE, D), BF16),
                        pltpu.SemaphoreType.DMA((3,))],
    )
    return pl.pallas_call(
        functools.partial(_expert_kernel, layer=l),
        grid_spec=grid_spec,
        out_shape=jax.ShapeDtypeStruct((n_rows * n_s, LANES), jnp.uint32),
        compiler_params=_params(("arbitrary",), 56),
        name="expert_mlp",
    )(block_e, n_act, next_e, xs, w_gate, w_up, w_down)


def _combine_kernel(cur_ref, nxt_ref, x_ref, ys_ref, w_ref, g2_ref, fg_ref, o_ref, ybuf0, ybuf1, sem,
                    *, final_norm):
    i = pl.program_id(0)
    n = pl.num_programs(0)
    tm, d = x_ref.shape
    half = d // 2
    n_s = half // LANES
    n_assign = tm * TOP_K
    bufs = (ybuf0, ybuf1)

    def row_copy(idx_ref, r, s):
        return pltpu.make_async_copy(_row(ys_ref, idx_ref[0, 0, r], n_s), _row(bufs[s], r, n_s),
                                     sem.at[s])

    def wait_all(s):
        pltpu.make_async_copy(ys_ref.at[pl.ds(0, n_assign * n_s)], bufs[s], sem.at[s]).wait()

    @pl.when(i == 0)
    def _():
        def issue(g, c):
            for u in range(ISSUE_UNROLL):
                row_copy(cur_ref, g * ISSUE_UNROLL + u, 0).start(priority=u % 2)
            return c
        lax.fori_loop(0, n_assign // ISSUE_UNROLL, issue, 0)

    def step(s):
        wait_all(s)
        for r in range(n_assign):
            row_copy(nxt_ref, r, 1 - s).start(priority=r % 2)
        yg_ref = bufs[s]
        w = w_ref[...]
        w0 = jnp.broadcast_to(w[:, 0:1], (tm, LANES))
        w1 = jnp.broadcast_to(w[:, 1:2], (tm, LANES))
        lo0, hi0 = _load_packed(yg_ref, tm, n_s, offset=0, group=TOP_K)
        lo1, hi1 = _load_packed(yg_ref, tm, n_s, offset=1, group=TOP_K)
        ss = jnp.zeros((tm, 1), F32)
        for c in range(n_s):
            for base, p0, p1 in ((0, lo0[c], lo1[c]), (half, hi0[c], hi1[c])):
                cs = slice(base + c * LANES, base + (c + 1) * LANES)
                xv = x_ref[:, cs] + g2_ref[0, :, cs] * (w0 * p0 + w1 * p1)
                o_ref[:, cs] = xv
                if final_norm:
                    ss = ss + jnp.sum(xv * xv, axis=-1, keepdims=True)
        if final_norm:
            o_ref[...] = o_ref[...] * lax.rsqrt(ss * (1.0 / d) + EPS) * fg_ref[...]

        @pl.when(i == n - 1)
        def _():
            wait_all(1 - s)

    for s in range(2):
        pl.when(i % 2 == s)(functools.partial(step, s))


def _combine_call(x1, ys, dest, wts, g2, final_g, S, final_norm):
    T, D = x1.shape
    tm = min(512, S)
    nt = T // tm
    per_b = S // tm
    n_s = D // (2 * LANES)
    dest3 = dest.reshape(nt, 1, tm * TOP_K)
    kern = functools.partial(_combine_kernel, final_norm=final_norm)
    return pl.pallas_call(
        kern,
        grid=(nt,),
        in_specs=[pl.BlockSpec((1, 1, tm * TOP_K), lambda i: (i, 0, 0), memory_space=pltpu.SMEM),
                  pl.BlockSpec((1, 1, tm * TOP_K), lambda i: (jnp.minimum(i + 1, nt - 1), 0, 0),
                               memory_space=pltpu.SMEM),
                  pl.BlockSpec((tm, D), lambda i: (i, 0)),
                  pl.BlockSpec(memory_space=pl.ANY),
                  pl.BlockSpec((tm, LANES), lambda i: (i, 0)),
                  pl.BlockSpec((1, 1, D), lambda i: (i // per_b, 0, 0)),
                  pl.BlockSpec((1, D), lambda i: (0, 0))],
        out_specs=pl.BlockSpec((tm, D), lambda i: (i, 0)),
        out_shape=jax.ShapeDtypeStruct((T, D), F32),
        scratch_shapes=[pltpu.VMEM((tm * TOP_K * n_s, LANES), jnp.uint32),
                        pltpu.VMEM((tm * TOP_K * n_s, LANES), jnp.uint32),
                        pltpu.SemaphoreType.DMA((2,))],
        compiler_params=_params(("arbitrary",), 48),
        name="moe_combine",
    )(dest3, dest3, x1, ys, wts, g2, final_g)


def _dispatch_plan(routed, cnt, n_experts):
    T = routed.shape[0]
    A = T * TOP_K
    BM = MOE_BLOCK
    eid = routed[:, :TOP_K]
    rank = routed[:, TOP_K:2 * TOP_K]
    counts = cnt[0, :n_experts].astype(jnp.int32)
    padded = (counts + BM - 1) // BM * BM
    pad_end = jnp.cumsum(padded)
    pad_start = pad_end - padded
    onehot = eid[:, :, None] == jnp.arange(n_experts, dtype=jnp.int32)[None, None, :]
    dest = (jnp.sum(jnp.where(onehot, pad_start[None, None, :], 0), axis=-1)
            + rank).reshape(A)
    nb = (A + n_experts * (BM - 1) + BM - 1) // BM
    block_start = jnp.arange(nb, dtype=jnp.int32) * BM
    block_e = jnp.minimum(jnp.sum(block_start[:, None] >= pad_end[None, :], axis=1),
                          n_experts - 1).astype(jnp.int32)
    n_act = (pad_end[-1] // BM).astype(jnp.int32).reshape(1)
    last_block = jnp.maximum(pad_end // BM - 1, 0).astype(jnp.int32)
    ex = jnp.arange(n_experts, dtype=jnp.int32)
    later = (ex[None, :] > ex[:, None]) & (counts[None, :] > 0)
    next_of = jnp.min(jnp.where(later, ex[None, :], n_experts), axis=1)
    next_of = jnp.where(next_of < n_experts, next_of, -1)
    next_e = jnp.sum(jnp.where(block_e[:, None] == ex[None, :], next_of[None, :], 0),
                     axis=1).astype(jnp.int32)
    return dest.astype(jnp.int32), block_e, n_act, next_e, last_block, nb * BM


def kernel(x, c, ada_w, ada_b, norm1_g, w_in, gm_vn_g, gm_vn_b, gm_ws, gm_bs, gla_wa2, gla_ba,
           gla_on_g, w_pa, w_pb, w_out, norm2_g, w_rg, b_rg, w_re, b_re, w_e_gate, w_e_up,
           w_e_down, final_g):
    B, S, D = x.shape
    T = B * S
    L = ada_w.shape[0]
    W = gm_vn_g.shape[1]
    G, C = gm_ws.shape[1], gm_ws.shape[2]
    RANK, DK = gla_wa2.shape[1], gla_wa2.shape[2]
    DV = gla_on_g.shape[1]
    E = w_e_gate.shape[1]
    per_group = E // MOE_GROUPS
    SUB = D // (2 * LANES)
    assert B <= 8 and MOE_GROUPS + E <= LANES and RANK <= LANES

    c8 = jnp.zeros((8, D), F32).at[:B].set(c)
    mod = _ada_call(c8, ada_w, ada_b.reshape(L, 1, 6 * D))

    o_alr = 2 * W + 2 * DK + 2 * DV
    x2 = x.reshape(T, D)
    w_main = _inproj_weight_call(w_in, o_alr, RANK)
    w_alr = jnp.zeros((L, D, LANES), BF16).at[:, :, :RANK].set(
        w_in[:, :, o_alr:o_alr + RANK].astype(BF16))
    w_pa_b, w_pb_b, w_out_b = _cast_call(w_pa), _cast_call(w_pb), _cast_call(w_out)
    for l in range(L):
        sh1, sc1, g1, sh2, sc2, g2 = [mod[l, :B, k * D:(k + 1) * D].reshape(B, 1, D) for k in range(6)]
        wa2p = jnp.zeros((LANES, DK), BF16).at[:RANK].set(gla_wa2[l].astype(BF16))
        bias_full = jnp.repeat(gm_bs[l].T, W // G, axis=1)
        wr = jnp.zeros((D, LANES), BF16).at[:, :MOE_GROUPS + E].set(
            jnp.concatenate([w_rg[l], w_re[l]], axis=1).astype(BF16))
        br = jnp.zeros((1, LANES), F32).at[0, :MOE_GROUPS + E].set(
            jnp.concatenate([b_rg[l], b_re[l]]))

        proj, alr = _inproj_call(x2, norm1_g[l].reshape(1, D), sc1, sh1, w_main, w_alr, S, l)
        ya = _gmlp_call(proj, gm_vn_g[l].reshape(1, W), gm_vn_b[l].reshape(1, W), gm_ws[l], bias_full)
        yb = _gla_call(proj, alr, wa2p, gla_ba[l].reshape(1, DK), gla_on_g[l].reshape(1, DV), B, S, W)
        x1, h2, logits = _merge_call(ya, yb, proj, w_pa_b, w_pb_b, w_out_b, x2, g1,
                                     norm2_g[l].reshape(1, D), sc2, sh2, wr, S, W, l)
        routed, wts, cnt = _router_call(logits, br, per_group)
        dest, block_e, n_act, next_e, last_block, n_rows = _dispatch_plan(routed, cnt, E)
        xs = _dispatch_call(h2, dest, last_block, n_act, n_rows, SUB)
        ys = _experts_call(xs, block_e, n_act, next_e, w_e_gate, w_e_up, w_e_down, l)
        x2 = _combine_call(x1, ys, dest, wts, g2, final_g.reshape(1, D), S, final_norm=(l == L - 1))
    return x2.reshape(B, S, D)
```

```python
import functools

import jax
import jax.numpy as jnp
from jax import lax
from jax.experimental import pallas as pl
from jax.experimental.pallas import tpu as pltpu

F32 = jnp.float32
BF16 = jnp.bfloat16

EPS = 1e-6
GLA_HEADS = 4
GLA_TAU = 16.0
GLA_CHUNK = 64
MOE_GROUPS = 4
TOP_K = 2

LANES = 128
MIB = 1024 * 1024
MOE_BLOCK = 256
ISSUE_UNROLL = 8
GLA_UNROLL = 4

NT_DIMS = (((1,), (1,)), ((), ()))
TN_DIMS = (((0,), (0,)), ((), ()))


def _params(semantics, vmem_mib):
    return pltpu.CompilerParams(dimension_semantics=semantics,
                                vmem_limit_bytes=int(vmem_mib * MIB))


def _dot(a, b):
    return jnp.dot(a, b, preferred_element_type=F32)


def _sigmoid(x):
    return 1.0 / (1.0 + jnp.exp(-x))


def _gelu(x):
    return 0.5 * x * (1.0 + lax.erf(x * (2.0 ** -0.5)))


def _pack_pairs(lo, hi):
    lo_w = pltpu.bitcast(lo.astype(BF16).astype(F32), jnp.uint32) >> 16
    hi_w = pltpu.bitcast(hi.astype(BF16).astype(F32), jnp.uint32) & jnp.uint32(0xFFFF0000)
    return lo_w | hi_w


def _unpack_pairs(w):
    lo = pltpu.bitcast(w << 16, F32)
    hi = pltpu.bitcast(w & jnp.uint32(0xFFFF0000), F32)
    return lo, hi


def _store_packed(ref, x):
    rows, d = x.shape
    half = d // 2
    n_s = half // LANES
    for s in range(n_s):
        lo = x[:, s * LANES:(s + 1) * LANES]
        hi = x[:, half + s * LANES:half + (s + 1) * LANES]
        ref[pl.ds(s, rows, stride=n_s), :] = _pack_pairs(lo, hi)


def _load_packed(ref, rows, n_s, offset=0, group=1):
    los, his = [], []
    for s in range(n_s):
        lo, hi = _unpack_pairs(ref[pl.ds(offset * n_s + s, rows, stride=n_s * group), :])
        los.append(lo)
        his.append(hi)
    return los, his


def _ada_kernel(c_ref, w_ref, b_ref, o_ref):
    c = c_ref[...]
    cond = c * _sigmoid(c)
    o_ref[0] = _dot(cond.astype(BF16), w_ref[0].astype(BF16)) + b_ref[0]


def _ada_call(c8, ada_w, ada_b3):
    L, D, N = ada_w.shape
    tn = min(2048, N)
    return pl.pallas_call(
        _ada_kernel,
        grid=(L, N // tn),
        in_specs=[pl.BlockSpec((8, D), lambda l, j: (0, 0)),
                  pl.BlockSpec((1, D, tn), lambda l, j: (l, 0, j)),
                  pl.BlockSpec((1, 1, tn), lambda l, j: (l, 0, j))],
        out_specs=pl.BlockSpec((1, 8, tn), lambda l, j: (l, 0, j)),
        out_shape=jax.ShapeDtypeStruct((L, 8, N), F32),
        compiler_params=_params(("parallel", "parallel"), 52),
        name="ada_mod",
    )(c8, ada_w, ada_b3)


def _cast_kernel(w_ref, o_ref):
    o_ref[...] = w_ref[...].astype(o_ref.dtype)


def _cast_call(w):
    L, K, N = w.shape
    tn = min(1024, N)
    return pl.pallas_call(
        _cast_kernel,
        grid=(L, N // tn),
        in_specs=[pl.BlockSpec((1, K, tn), lambda l, j: (l, 0, j))],
        out_specs=pl.BlockSpec((1, K, tn), lambda l, j: (l, 0, j)),
        out_shape=jax.ShapeDtypeStruct(w.shape, BF16),
        compiler_params=_params(("parallel", "parallel"), 32),
        name="cast_bf16",
    )(w)


def _inproj_weight_kernel(wt_ref, o_ref):
    o_ref[0] = wt_ref[...].T.astype(BF16)


def _inproj_weight_call(w_in, head_cols, rank):
    L, D, N = w_in.shape
    NM = N - rank
    tn = min(1024, NM)
    n_head = head_cols // tn
    sub = 8
    assert head_cols % tn == 0 and NM % tn == 0 and N % sub == 0 and rank % sub == 0
    wt = jnp.swapaxes(w_in, 1, 2).reshape(L * N, D)

    def first_row(l, j):
        return pl.multiple_of(l * N + j * tn + jnp.where(j >= n_head, rank, 0), sub)

    return pl.pallas_call(
        _inproj_weight_kernel,
        grid=(L, NM // tn),
        in_specs=[pl.BlockSpec((pl.Element(tn), pl.Element(D)), lambda l, j: (first_row(l, j), 0))],
        out_specs=pl.BlockSpec((1, D, tn), lambda l, j: (l, 0, j)),
        out_shape=jax.ShapeDtypeStruct((L, D, NM), BF16),
        compiler_params=_params(("parallel", "parallel"), 48),
        name="in_proj_weight",
    )(wt)


def _inproj_kernel(x_ref, g_ref, sc_ref, sh_ref, w_ref, walr_ref, o_ref, alr_ref, h_scr):
    @pl.when(pl.program_id(1) == 0)
    def _():
        x = x_ref[...]
        ms = jnp.mean(x * x, axis=-1, keepdims=True)
        y = x * lax.rsqrt(ms + EPS) * g_ref[...]
        h = (y * (1.0 + sc_ref[0]) + sh_ref[0]).astype(BF16)
        h_scr[...] = h
        alr_ref[...] = _dot(h, walr_ref[0])

    o_ref[...] = _dot(h_scr[...], w_ref[0]).astype(o_ref.dtype)


def _inproj_call(x2, g, sc, sh, w_main, w_alr, S, l):
    T, D = x2.shape
    NM = w_main.shape[2]
    tm = min(1024, S)
    tn = min(2048, NM)
    per_b = S // tm
    return pl.pallas_call(
        _inproj_kernel,
        grid=(T // tm, NM // tn),
        in_specs=[pl.BlockSpec((tm, D), lambda i, j: (i, 0)),
                  pl.BlockSpec((1, D), lambda i, j: (0, 0)),
                  pl.BlockSpec((1, 1, D), lambda i, j: (i // per_b, 0, 0)),
                  pl.BlockSpec((1, 1, D), lambda i, j: (i // per_b, 0, 0)),
                  pl.BlockSpec((1, D, tn), lambda i, j: (l, 0, j)),
                  pl.BlockSpec((1, D, LANES), lambda i, j: (l, 0, 0))],
        out_specs=[pl.BlockSpec((tm, tn), lambda i, j: (i, j)),
                   pl.BlockSpec((tm, LANES), lambda i, j: (i, 0))],
        out_shape=[jax.ShapeDtypeStruct((T, NM), BF16),
                   jax.ShapeDtypeStruct((T, LANES), F32)],
        scratch_shapes=[pltpu.VMEM((tm, D), BF16)],
        compiler_params=_params(("parallel", "arbitrary"), 56),
        name="in_proj",
    )(x2, g, sc, sh, w_main, w_alr)


def _gmlp_kernel(u_ref, v_ref, g_ref, b_ref, ws_ref, bias_ref, o_ref, vn_scr, *, groups, chunk):
    rows, width = v_ref.shape
    gc = width // groups
    r = lax.broadcasted_iota(jnp.int32, (chunk, chunk), 0)
    c = lax.broadcasted_iota(jnp.int32, (chunk, chunk), 1)
    causal = r >= c
    for ci in range(rows // chunk):
        rs = slice(ci * chunk, (ci + 1) * chunk)
        gv = _gelu(v_ref[rs, :].astype(F32))
        mu = jnp.mean(gv, axis=-1, keepdims=True)
        xc = gv - mu
        var = jnp.mean(xc * xc, axis=-1, keepdims=True)
        vn = xc * lax.rsqrt(var + EPS) * g_ref[...] + b_ref[...]
        vn_scr[...] = vn.astype(BF16)
        for gi in range(groups):
            cs = slice(gi * gc, (gi + 1) * gc)
            wm = jnp.where(causal, ws_ref[gi], 0.0).astype(BF16)
            mixed = _dot(wm, vn_scr[:, cs]) + bias_ref[:, cs]
            gu = _gelu(u_ref[rs, cs].astype(F32))
            o_ref[rs, cs] = (gu * mixed).astype(o_ref.dtype)


def _gmlp_call(proj, vn_g, vn_b, ws, bias_full):
    T = proj.shape[0]
    G, C, _ = ws.shape
    W = vn_g.shape[1]
    R = min(4 * C, T)
    kern = functools.partial(_gmlp_kernel, groups=G, chunk=C)
    return pl.pallas_call(
        kern,
        grid=(T // R,),
        in_specs=[pl.BlockSpec((R, W), lambda i: (i, 0)),
                  pl.BlockSpec((R, W), lambda i: (i, 1)),
                  pl.BlockSpec((1, W), lambda i: (0, 0)),
                  pl.BlockSpec((1, W), lambda i: (0, 0)),
                  pl.BlockSpec((G, C, C), lambda i: (0, 0, 0)),
                  pl.BlockSpec((C, W), lambda i: (0, 0))],
        out_specs=pl.BlockSpec((R, W), lambda i: (i, 0)),
        out_shape=jax.ShapeDtypeStruct((T, W), BF16),
        scratch_shapes=[pltpu.VMEM((C, W), BF16)],
        compiler_params=_params(("parallel",), 32),
        name="gmlp_gate",
    )(proj, proj, vn_g, vn_b, ws, bias_full)


def _gla_kernel(q_ref, k_ref, v_ref, r_ref, alr_ref, wa2_ref, ba_ref, g_ref, o_ref,
                st_ref, la_ref, *, heads):
    rows, dk = q_ref.shape
    dv = v_ref.shape[1]
    hk = dk // heads
    hv = dv // heads
    C = GLA_CHUNK
    scale = hk ** -0.5

    @pl.when(pl.program_id(1) == 0)
    def _():
        st_ref[...] = jnp.zeros(st_ref.shape, F32)

    z = _dot(alr_ref[...].astype(BF16), wa2_ref[...]) + ba_ref[...]
    la_ref[...] = (jnp.minimum(z, 0.0) - jnp.log(1.0 + jnp.exp(-jnp.abs(z)))) * (1.0 / GLA_TAU)

    ri = lax.broadcasted_iota(jnp.int32, (C, C), 0)
    ci = lax.broadcasted_iota(jnp.int32, (C, C), 1)
    causal = ri >= ci
    tri = causal.astype(BF16)

    def chunk_step(c, carry):
        rs = pl.ds(pl.multiple_of(c * C, C), C)
        la = la_ref[rs, :]
        hi = la.astype(BF16)
        r1 = la - hi.astype(F32)
        mid = r1.astype(BF16)
        lo = (r1 - mid.astype(F32)).astype(BF16)
        cum = _dot(tri, hi) + _dot(tri, mid) + _dot(tri, lo)
        cl = cum[C - 1:C, :]
        q = q_ref[rs, :].astype(F32) * scale
        k = k_ref[rs, :].astype(F32)
        qd = (q * jnp.exp(cum)).astype(BF16)
        ki = (k * jnp.exp(-cum)).astype(BF16)
        ke = (k * jnp.exp(cl - cum)).astype(BF16)
        dec = jnp.exp(cl)
        for h in range(heads):
            ks = slice(h * hk, (h + 1) * hk)
            vs = slice(h * hv, (h + 1) * hv)
            vh = v_ref[rs, vs]
            s = lax.dot_general(qd[:, ks], ki[:, ks], NT_DIMS, preferred_element_type=F32)
            s = jnp.where(causal, s, 0.0).astype(BF16)
            st = st_ref[h]
            o = _dot(s, vh) + lax.dot_general(qd[:, ks], st.astype(BF16), NT_DIMS,
                                              preferred_element_type=F32)
            st_ref[h] = st * dec[:, ks] + lax.dot_general(vh, ke[:, ks], TN_DIMS,
                                                          preferred_element_type=F32)
            ms = jnp.mean(o * o, axis=-1, keepdims=True)
            on = o * lax.rsqrt(ms + EPS) * g_ref[:, vs]
            rr = r_ref[rs, vs].astype(F32)
            o_ref[rs, vs] = (on * (rr * _sigmoid(rr))).astype(o_ref.dtype)
        return carry

    lax.fori_loop(0, rows // C, chunk_step, 0, unroll=GLA_UNROLL)


def _gla_call(proj, alr, wa2p, ba, on_g, B, S, W):
    DK = wa2p.shape[1]
    DV = on_g.shape[1]
    H = GLA_HEADS
    Cb = min(1024, S)
    nb = S // Cb
    q_blk = (2 * W) // DK
    k_blk = (2 * W + DK) // DK
    v_blk = (2 * W + 2 * DK) // DV
    r_blk = (2 * W + 2 * DK + DV) // DV
    kern = functools.partial(_gla_kernel, heads=H)
    return pl.pallas_call(
        kern,
        grid=(B, nb),
        in_specs=[pl.BlockSpec((Cb, DK), lambda b, i: (b * nb + i, q_blk)),
                  pl.BlockSpec((Cb, DK), lambda b, i: (b * nb + i, k_blk)),
                  pl.BlockSpec((Cb, DV), lambda b, i: (b * nb + i, v_blk)),
                  pl.BlockSpec((Cb, DV), lambda b, i: (b * nb + i, r_blk)),
                  pl.BlockSpec((Cb, LANES), lambda b, i: (b * nb + i, 0)),
                  pl.BlockSpec((LANES, DK), lambda b, i: (0, 0)),
                  pl.BlockSpec((1, DK), lambda b, i: (0, 0)),
                  pl.BlockSpec((1, DV), lambda b, i: (0, 0))],
        out_specs=pl.BlockSpec((Cb, DV), lambda b, i: (b * nb + i, 0)),
        out_shape=jax.ShapeDtypeStruct((B * S, DV), BF16),
        scratch_shapes=[pltpu.VMEM((H, DV // H, DK // H), F32),
                        pltpu.VMEM((Cb, DK), F32)],
        compiler_params=_params(("parallel", "arbitrary"), 48),
        name="gla",
    )(proj, proj, proj, proj, alr, wa2p, ba, on_g)


def _merge_kernel(ya_ref, yb_ref, ga_ref, gb_ref, wpa_ref, wpb_ref, wout_ref, x_ref, g1_ref,
                  n2_ref, sc_ref, sh_ref, wr_ref, x1_ref, h2_ref, lg_ref):
    a = _dot(ya_ref[...], wpa_ref[0])
    b = _dot(yb_ref[...], wpb_ref[0])
    y = _sigmoid(ga_ref[...].astype(F32)) * a + _sigmoid(gb_ref[...].astype(F32)) * b
    out = _dot(y.astype(BF16), wout_ref[0])
    x1 = x_ref[...] + g1_ref[0] * out
    x1_ref[...] = x1
    ms = jnp.mean(x1 * x1, axis=-1, keepdims=True)
    h2 = (x1 * lax.rsqrt(ms + EPS) * n2_ref[...]) * (1.0 + sc_ref[0]) + sh_ref[0]
    _store_packed(h2_ref, h2)
    lg_ref[...] = _dot(h2.astype(BF16), wr_ref[...])


def _merge_call(ya, yb, proj, wpa, wpb, wout, x2, g1, n2g, sc2, sh2, wr, S, W, l):
    T, D = x2.shape
    ga_blk = (proj.shape[1] - 2 * D) // D
    gb_blk = ga_blk + 1
    tm = min(256, S)
    per_b = S // tm
    n_s = D // (2 * LANES)
    const = dict(pipeline_mode=pl.Buffered(1))
    return pl.pallas_call(
        _merge_kernel,
        grid=(T // tm,),
        in_specs=[pl.BlockSpec((tm, W), lambda i: (i, 0)),
                  pl.BlockSpec((tm, W), lambda i: (i, 0)),
                  pl.BlockSpec((tm, D), lambda i: (i, ga_blk)),
                  pl.BlockSpec((tm, D), lambda i: (i, gb_blk)),
                  pl.BlockSpec((1, W, D), lambda i: (l, 0, 0), **const),
                  pl.BlockSpec((1, W, D), lambda i: (l, 0, 0), **const),
                  pl.BlockSpec((1, D, D), lambda i: (l, 0, 0), **const),
                  pl.BlockSpec((tm, D), lambda i: (i, 0)),
                  pl.BlockSpec((1, 1, D), lambda i: (i // per_b, 0, 0)),
                  pl.BlockSpec((1, D), lambda i: (0, 0)),
                  pl.BlockSpec((1, 1, D), lambda i: (i // per_b, 0, 0)),
                  pl.BlockSpec((1, 1, D), lambda i: (i // per_b, 0, 0)),
                  pl.BlockSpec((D, LANES), lambda i: (0, 0), **const)],
        out_specs=[pl.BlockSpec((tm, D), lambda i: (i, 0)),
                   pl.BlockSpec((tm * n_s, LANES), lambda i: (i, 0)),
                   pl.BlockSpec((tm, LANES), lambda i: (i, 0))],
        out_shape=[jax.ShapeDtypeStruct((T, D), F32),
                   jax.ShapeDtypeStruct((T * n_s, LANES), jnp.uint32),
                   jax.ShapeDtypeStruct((T, LANES), F32)],
        compiler_params=_params(("parallel",), 56),
        name="merge_out",
    )(ya, yb, proj, proj, wpa, wpb, wout, x2, g1, n2g, sc2, sh2, wr)


def _router_kernel(lg_ref, b_ref, e_ref, w_ref, cnt_ref, carry, *, groups, per_group):
    @pl.when(pl.program_id(0) == 0)
    def _():
        carry[...] = jnp.zeros(carry.shape, F32)

    lg = lg_ref[...] + b_ref[...]
    lane = lax.broadcasted_iota(jnp.int32, lg.shape, 1)
    lane_f = lane.astype(F32)
    neg = jnp.float32(-1e30)
    big = jnp.float32(LANES)

    def first_argmax(vals):
        m = jnp.max(vals, axis=-1, keepdims=True)
        idx = jnp.min(jnp.where(vals == m, lane_f, big), axis=-1, keepdims=True)
        return m, idx

    gmask = lane < groups
    gl = jnp.where(gmask, lg, neg)
    gmax, gidx = first_argmax(gl)
    gsum = jnp.sum(jnp.where(gmask, jnp.exp(gl - gmax), 0.0), axis=-1, keepdims=True)
    gtop = 1.0 / gsum

    lo = groups + gidx * per_group
    emask = (lane_f >= lo) & (lane_f < lo + per_group)
    el = jnp.where(emask, lg, neg)
    m1, i1 = first_argmax(el)
    el2 = jnp.where(lane_f == i1, neg, el)
    m2, i2 = first_argmax(el2)
    den = jnp.sum(jnp.where(emask, jnp.exp(el - m1), 0.0), axis=-1, keepdims=True)
    p1 = 1.0 / den
    p2 = jnp.exp(m2 - m1) / den
    ps = p1 + p2
    w1 = gtop * p1 / ps
    w2 = gtop * p2 / ps
    e1 = i1 - groups
    e2 = i2 - groups
    w_ref[...] = jnp.where(lane == 0, w1, jnp.where(lane == 1, w2, 0.0))

    oh0 = lane_f == e1
    oh1 = lane_f == e2
    both = oh0.astype(F32) + oh1.astype(F32)
    tb = lg.shape[0]
    r = lax.broadcasted_iota(jnp.int32, (tb, tb), 0)
    c = lax.broadcasted_iota(jnp.int32, (tb, tb), 1)
    strict = (r > c).astype(BF16)
    before = _dot(strict, both.astype(BF16)) + carry[0:1, :]
    r0 = jnp.sum(jnp.where(oh0, before, 0.0), axis=-1, keepdims=True)
    r1 = jnp.sum(jnp.where(oh1, before, 0.0), axis=-1, keepdims=True)
    packed = jnp.where(lane == 0, e1, jnp.where(lane == 1, e2,
                       jnp.where(lane == 2, r0, jnp.where(lane == 3, r1, 0.0))))
    e_ref[...] = packed.astype(jnp.int32)
    carry[...] = carry[...] + jnp.sum(both, axis=0, keepdims=True)
    cnt_ref[...] = carry[...]


def _router_call(logits, bias, per_group):
    T = logits.shape[0]
    tm = min(512, T)
    kern = functools.partial(_router_kernel, groups=MOE_GROUPS, per_group=per_group)
    return pl.pallas_call(
        kern,
        grid=(T // tm,),
        in_specs=[pl.BlockSpec((tm, LANES), lambda i: (i, 0)),
                  pl.BlockSpec((1, LANES), lambda i: (0, 0))],
        out_specs=[pl.BlockSpec((tm, LANES), lambda i: (i, 0)),
                   pl.BlockSpec((tm, LANES), lambda i: (i, 0)),
                   pl.BlockSpec((8, LANES), lambda i: (0, 0))],
        out_shape=[jax.ShapeDtypeStruct((T, LANES), jnp.int32),
                   jax.ShapeDtypeStruct((T, LANES), F32),
                   jax.ShapeDtypeStruct((8, LANES), F32)],
        scratch_shapes=[pltpu.VMEM((8, LANES), F32)],
        compiler_params=_params(("arbitrary",), 16),
        name="router_topk",
    )(logits, bias)


def _row(ref, t, n_s):
    return ref.at[pl.ds(pl.multiple_of(t * n_s, n_s), n_s)]


def _dispatch_kernel(zr_ref, na_ref, dest_ref, h_ref, out_ref, zbuf, sem, zsem, *, n_s, block):
    n_assign = dest_ref.shape[2]
    tm = n_assign // TOP_K
    nb = out_ref.shape[0] // (block * n_s)

    def zero_block(b):
        start = pl.multiple_of(b * (block * n_s), block * n_s)
        return pltpu.make_async_copy(zbuf, out_ref.at[pl.ds(start, block * n_s)], zsem.at[0])

    @pl.when(pl.program_id(0) == 0)
    def _():
        zbuf[...] = jnp.zeros(zbuf.shape, zbuf.dtype)
        n_exp = zr_ref.shape[0]
        lax.fori_loop(0, n_exp, lambda e, c: (zero_block(zr_ref[e]).start(), c)[1], 0)
        lax.fori_loop(na_ref[0], nb, lambda b, c: (zero_block(b).start(), c)[1], 0)
        lax.fori_loop(0, n_exp, lambda e, c: (zero_block(0).wait(), c)[1], 0)
        lax.fori_loop(na_ref[0], nb, lambda b, c: (zero_block(0).wait(), c)[1], 0)

    tok_unroll = ISSUE_UNROLL // TOP_K

    def issue(g, c):
        for u in range(tok_unroll):
            t = g * tok_unroll + u
            for k in range(TOP_K):
                pltpu.make_async_copy(_row(h_ref, t, n_s),
                                      _row(out_ref, dest_ref[0, 0, t * TOP_K + k], n_s),
                                      sem.at[0]).start(priority=k % 2)
        return c
    lax.fori_loop(0, tm // tok_unroll, issue, 0)
    for _ in range(TOP_K):
        pltpu.make_async_copy(h_ref, out_ref.at[pl.ds(0, tm * n_s)], sem.at[0]).wait()


def _dispatch_call(h2p, dest, zero_rows, n_act, n_rows, n_s):
    T = h2p.shape[0] // n_s
    tm = min(1024, T)
    nt = T // tm
    kern = functools.partial(_dispatch_kernel, n_s=n_s, block=MOE_BLOCK)
    grid_spec = pltpu.PrefetchScalarGridSpec(
        num_scalar_prefetch=2,
        grid=(nt,),
        in_specs=[pl.BlockSpec((1, 1, tm * TOP_K), lambda i, zr, na: (i, 0, 0),
                               memory_space=pltpu.SMEM),
                  pl.BlockSpec((tm * n_s, LANES), lambda i, zr, na: (i, 0))],
        out_specs=pl.BlockSpec(memory_space=pl.ANY),
        scratch_shapes=[pltpu.VMEM((MOE_BLOCK * n_s, LANES), h2p.dtype),
                        pltpu.SemaphoreType.DMA((1,)),
                        pltpu.SemaphoreType.DMA((1,))],
    )
    return pl.pallas_call(
        kern,
        grid_spec=grid_spec,
        out_shape=jax.ShapeDtypeStruct((n_rows * n_s, LANES), h2p.dtype),
        compiler_params=_params(("arbitrary",), 16),
        name="row_dispatch",
    )(zero_rows, n_act, dest.reshape(nt, 1, tm * TOP_K), h2p)


def _stage_expert_weights(layer, be_ref, na_ref, nx_ref, i, hbm_refs, stage, bf_refs, sem):
    e = be_ref[i]
    first = (i == 0) | (e != be_ref[jnp.maximum(i - 1, 0)])

    def copies(expert):
        return [pltpu.make_async_copy(w.at[layer, expert], stage[k], sem.at[k])
                for k, w in enumerate(hbm_refs)]

    @pl.when(i == 0)
    def _():
        for cp in copies(e):
            cp.start()

    @pl.when((i < na_ref[0]) & first)
    def _():
        for k, cp in enumerate(copies(e)):
            cp.wait()
            bf_refs[k][...] = stage[k][...].astype(BF16)
        nxt = nx_ref[i]

        @pl.when(nxt >= 0)
        def _():
            for cp in copies(nxt):
                cp.start()


def _expert_kernel(be_ref, na_ref, nx_ref, xs_ref, wg_hbm, wu_hbm, wd_hbm, ys_ref,
                   sg, su, sd, wgb, wub, wdb, sem, *, layer):
    i = pl.program_id(0)
    active = i < na_ref[0]
    _stage_expert_weights(layer, be_ref, na_ref, nx_ref, i, (wg_hbm, wu_hbm, wd_hbm),
                          (sg, su, sd), (wgb, wub, wdb), sem)

    @pl.when(active)
    def _():
        n_s = sd.shape[1] // (2 * LANES)
        rows = xs_ref.shape[0] // n_s
        los, his = _load_packed(xs_ref, rows, n_s)
        x = jnp.concatenate([p.astype(BF16) for p in los + his], axis=1)
        a = _dot(x, wgb[...])
        b = _dot(x, wub[...])
        hid = (a * _sigmoid(a) * b).astype(BF16)
        _store_packed(ys_ref, _dot(hid, wdb[...]))

    @pl.when(jnp.logical_not(active))
    def _():
        ys_ref[...] = jnp.zeros(ys_ref.shape, ys_ref.dtype)


def _experts_call(xs, block_e, n_act, next_e, w_gate, w_up, w_down, l):
    D, DE = w_gate.shape[-2], w_gate.shape[-1]
    n_s = D // (2 * LANES)
    n_rows = xs.shape[0] // n_s
    BM = MOE_BLOCK
    nb = n_rows // BM
    any_spec = pl.BlockSpec(memory_space=pl.ANY)
    grid_spec = pltpu.PrefetchScalarGridSpec(
        num_scalar_prefetch=3,
        grid=(nb,),
        in_specs=[pl.BlockSpec((BM * n_s, LANES),
                               lambda i, be, na, nx: (jnp.minimum(i, na[0] - 1), 0)),
                  any_spec, any_spec, any_spec],
        out_specs=pl.BlockSpec((BM * n_s, LANES), lambda i, be, na, nx: (i, 0)),
        scratch_shapes=[pltpu.VMEM((D, DE), F32), pltpu.VMEM((D, DE), F32), pltpu.VMEM((DE, D), F32),
                        pltpu.VMEM((D, DE), BF16), pltpu.VMEM((D, DE), BF16),
                        pltpu.VMEM((DE, D), BF16),
                        pltpu.SemaphoreType.DMA((3,))],
    )
    return pl.pallas_call(
        functools.partial(_expert_kernel, layer=l),
        grid_spec=grid_spec,
        out_shape=jax.ShapeDtypeStruct((n_rows * n_s, LANES), jnp.uint32),
        compiler_params=_params(("arbitrary",), 56),
        name="expert_mlp",
    )(block_e, n_act, next_e, xs, w_gate, w_up, w_down)


def _combine_kernel(cur_ref, nxt_ref, x_ref, ys_ref, w_ref, g2_ref, fg_ref, o_ref, ybuf0, ybuf1, sem,
                    *, final_norm):
    i = pl.program_id(0)
    n = pl.num_programs(0)
    tm, d = x_ref.shape
    half = d // 2
    n_s = half // LANES
    n_assign = tm * TOP_K
    bufs = (ybuf0, ybuf1)

    def row_copy(idx_ref, r, s):
        return pltpu.make_async_copy(_row(ys_ref, idx_ref[0, 0, r], n_s), _row(bufs[s], r, n_s),
                                     sem.at[s])

    def wait_all(s):
        pltpu.make_async_copy(ys_ref.at[pl.ds(0, n_assign * n_s)], bufs[s], sem.at[s]).wait()

    @pl.when(i == 0)
    def _():
        def issue(g, c):
            for u in range(ISSUE_UNROLL):
                row_copy(cur_ref, g * ISSUE_UNROLL + u, 0).start(priority=u % 2)
            return c
        lax.fori_loop(0, n_assign // ISSUE_UNROLL, issue, 0)

    def step(s):
        wait_all(s)
        for r in range(n_assign):
            row_copy(nxt_ref, r, 1 - s).start(priority=r % 2)
        yg_ref = bufs[s]
        w = w_ref[...]
        w0 = jnp.broadcast_to(w[:, 0:1], (tm, LANES))
        w1 = jnp.broadcast_to(w[:, 1:2], (tm, LANES))
        lo0, hi0 = _load_packed(yg_ref, tm, n_s, offset=0, group=TOP_K)
        lo1, hi1 = _load_packed(yg_ref, tm, n_s, offset=1, group=TOP_K)
        ss = jnp.zeros((tm, 1), F32)
        for c in range(n_s):
            for base, p0, p1 in ((0, lo0[c], lo1[c]), (half, hi0[c], hi1[c])):
                cs = slice(base + c * LANES, base + (c + 1) * LANES)
                xv = x_ref[:, cs] + g2_ref[0, :, cs] * (w0 * p0 + w1 * p1)
                o_ref[:, cs] = xv
                if final_norm:
                    ss = ss + jnp.sum(xv * xv, axis=-1, keepdims=True)
        if final_norm:
            o_ref[...] = o_ref[...] * lax.rsqrt(ss * (1.0 / d) + EPS) * fg_ref[...]

        @pl.when(i == n - 1)
        def _():
            wait_all(1 - s)

    for s in range(2):
        pl.when(i % 2 == s)(functools.partial(step, s))


def _combine_call(x1, ys, dest, wts, g2, final_g, S, final_norm):
    T, D = x1.shape
    tm = min(512, S)
    nt = T // tm
    per_b = S // tm
    n_s = D // (2 * LANES)
    dest3 = dest.reshape(nt, 1, tm * TOP_K)
    kern = functools.partial(_combine_kernel, final_norm=final_norm)
    return pl.pallas_call(
        kern,
        grid=(nt,),
        in_specs=[pl.BlockSpec((1, 1, tm * TOP_K), lambda i: (i, 0, 0), memory_space=pltpu.SMEM),
                  pl.BlockSpec((1, 1, tm * TOP_K), lambda i: (jnp.minimum(i + 1, nt - 1), 0, 0),
                               memory_space=pltpu.SMEM),
                  pl.BlockSpec((tm, D), lambda i: (i, 0)),
                  pl.BlockSpec(memory_space=pl.ANY),
                  pl.BlockSpec((tm, LANES), lambda i: (i, 0)),
                  pl.BlockSpec((1, 1, D), lambda i: (i // per_b, 0, 0)),
                  pl.BlockSpec((1, D), lambda i: (0, 0))],
        out_specs=pl.BlockSpec((tm, D), lambda i: (i, 0)),
        out_shape=jax.ShapeDtypeStruct((T, D), F32),
        scratch_shapes=[pltpu.VMEM((tm * TOP_K * n_s, LANES), jnp.uint32),
                        pltpu.VMEM((tm * TOP_K * n_s, LANES), jnp.uint32),
                        pltpu.SemaphoreType.DMA((2,))],
        compiler_params=_params(("arbitrary",), 48),
        name="moe_combine",
    )(dest3, dest3, x1, ys, wts, g2, final_g)


def _dispatch_plan(routed, cnt, n_experts):
    T = routed.shape[0]
    A = T * TOP_K
    BM = MOE_BLOCK
    eid = routed[:, :TOP_K]
    rank = routed[:, TOP_K:2 * TOP_K]
    counts = cnt[0, :n_experts].astype(jnp.int32)
    padded = (counts + BM - 1) // BM * BM
    pad_end = jnp.cumsum(padded)
    pad_start = pad_end - padded
    onehot = eid[:, :, None] == jnp.arange(n_experts, dtype=jnp.int32)[None, None, :]
    dest = (jnp.sum(jnp.where(onehot, pad_start[None, None, :], 0), axis=-1)
            + rank).reshape(A)
    nb = (A + n_experts * (BM - 1) + BM - 1) // BM
    block_start = jnp.arange(nb, dtype=jnp.int32) * BM
    block_e = jnp.minimum(jnp.sum(block_start[:, None] >= pad_end[None, :], axis=1),
                          n_experts - 1).astype(jnp.int32)
    n_act = (pad_end[-1] // BM).astype(jnp.int32).reshape(1)
    last_block = jnp.maximum(pad_end // BM - 1, 0).astype(jnp.int32)
    ex = jnp.arange(n_experts, dtype=jnp.int32)
    later = (ex[None, :] > ex[:, None]) & (counts[None, :] > 0)
    next_of = jnp.min(jnp.where(later, ex[None, :], n_experts), axis=1)
    next_of = jnp.where(next_of < n_experts, next_of, -1)
    next_e = jnp.sum(jnp.where(block_e[:, None] == ex[None, :], next_of[None, :], 0),
                     axis=1).astype(jnp.int32)
    return dest.astype(jnp.int32), block_e, n_act, next_e, last_block, nb * BM


def kernel(x, c, ada_w, ada_b, norm1_g, w_in, gm_vn_g, gm_vn_b, gm_ws, gm_bs, gla_wa2, gla_ba,
           gla_on_g, w_pa, w_pb, w_out, norm2_g, w_rg, b_rg, w_re, b_re, w_e_gate, w_e_up,
           w_e_down, final_g):
    B, S, D = x.shape
    T = B * S
    L = ada_w.shape[0]
    W = gm_vn_g.shape[1]
    G, C = gm_ws.shape[1], gm_ws.shape[2]
    RANK, DK = gla_wa2.shape[1], gla_wa2.shape[2]
    DV = gla_on_g.shape[1]
    E = w_e_gate.shape[1]
    per_group = E // MOE_GROUPS
    SUB = D // (2 * LANES)
    assert B <= 8 and MOE_GROUPS + E <= LANES and RANK <= LANES

    c8 = jnp.zeros((8, D), F32).at[:B].set(c)
    mod = _ada_call(c8, ada_w, ada_b.reshape(L, 1, 6 * D))

    o_alr = 2 * W + 2 * DK + 2 * DV
    x2 = x.reshape(T, D)
    w_main = _inproj_weight_call(w_in, o_alr, RANK)
    w_alr = jnp.zeros((L, D, LANES), BF16).at[:, :, :RANK].set(
        w_in[:, :, o_alr:o_alr + RANK].astype(BF16))
    w_pa_b, w_pb_b, w_out_b = _cast_call(w_pa), _cast_call(w_pb), _cast_call(w_out)
    for l in range(L):
        sh1, sc1, g1, sh2, sc2, g2 = [mod[l, :B, k * D:(k + 1) * D].reshape(B, 1, D) for k in range(6)]
        wa2p = jnp.zeros((LANES, DK), BF16).at[:RANK].set(gla_wa2[l].astype(BF16))
        bias_full = jnp.repeat(gm_bs[l].T, W // G, axis=1)
        wr = jnp.zeros((D, LANES), BF16).at[:, :MOE_GROUPS + E].set(
            jnp.concatenate([w_rg[l], w_re[l]], axis=1).astype(BF16))
        br = jnp.zeros((1, LANES), F32).at[0, :MOE_GROUPS + E].set(
            jnp.concatenate([b_rg[l], b_re[l]]))

        proj, alr = _inproj_call(x2, norm1_g[l].reshape(1, D), sc1, sh1, w_main, w_alr, S, l)
        ya = _gmlp_call(proj, gm_vn_g[l].reshape(1, W), gm_vn_b[l].reshape(1, W), gm_ws[l], bias_full)
        yb = _gla_call(proj, alr, wa2p, gla_ba[l].reshape(1, DK), gla_on_g[l].reshape(1, DV), B, S, W)
        x1, h2, logits = _merge_call(ya, yb, proj, w_pa_b, w_pb_b, w_out_b, x2, g1,
                                     norm2_g[l].reshape(1, D), sc2, sh2, wr, S, W, l)
        routed, wts, cnt = _router_call(logits, br, per_group)
        dest, block_e, n_act, next_e, last_block, n_rows = _dispatch_plan(routed, cnt, E)
        xs = _dispatch_call(h2, dest, last_block, n_act, n_rows, SUB)
        ys = _experts_call(xs, block_e, n_act, next_e, w_e_gate, w_e_up, w_e_down, l)
        x2 = _combine_call(x1, ys, dest, wts, g2, final_g.reshape(1, D), S, final_norm=(l == L - 1))
    return x2.reshape(B, S, D)
```
